```python
import math
import jax, jax.numpy as jnp
from jax import lax
import numpy as np

D_MODEL = 2048
BATCH = 2
SEQ = 4096
DEPTH = 4

CHUNK = 64
N_BRANCH = 4
BRANCH_WIDTH = D_MODEL // 2
POOL_WIDTH = BRANCH_WIDTH
POOL_WINDOWS = (2, 4, 8, 16)
POOL_GROUP = POOL_WIDTH // len(POOL_WINDOWS)
SSD_INNER = BRANCH_WIDTH
SSD_HEADDIM = 64
SSD_HEADS = SSD_INNER // SSD_HEADDIM
SSD_STATE = 128
SSD_GROUPS = 4
SSD_HPG = SSD_HEADS // SSD_GROUPS
SSD_CONV = 4
SSD_CONV_DIM = SSD_INNER + 2 * SSD_GROUPS * SSD_STATE
SSD_NORM_GROUPS = SSD_GROUPS
HGRN_KDIM = 128
HGRN_HEADS = BRANCH_WIDTH // HGRN_KDIM
HGRN_VDIM = BRANCH_WIDTH // HGRN_HEADS
HGRN_KEY_WIDTH = HGRN_HEADS * HGRN_KDIM
HGRN_VAL_WIDTH = HGRN_HEADS * HGRN_VDIM
MEM_TOKENS = 256
MEM_HEADS = 4
MEM_WIDTH = BRANCH_WIDTH
MEM_HEAD_DIM = MEM_WIDTH // MEM_HEADS
IN_SPLITS = (POOL_WIDTH, SSD_INNER, SSD_CONV_DIM, SSD_HEADS,
             HGRN_KEY_WIDTH, HGRN_KEY_WIDTH, HGRN_VAL_WIDTH, HGRN_VAL_WIDTH,
             MEM_WIDTH, N_BRANCH * D_MODEL)
IN_COLS = sum(IN_SPLITS)
N_EXPERTS = 64
TOP_K = 8
N_GROUPS = 8
TOPK_GROUPS = 4
D_EXPERT = 256
D_SHARED = 256
ROUTED_SCALE = 2.5
EXPERT_BLOCK = 128
MASK_SCORE = -1e4
ALPHA = (2 * DEPTH) ** 0.25
BETA = (8 * DEPTH) ** -0.25
EPS = 1e-5

kernel_name = "hybrid_pool_ssd_hgrn2_mem_moe_deepnorm"

f32 = jnp.float32


def layer_norm(x, g, b):
    xf = x.astype(f32)
    mu = jnp.mean(xf, -1, keepdims=True)
    var = jnp.mean(jnp.square(xf - mu), -1, keepdims=True)
    return ((xf - mu) * lax.rsqrt(var + EPS) * g.astype(f32) + b.astype(f32)).astype(x.dtype)


def rms_norm(x, w):
    xf = x.astype(f32)
    return xf * lax.rsqrt(jnp.mean(jnp.square(xf), -1, keepdims=True) + EPS) * w.astype(f32)


def pool_mixer(u, pool_w, pool_scale):
    bsz, L, _ = u.shape
    uf = u.astype(f32)
    cs = jnp.concatenate([jnp.zeros((bsz, 1, POOL_WIDTH), f32), jnp.cumsum(uf, axis=1)], axis=1)
    t = jnp.arange(L)
    outs = []
    for gi, w in enumerate(POOL_WINDOWS):
        sl = slice(gi * POOL_GROUP, (gi + 1) * POOL_GROUP)
        lo = jnp.maximum(t + 1 - w, 0)
        win_sum = cs[:, 1:, sl] - cs[:, lo, sl]
        cnt = jnp.minimum(t + 1, w).astype(f32)
        d = win_sum / cnt[None, :, None] - uf[:, :, sl]
        outs.append(d @ pool_w[gi].astype(f32))
    y = jnp.concatenate(outs, axis=-1) * pool_scale.astype(f32)
    return y.astype(u.dtype)


def causal_depthwise_conv(u, w, b):
    k = w.shape[0]
    y = lax.conv_general_dilated(u, w[:, None, :].astype(u.dtype), window_strides=(1,),
                                 padding=[(k - 1, 0)], dimension_numbers=("NWC", "WIO", "NWC"),
                                 feature_group_count=u.shape[-1])
    return y + b.astype(u.dtype)


def decay_matrix(a):
    T = a.shape[-1]
    cs = jnp.cumsum(a, axis=-1)
    diff = cs[..., :, None] - cs[..., None, :]
    mask = jnp.tril(jnp.ones((T, T), bool))
    return jnp.where(mask, jnp.exp(jnp.where(mask, diff, 0.0)), 0.0)


def ssd_mixer(z, xbc, dt_raw, conv_w, conv_b, dt_bias, a_log, d_skip, norm_w):
    bsz, L, _ = z.shape
    nc = L // CHUNK
    xbc = jax.nn.silu(causal_depthwise_conv(xbc, conv_w, conv_b)).astype(f32)
    xs = xbc[..., :SSD_INNER].reshape(bsz, L, SSD_HEADS, SSD_HEADDIM)
    bm = xbc[..., SSD_INNER:SSD_INNER + SSD_GROUPS * SSD_STATE].reshape(bsz, L, SSD_GROUPS, SSD_STATE)
    cm = xbc[..., SSD_INNER + SSD_GROUPS * SSD_STATE:].reshape(bsz, L, SSD_GROUPS, SSD_STATE)
    bm = jnp.repeat(bm, SSD_HPG, axis=2)
    cm = jnp.repeat(cm, SSD_HPG, axis=2)
    dt = jax.nn.softplus(dt_raw.astype(f32) + dt_bias.astype(f32))
    a = -jnp.exp(a_log.astype(f32))
    X = (xs * dt[..., None]).reshape(bsz, nc, CHUNK, SSD_HEADS, SSD_HEADDIM)
    Bc = bm.reshape(bsz, nc, CHUNK, SSD_HEADS, SSD_STATE)
    Cc = cm.reshape(bsz, nc, CHUNK, SSD_HEADS, SSD_STATE)
    Ad = (dt * a).reshape(bsz, nc, CHUNK, SSD_HEADS).transpose(0, 3, 1, 2)
    A_cs = jnp.cumsum(Ad, axis=-1)
    scores = jnp.einsum('bclhn,bcshn->bhcls', Cc, Bc) * decay_matrix(Ad)
    y_diag = jnp.einsum('bhcls,bcshp->bclhp', scores, X)
    decay_states = jnp.exp(A_cs[..., -1:] - A_cs).transpose(0, 2, 3, 1)
    states = jnp.einsum('bclhn,bclhp->bchpn', Bc * decay_states[..., None], X)
    states = jnp.concatenate([jnp.zeros_like(states[:, :1]), states], axis=1)
    chunk_decay = decay_matrix(jnp.pad(A_cs[..., -1], ((0, 0), (0, 0), (1, 0))))
    prev_states = jnp.einsum('bhzc,bchpn->bzhpn', chunk_decay, states)[:, :-1]
    y_off = jnp.einsum('bclhn,bchpn->bclhp', Cc, prev_states) * jnp.exp(A_cs).transpose(0, 2, 3, 1)[..., None]
    y = (y_diag + y_off).reshape(bsz, L, SSD_HEADS, SSD_HEADDIM) + xs * d_skip.astype(f32)[:, None]
    y = y.reshape(bsz, L, SSD_INNER) * jax.nn.silu(z.astype(f32))
    y = rms_norm(y.reshape(bsz, L, SSD_NORM_GROUPS, SSD_INNER // SSD_NORM_GROUPS),
                 norm_w.reshape(SSD_NORM_GROUPS, SSD_INNER // SSD_NORM_GROUPS))
    return y.reshape(bsz, L, SSD_INNER).astype(z.dtype)


def hgrn2_mixer(q, f, i, g, lb, norm_w):
    bsz, L, _ = q.shape
    nc = L // CHUNK
    H, dk, dv = HGRN_HEADS, HGRN_KDIM, HGRN_VDIM
    qf = jax.nn.silu(q.astype(f32))
    ff = f.astype(f32)
    lbf = lb.astype(f32)
    kf = (1.0 - lbf) * jax.nn.sigmoid(-ff)
    log_forget = jnp.log1p(-kf)
    vf = i.astype(f32)

    def to_chunks(t, d):
        return t.reshape(bsz, nc, CHUNK, H, d).transpose(1, 0, 3, 2, 4)

    xs = (to_chunks(qf, dk), to_chunks(kf, dk), to_chunks(vf, dv), to_chunks(log_forget, dk))
    causal = jnp.tril(jnp.ones((CHUNK, CHUNK), bool))[None, None, :, :, None]

    def step(S, inp):
        qc, kc, vc, lf = inp
        b = jnp.cumsum(lf, axis=2)
        o_inter = jnp.einsum('bhtk,bhkv->bhtv', qc * jnp.exp(b), S)
        diff = b[:, :, :, None, :] - b[:, :, None, :, :]
        decay = jnp.where(causal, jnp.exp(jnp.where(causal, diff, 0.0)), 0.0)
        att = jnp.einsum('bhtk,bhsk,bhtsk->bhts', qc, kc, decay)
        o_intra = jnp.einsum('bhts,bhsv->bhtv', att, vc)
        b_last = b[:, :, -1]
        S_new = jnp.exp(b_last)[..., None] * S + jnp.einsum(
            'bhsk,bhsv->bhkv', kc * jnp.exp(b_last[:, :, None] - b), vc)
        return S_new, o_inter + o_intra

    S0 = jnp.zeros((bsz, H, dk, dv), f32)
    _, o = lax.scan(step, S0, xs)
    o = o.transpose(1, 0, 3, 2, 4).reshape(bsz, L, H, dv)
    o = rms_norm(o, norm_w) * jax.nn.silu(g.astype(f32).reshape(bsz, L, H, dv))
    return o.reshape(bsz, L, HGRN_VAL_WIDTH).astype(q.dtype)


def memory_attention(xq, mem, w_mem_kv):
    bsz, L, _ = xq.shape
    q = xq.reshape(bsz, L, MEM_HEADS, MEM_HEAD_DIM)
    kv = mem @ w_mem_kv
    k = kv[..., :MEM_WIDTH].reshape(bsz, -1, MEM_HEADS, MEM_HEAD_DIM)
    v = kv[..., MEM_WIDTH:].reshape(bsz, -1, MEM_HEADS, MEM_HEAD_DIM)
    s = jnp.einsum('blhd,bmhd->bhlm', q, k).astype(f32) * (MEM_HEAD_DIM ** -0.5)
    p = jax.nn.softmax(s, axis=-1).astype(v.dtype)
    return jnp.einsum('bhlm,bmhd->blhd', p, v).reshape(bsz, L, MEM_WIDTH)


def moe_ffn(h, w_router, router_bias, w_gate, w_up, w_down, ws_gate, ws_up, ws_down):
    N, D = h.shape
    E = N_EXPERTS
    scores = jax.nn.sigmoid((h @ w_router).astype(f32))
    choice = scores + router_bias.astype(f32)
    grp_score = lax.top_k(choice.reshape(N, N_GROUPS, E // N_GROUPS), 2)[0].sum(-1)
    _, gidx = lax.top_k(grp_score, TOPK_GROUPS)
    gmask = jax.nn.one_hot(gidx, N_GROUPS, dtype=f32).sum(1)
    emask = jnp.repeat(gmask, E // N_GROUPS, axis=1) > 0
    _, eidx = lax.top_k(jnp.where(emask, choice, MASK_SCORE), TOP_K)
    wts = jnp.take_along_axis(scores, eidx, axis=1)
    wts = wts / jnp.sum(wts, -1, keepdims=True) * ROUTED_SCALE
    NK = N * TOP_K
    NB = -(-NK // EXPERT_BLOCK) + E
    R = NB * EXPERT_BLOCK
    flat_e = eidx.reshape(NK)
    flat_tok = jnp.repeat(jnp.arange(N, dtype=jnp.int32), TOP_K)
    flat_w = wts.reshape(NK)
    order = jnp.argsort(flat_e)
    se = flat_e[order]
    counts = jnp.bincount(flat_e, length=E)
    starts = jnp.cumsum(counts) - counts
    rank = jnp.arange(NK) - starts[se]
    padded = (counts + EXPERT_BLOCK - 1) // EXPERT_BLOCK * EXPERT_BLOCK
    pend = jnp.cumsum(padded)
    dest = (pend - padded)[se] + rank
    row_tok = jnp.zeros((R,), jnp.int32).at[dest].set(flat_tok[order])
    row_w = jnp.zeros((R,), f32).at[dest].set(flat_w[order])
    block_e = jnp.minimum(jnp.searchsorted(pend, jnp.arange(NB) * EXPERT_BLOCK, side='right'), E - 1)

    def expert_block(args):
        e, tok, wr = args
        xb = h[tok]
        y = (jax.nn.silu(xb @ w_gate[e]) * (xb @ w_up[e])) @ w_down[e]
        return y * wr[:, None].astype(h.dtype)

    ys = lax.map(expert_block, (block_e, row_tok.reshape(NB, EXPERT_BLOCK), row_w.reshape(NB, EXPERT_BLOCK)))
    routed = jax.ops.segment_sum(ys.reshape(R, D), row_tok, num_segments=N)
    shared = (jax.nn.silu(h @ ws_gate) * (h @ ws_up)) @ ws_down
    return routed + shared


def hgrn_lower_bounds(lb_param):
    sm = jax.nn.softmax(lb_param.astype(f32), axis=0)
    return jnp.cumsum(sm, axis=0) - sm[0:1]


def setup_inputs(seed: int = 0) -> dict:
    key = jax.random.key(seed)
    ks = iter(jax.random.split(key, 40))

    def nrm(shape, scale):
        return jax.random.normal(next(ks), shape, f32) * scale

    dt0 = jnp.exp(jax.random.uniform(next(ks), (DEPTH, SSD_HEADS), f32)
                  * (math.log(0.1) - math.log(0.001)) + math.log(0.001))
    return {
        "x": nrm((BATCH, SEQ, D_MODEL), 1.0),
        "mem": nrm((BATCH, MEM_TOKENS, D_MODEL), 1.0),
        "w_in": nrm((DEPTH, D_MODEL, IN_COLS), D_MODEL ** -0.5),
        "conv_w": nrm((DEPTH, SSD_CONV, SSD_CONV_DIM), SSD_CONV ** -0.5),
        "conv_b": nrm((DEPTH, SSD_CONV_DIM), 0.02),
        "dt_bias": dt0 + jnp.log(-jnp.expm1(-dt0)),
        "a_log": jnp.log(jax.random.uniform(next(ks), (DEPTH, SSD_HEADS), f32, 1.0, 16.0)),
        "d_skip": 1.0 + nrm((DEPTH, SSD_HEADS), 0.1),
        "ssd_norm": 1.0 + nrm((DEPTH, SSD_INNER), 0.02),
        "pool_w": nrm((DEPTH, len(POOL_WINDOWS), POOL_GROUP, POOL_GROUP), POOL_GROUP ** -0.5),
        "pool_scale": 1.0 + nrm((DEPTH, POOL_WIDTH), 0.02),
        "hgrn_lb": nrm((DEPTH, HGRN_KEY_WIDTH), 0.1),
        "hgrn_norm": 1.0 + nrm((DEPTH, HGRN_VDIM), 0.02),
        "w_mem_kv": nrm((DEPTH, D_MODEL, 2 * MEM_WIDTH), D_MODEL ** -0.5),
        "w_branch": nrm((DEPTH, N_BRANCH, BRANCH_WIDTH, D_MODEL), BRANCH_WIDTH ** -0.5),
        "w_o": nrm((DEPTH, D_MODEL, D_MODEL), D_MODEL ** -0.5 * BETA),
        "ln1_g": 1.0 + nrm((DEPTH, D_MODEL), 0.02),
        "ln1_b": nrm((DEPTH, D_MODEL), 0.02),
        "w_router": nrm((DEPTH, D_MODEL, N_EXPERTS), D_MODEL ** -0.5),
        "router_bias": nrm((DEPTH, N_EXPERTS), 0.01),
        "w_exp_gate": nrm((DEPTH, N_EXPERTS, D_MODEL, D_EXPERT), D_MODEL ** -0.5),
        "w_exp_up": nrm((DEPTH, N_EXPERTS, D_MODEL, D_EXPERT), D_MODEL ** -0.5),
        "w_exp_down": nrm((DEPTH, N_EXPERTS, D_EXPERT, D_MODEL), D_EXPERT ** -0.5 * BETA),
        "w_sh_gate": nrm((DEPTH, D_MODEL, D_SHARED), D_MODEL ** -0.5),
        "w_sh_up": nrm((DEPTH, D_MODEL, D_SHARED), D_MODEL ** -0.5),
        "w_sh_down": nrm((DEPTH, D_SHARED, D_MODEL), D_SHARED ** -0.5 * BETA),
        "ln2_g": 1.0 + nrm((DEPTH, D_MODEL), 0.02),
        "ln2_b": nrm((DEPTH, D_MODEL), 0.02),
    }


def reference(x, mem, w_in, conv_w, conv_b, dt_bias, a_log, d_skip, ssd_norm, pool_w, pool_scale,
              hgrn_lb, hgrn_norm, w_mem_kv, w_branch, w_o, ln1_g, ln1_b, w_router, router_bias,
              w_exp_gate, w_exp_up, w_exp_down, w_sh_gate, w_sh_up, w_sh_down, ln2_g, ln2_b):
    bsz, L, D = x.shape
    lb_all = hgrn_lower_bounds(hgrn_lb)
    split_at = np.cumsum(IN_SPLITS)[:-1].tolist()
    h = x
    for l in range(DEPTH):
        proj = h @ w_in[l]
        (u_pool, z_ssd, xbc_ssd, dt_ssd, q_h, f_h, i_h, g_h, q_mem, gate_logits) = jnp.split(proj, split_at, axis=-1)
        branches = (
            pool_mixer(u_pool, pool_w[l], pool_scale[l]),
            ssd_mixer(z_ssd, xbc_ssd, dt_ssd, conv_w[l], conv_b[l], dt_bias[l], a_log[l], d_skip[l], ssd_norm[l]),
            hgrn2_mixer(q_h, f_h, i_h, g_h, lb_all[l], hgrn_norm[l]),
            memory_attention(q_mem, mem, w_mem_kv[l]),
        )
        gates = jax.nn.sigmoid(gate_logits.astype(f32)).reshape(bsz, L, N_BRANCH, D).astype(h.dtype)
        merged = gates[:, :, 0] * (branches[0] @ w_branch[l, 0])
        for n in range(1, N_BRANCH):
            merged = merged + gates[:, :, n] * (branches[n] @ w_branch[l, n])
        h = layer_norm(ALPHA * h + merged @ w_o[l], ln1_g[l], ln1_b[l])
        ff = moe_ffn(h.reshape(bsz * L, D), w_router[l], router_bias[l], w_exp_gate[l], w_exp_up[l],
                     w_exp_down[l], w_sh_gate[l], w_sh_up[l], w_sh_down[l]).reshape(bsz, L, D)
        h = layer_norm(ALPHA * h + ff, ln2_g[l], ln2_b[l])
    return h
```

```python
import functools

import jax
import jax.numpy as jnp
from jax import lax
from jax.experimental import pallas as pl
from jax.experimental.pallas import tpu as pltpu

f32 = jnp.float32
bf16 = jnp.bfloat16

D_MODEL = 2048
DEPTH = 4
CHUNK = 64
N_BRANCH = 4
BRANCH_WIDTH = 1024
POOL_WINDOWS = (2, 4, 8, 16)
POOL_GROUP = 256
SSD_HEADS = 16
SSD_HEADDIM = 64
SSD_STATE = 128
SSD_GROUPS = 4
SSD_CONV = 4
SSD_CONV_DIM = 2048
HGRN_HEADS = 8
HGRN_DIM = 128
MEM_TOKENS = 256
MEM_HEADS = 4
MEM_HEAD_DIM = 256
N_EXPERTS = 64
TOP_K = 8
N_GROUPS = 8
TOPK_GROUPS = 4
D_EXPERT = 256
D_SHARED = 256
ROUTED_SCALE = 2.5
MASK_SCORE = -1e4
ALPHA = (2 * DEPTH) ** 0.25
EPS = 1e-5

LANES = 128
ROW_TILES = D_MODEL // LANES
EXPERT_ROWS = 128
SUB = 16
VMEM_LIMIT = 56 * 1024 * 1024

NEG_INF = float("-inf")


def _cparams(n_axes):
    return pltpu.CompilerParams(dimension_semantics=("arbitrary",) * n_axes,
                                vmem_limit_bytes=VMEM_LIMIT)


def _sigmoid(x):
    return jax.nn.sigmoid(x)


def _silu(x):
    return x * jax.nn.sigmoid(x)


def _split3(x):
    hi = x.astype(bf16)
    r1 = x - hi.astype(f32)
    mid = r1.astype(bf16)
    lo = (r1 - mid.astype(f32)).astype(bf16)
    return hi, mid, lo


def _dot(a, b):
    return jnp.dot(a, b, preferred_element_type=f32)


def _dot_nt(a, b):
    return lax.dot_general(a, b, (((1,), (1,)), ((), ())), preferred_element_type=f32)


def _dot_tn(a, b):
    return lax.dot_general(a, b, (((0,), (0,)), ((), ())), preferred_element_type=f32)


def _sel_left(sel01, x):
    return sum(_dot(sel01, p) for p in _split3(x))


def _sel_right(x, sel01):
    return sum(_dot(p, sel01) for p in _split3(x))


def _mm_kernel(x_ref, w_ref, o_ref):
    o_ref[...] = _dot(x_ref[...], w_ref[...]).astype(o_ref.dtype)


def _matmul(x, w, out_dtype, tm, tn):
    m, k = x.shape
    n = w.shape[1]
    return pl.pallas_call(
        _mm_kernel,
        grid=(n // tn, m // tm),
        in_specs=[pl.BlockSpec((tm, k), lambda j, i: (i, 0)),
                  pl.BlockSpec((k, tn), lambda j, i: (0, j))],
        out_specs=pl.BlockSpec((tm, tn), lambda j, i: (i, j)),
        out_shape=jax.ShapeDtypeStruct((m, n), out_dtype),
        compiler_params=_cparams(2),
        name="matmul",
    )(x, w)


POOL_T = 512
POOL_HALO = 16


def _pool_kernel(u_ref, halo_ref, w_ref, scale_ref, o_ref):
    i = pl.program_id(1)
    u = u_ref[...]
    halo = jnp.where(i > 0, halo_ref[...], 0.0)
    ext = jnp.concatenate([halo, u], axis=0)
    t = i * POOL_T + lax.broadcasted_iota(jnp.int32, (POOL_T, 1), 0)
    outs = []
    for gi, w in enumerate(POOL_WINDOWS):
        xg = ext[:, gi * POOL_GROUP:(gi + 1) * POOL_GROUP]
        s = xg
        span = 1
        while span < w:
            s = s[span:, :] + s[:-span, :]
            span *= 2
        s = s[s.shape[0] - POOL_T:, :]
        cnt = jnp.minimum(t + 1, w).astype(f32)
        d = s / cnt - u[:, gi * POOL_GROUP:(gi + 1) * POOL_GROUP]
        outs.append(_dot(d.astype(bf16), w_ref[gi]))
    y = jnp.concatenate(outs, axis=-1) * scale_ref[...]
    o_ref[...] = y.astype(o_ref.dtype)


def _pool_mixer(proj, pool_w, pool_scale, bsz, seq):
    n = bsz * seq
    nt = seq // POOL_T
    return pl.pallas_call(
        _pool_kernel,
        grid=(bsz, nt),
        in_specs=[
            pl.BlockSpec((POOL_T, BRANCH_WIDTH), lambda b, i: (b * nt + i, 0)),
            pl.BlockSpec((POOL_HALO, BRANCH_WIDTH),
                         lambda b, i: (jnp.maximum((b * nt + i) * (POOL_T // POOL_HALO) - 1, 0), 0)),
            pl.BlockSpec((len(POOL_WINDOWS), POOL_GROUP, POOL_GROUP), lambda b, i: (0, 0, 0)),
            pl.BlockSpec((1, BRANCH_WIDTH), lambda b, i: (0, 0)),
        ],
        out_specs=pl.BlockSpec((POOL_T, BRANCH_WIDTH), lambda b, i: (b * nt + i, 0)),
        out_shape=jax.ShapeDtypeStruct((n, BRANCH_WIDTH), bf16),
        compiler_params=_cparams(2),
        name="pool_mixer",
    )(proj, proj, pool_w, pool_scale)


CONV_TAIL = 8


def _softplus(x):
    return jnp.maximum(x, 0.0) + jnp.log1p(jnp.exp(-jnp.abs(x)))


def _ssd_kernel(z_ref, xbc_ref, dt_ref, cw_ref, cb_ref, dtb_ref, alog_ref, dskip_ref, nw_ref,
                tri_ref, trit_ref, exp_ref, o_ref, state_ref, tail_ref):
    i = pl.program_id(1)

    @pl.when(i == 0)
    def _():
        state_ref[...] = jnp.zeros_like(state_ref)
        tail_ref[...] = jnp.zeros_like(tail_ref)

    xbc = xbc_ref[...]
    ext = jnp.concatenate([tail_ref[...], xbc], axis=0)
    cw = cw_ref[...]
    conv = cb_ref[...]
    for k in range(SSD_CONV):
        off = CONV_TAIL - (SSD_CONV - 1) + k
        conv = conv + cw[k:k + 1, :] * ext[off:off + CHUNK, :]
    tail_ref[...] = xbc[CHUNK - CONV_TAIL:, :]
    act = _silu(conv)
    inner = SSD_HEADS * SSD_HEADDIM
    gw = SSD_GROUPS * SSD_STATE
    xs = act[:, :inner]
    bmat = act[:, inner:inner + gw]
    cmat = act[:, inner + gw:]

    dt = _softplus(dt_ref[...] + dtb_ref[...])
    a = -jnp.exp(alog_ref[...])
    ad = dt * a
    ad_parts = _split3(ad)
    tri = tri_ref[...]
    acs = sum(_dot(tri, p) for p in ad_parts)
    acs_t = sum(_dot_tn(p, trit_ref[...]) for p in ad_parts)
    a_last = acs[CHUNK - 1:CHUNK, :]
    expand = exp_ref[...]
    dt_e = _sel_right(dt, expand)
    eacs_e = _sel_right(jnp.exp(acs), expand)
    dec_e = _sel_right(jnp.exp(a_last - acs), expand)
    x = xs * dt_e
    xb = x.astype(bf16)
    xd = (x * dec_e).astype(bf16)
    elast_e = eacs_e[CHUNK - 1:CHUNK, :]

    row = lax.broadcasted_iota(jnp.int32, (CHUNK, CHUNK), 0)
    col = lax.broadcasted_iota(jnp.int32, (CHUNK, CHUNK), 1)
    causal = row >= col
    hpg = SSD_HEADS // SSD_GROUPS
    gch = hpg * SSD_HEADDIM
    y_diag = []
    y_off = []
    for g in range(SSD_GROUPS):
        bg = bmat[:, g * SSD_STATE:(g + 1) * SSD_STATE].astype(bf16)
        cg = cmat[:, g * SSD_STATE:(g + 1) * SSD_STATE].astype(bf16)
        cb = _dot_nt(cg, bg)
        for hh in range(hpg):
            h = g * hpg + hh
            diff = acs[:, h:h + 1] - acs_t[h:h + 1, :]
            decay = jnp.where(causal, jnp.exp(jnp.where(causal, diff, 0.0)), 0.0)
            sc = (cb * decay).astype(bf16)
            y_diag.append(_dot(sc, xb[:, h * SSD_HEADDIM:(h + 1) * SSD_HEADDIM]))
        st = state_ref[:, g * gch:(g + 1) * gch]
        y_off.append(_dot(cg, st.astype(bf16)))
        upd = _dot_tn(bg, xd[:, g * gch:(g + 1) * gch])
        state_ref[:, g * gch:(g + 1) * gch] = st * elast_e[:, g * gch:(g + 1) * gch] + upd
    y = (jnp.concatenate(y_diag, axis=-1) + jnp.concatenate(y_off, axis=-1) * eacs_e
         + xs * dskip_ref[...])
    y = y * _silu(z_ref[...])
    nw = nw_ref[...]
    ngw = inner // SSD_GROUPS
    outs = []
    for g in range(SSD_GROUPS):
        yg = y[:, g * ngw:(g + 1) * ngw]
        ms = jnp.mean(yg * yg, axis=-1, keepdims=True)
        outs.append(yg * lax.rsqrt(ms + EPS) * nw[:, g * ngw:(g + 1) * ngw])
    o_ref[...] = jnp.concatenate(outs, axis=-1).astype(o_ref.dtype)


def _ssd_mixer(proj, dtp, conv_w, conv_b, dt_bias, a_log, d_skip_e, norm_w, consts, bsz, seq):
    n = bsz * seq
    nc = seq // CHUNK
    tri, trit, expand = consts
    rowmap = lambda b, i: (b * nc + i, 0)
    const2 = lambda b, i: (0, 0)
    inner = SSD_HEADS * SSD_HEADDIM
    return pl.pallas_call(
        _ssd_kernel,
        grid=(bsz, nc),
        in_specs=[
            pl.BlockSpec((CHUNK, inner), lambda b, i: (b * nc + i, 1)),
            pl.BlockSpec((CHUNK, SSD_CONV_DIM), lambda b, i: (b * nc + i, 1)),
            pl.BlockSpec((CHUNK, LANES), rowmap),
            pl.BlockSpec((SSD_CONV, SSD_CONV_DIM), const2),
            pl.BlockSpec((1, SSD_CONV_DIM), const2),
            pl.BlockSpec((1, LANES), const2),
            pl.BlockSpec((1, LANES), const2),
            pl.BlockSpec((1, inner), const2),
            pl.BlockSpec((1, inner), const2),
            pl.BlockSpec((CHUNK, CHUNK), const2),
            pl.BlockSpec((CHUNK, CHUNK), const2),
            pl.BlockSpec((LANES, inner), const2),
        ],
        out_specs=pl.BlockSpec((CHUNK, inner), rowmap),
        out_shape=jax.ShapeDtypeStruct((n, inner), bf16),
        scratch_shapes=[pltpu.VMEM((SSD_STATE, inner), f32),
                        pltpu.VMEM((CONV_TAIL, SSD_CONV_DIM), f32)],
        compiler_params=_cparams(2),
        name="ssd_mixer",
    )(proj, proj, dtp, conv_w, conv_b, dt_bias, a_log, d_skip_e, norm_w, tri, trit, expand)


def _hgrn_kernel(q_ref, f_ref, i_ref, g_ref, lb_ref, nw_ref, tri_ref, hsum_ref, hexp_ref, o_ref, state_ref):
    c = pl.program_id(1)

    @pl.when(c == 0)
    def _():
        state_ref[...] = jnp.zeros_like(state_ref)

    width = HGRN_HEADS * HGRN_DIM
    qf = _silu(q_ref[...])
    kf = (1.0 - lb_ref[...]) * _sigmoid(-f_ref[...])
    lf = jnp.log1p(-kf)
    v = i_ref[...]
    b = _sel_left(tri_ref[...], lf)
    b_last = b[CHUNK - 1:CHUNK, :]
    qd = (qf * jnp.exp(b)).astype(bf16)
    kd = (kf * jnp.exp(b_last - b)).astype(bf16)
    eb_last = jnp.exp(b_last)
    vb = v.astype(bf16)

    o_inter = []
    for h in range(HGRN_HEADS):
        sl = slice(h * HGRN_DIM, (h + 1) * HGRN_DIM)
        st = state_ref[h]
        o_inter.append(_dot_nt(qd[:, sl], st.astype(bf16)))
        state_ref[h] = st * eb_last[:, sl] + _dot_tn(vb[:, sl], kd[:, sl])
    o = jnp.concatenate(o_inter, axis=-1)

    rows = [jnp.zeros((SUB, width), f32)]
    for blk in range(1, CHUNK // SUB):
        r0 = blk * SUB
        anchor = b[r0 - 1:r0, :]
        qq = (qf[r0:r0 + SUB, :] * jnp.exp(b[r0:r0 + SUB, :] - anchor)).astype(bf16)
        kk = (kf[:r0, :] * jnp.exp(anchor - b[:r0, :])).astype(bf16)
        parts = []
        for h in range(HGRN_HEADS):
            sl = slice(h * HGRN_DIM, (h + 1) * HGRN_DIM)
            att = _dot_nt(qq[:, sl], kk[:, sl])
            parts.append(_dot(att.astype(bf16), vb[:r0, sl]))
        rows.append(jnp.concatenate(parts, axis=-1))
    o = o + jnp.concatenate(rows, axis=0)

    tpos = lax.broadcasted_iota(jnp.int32, (CHUNK, LANES), 0) & (SUB - 1)
    hsum = hsum_ref[...]
    hexp = hexp_ref[...]
    for d in range(SUB):
        if d == 0:
            ks, bs, vs = kf, b, v
        else:
            ks = pltpu.roll(kf, d, 0)
            bs = pltpu.roll(b, d, 0)
            vs = pltpu.roll(v, d, 0)
        p = qf * ks * jnp.exp(b - bs)
        att = _dot(p.astype(bf16), hsum)
        att = jnp.where(tpos >= d, att, 0.0)
        o = o + _dot(att.astype(bf16), hexp) * vs

    nw = nw_ref[...]
    gate = _silu(g_ref[...])
    outs = []
    for h in range(HGRN_HEADS):
        sl = slice(h * HGRN_DIM, (h + 1) * HGRN_DIM)
        oh = o[:, sl]
        ms = jnp.mean(oh * oh, axis=-1, keepdims=True)
        outs.append(oh * lax.rsqrt(ms + EPS) * nw * gate[:, sl])
    o_ref[...] = jnp.concatenate(outs, axis=-1).astype(o_ref.dtype)


def _hgrn_mixer(proj, lb, norm_w, consts, bsz, seq):
    n = bsz * seq
    nc = seq // CHUNK
    tri, hsum, hexp = consts
    width = HGRN_HEADS * HGRN_DIM
    const2 = lambda b, i: (0, 0)
    col = lambda j: (lambda b, i: (b * nc + i, j))
    return pl.pallas_call(
        _hgrn_kernel,
        grid=(bsz, nc),
        in_specs=[
            pl.BlockSpec((CHUNK, width), col(4)),
            pl.BlockSpec((CHUNK, width), col(5)),
            pl.BlockSpec((CHUNK, width), col(6)),
            pl.BlockSpec((CHUNK, width), col(7)),
            pl.BlockSpec((1, width), const2),
            pl.BlockSpec((1, HGRN_DIM), const2),
            pl.BlockSpec((CHUNK, CHUNK), const2),
            pl.BlockSpec((width, LANES), const2),
            pl.BlockSpec((LANES, width), const2),
        ],
        out_specs=pl.BlockSpec((CHUNK, width), lambda b, i: (b * nc + i, 0)),
        out_shape=jax.ShapeDtypeStruct((n, width), bf16),
        scratch_shapes=[pltpu.VMEM((HGRN_HEADS, HGRN_DIM, HGRN_DIM), f32)],
        compiler_params=_cparams(2),
        name="hgrn_mixer",
    )(proj, proj, proj, proj, lb, norm_w, tri, hsum, hexp)


MEM_T = 512


def _memattn_kernel(q_ref, k_ref, v_ref, o_ref):
    q = q_ref[...].astype(bf16)
    outs = []
    for h in range(MEM_HEADS):
        sl = slice(h * MEM_HEAD_DIM, (h + 1) * MEM_HEAD_DIM)
        s = _dot_nt(q[:, sl], k_ref[:, sl]) * (MEM_HEAD_DIM ** -0.5)
        m = jnp.max(s, axis=-1, keepdims=True)
        e = jnp.exp(s - m)
        p = e / jnp.sum(e, axis=-1, keepdims=True)
        outs.append(_dot(p.astype(bf16), v_ref[:, sl]))
    o_ref[...] = jnp.concatenate(outs, axis=-1).astype(o_ref.dtype)


def _mem_attention(proj, kv, bsz, seq):
    n = bsz * seq
    nt = seq // MEM_T
    width = MEM_HEADS * MEM_HEAD_DIM
    return pl.pallas_call(
        _memattn_kernel,
        grid=(bsz, nt),
        in_specs=[
            pl.BlockSpec((MEM_T, width), lambda b, i: (b * nt + i, 8)),
            pl.BlockSpec((MEM_TOKENS, width), lambda b, i: (b, 0)),
            pl.BlockSpec((MEM_TOKENS, width), lambda b, i: (b, 1)),
        ],
        out_specs=pl.BlockSpec((MEM_T, width), lambda b, i: (b * nt + i, 0)),
        out_shape=jax.ShapeDtypeStruct((n, width), bf16),
        compiler_params=_cparams(2),
        name="mem_attention",
    )(proj, kv, kv)


MERGE_TM = 512
MERGE_TN = 512


def _merge_kernel(h_ref, b0_ref, b1_ref, b2_ref, b3_ref, wg_ref, wb_ref, o_ref):
    h = h_ref[...]
    acc = None
    for n, br in enumerate((b0_ref, b1_ref, b2_ref, b3_ref)):
        gate = _sigmoid(_dot(h, wg_ref[n]))
        term = gate * _dot(br[...], wb_ref[n])
        acc = term if acc is None else acc + term
    o_ref[...] = acc.astype(o_ref.dtype)


def _merge(h_bf, branches, w_gate, w_branch):
    n = h_bf.shape[0]
    tm, tn = MERGE_TM, MERGE_TN
    bspec = pl.BlockSpec((tm, BRANCH_WIDTH), lambda j, i: (i, 0))
    return pl.pallas_call(
        _merge_kernel,
        grid=(D_MODEL // tn, n // tm),
        in_specs=[pl.BlockSpec((tm, D_MODEL), lambda j, i: (i, 0)), bspec, bspec, bspec, bspec,
                  pl.BlockSpec((N_BRANCH, D_MODEL, tn), lambda j, i: (0, 0, j)),
                  pl.BlockSpec((N_BRANCH, BRANCH_WIDTH, tn), lambda j, i: (0, 0, j))],
        out_specs=pl.BlockSpec((tm, tn), lambda j, i: (i, j)),
        out_shape=jax.ShapeDtypeStruct((n, D_MODEL), bf16),
        compiler_params=_cparams(2),
        name="gated_merge",
    )(h_bf, *branches, w_gate, w_branch)


def _layer_norm(x, g, b):
    mu = jnp.mean(x, axis=-1, keepdims=True)
    xc = x - mu
    var = jnp.mean(xc * xc, axis=-1, keepdims=True)
    return xc * lax.rsqrt(var + EPS) * g + b


def _store_rows(rows_ref, x, tm):
    for j in range(ROW_TILES):
        rows_ref[pl.ds(j, tm, stride=ROW_TILES), :] = x[:, j * LANES:(j + 1) * LANES]


def _load_rows(rows_ref, tm, lead=None):
    parts = []
    for j in range(ROW_TILES):
        if lead is None:
            parts.append(rows_ref[pl.ds(j, tm, stride=ROW_TILES), :])
        else:
            parts.append(rows_ref[lead, pl.ds(j, tm, stride=ROW_TILES), :])
    return jnp.concatenate(parts, axis=-1)


WO_TM = 256


def _wo_ln_kernel(m_ref, w_ref, h_ref, g_ref, b_ref, hf_ref, hb_ref, hr_ref):
    y = ALPHA * h_ref[...] + _dot(m_ref[...], w_ref[...])
    hn = _layer_norm(y, g_ref[...], b_ref[...])
    hf_ref[...] = hn
    hb_ref[...] = hn.astype(bf16)
    _store_rows(hr_ref, hn, WO_TM)


def _wo_ln(merged, w_o, h, g, b):
    n = h.shape[0]
    tm = WO_TM
    rowspec = pl.BlockSpec((tm, D_MODEL), lambda i: (i, 0))
    const2 = lambda i: (0, 0)
    return pl.pallas_call(
        _wo_ln_kernel,
        grid=(n // tm,),
        in_specs=[rowspec, pl.BlockSpec((D_MODEL, D_MODEL), const2), rowspec,
                  pl.BlockSpec((1, D_MODEL), const2), pl.BlockSpec((1, D_MODEL), const2)],
        out_specs=[rowspec, rowspec, pl.BlockSpec((tm * ROW_TILES, LANES), lambda i: (i, 0))],
        out_shape=[jax.ShapeDtypeStruct((n, D_MODEL), f32),
                   jax.ShapeDtypeStruct((n, D_MODEL), bf16),
                   jax.ShapeDtypeStruct((n * ROW_TILES, LANES), f32)],
        compiler_params=_cparams(1),
        name="wo_layernorm",
    )(merged, w_o, h, g, b)


ROUTER_TM = 256


def _router_kernel(h_ref, w_ref, bias_ref, eidx_ref, wts_ref):
    h = h_ref[...]
    w = w_ref[...]
    h_hi = h.astype(bf16)
    h_lo = (h - h_hi.astype(f32)).astype(bf16)
    w_hi = w.astype(bf16)
    w_lo = (w - w_hi.astype(f32)).astype(bf16)
    logits = _dot(h_hi, w_hi) + _dot(h_hi, w_lo) + _dot(h_lo, w_hi)
    scores = _sigmoid(logits)
    lane_i = lax.broadcasted_iota(jnp.int32, logits.shape, 1)
    lane = lane_i.astype(f32)
    real = lane_i < N_EXPERTS
    choice = jnp.where(real, scores + bias_ref[...], NEG_INF)
    gsize = N_EXPERTS // N_GROUPS
    grp = lax.shift_right_logical(lane_i, gsize.bit_length() - 1)
    big = float(LANES)

    def first_argmax(x):
        m = jnp.max(x, axis=-1, keepdims=True)
        idx = jnp.min(jnp.where(x == m, lane, big), axis=-1, keepdims=True)
        return m, idx

    gscore = []
    for g in range(N_GROUPS):
        cg = jnp.where(grp == g, choice, NEG_INF)
        m1, i1 = first_argmax(cg)
        m2 = jnp.max(jnp.where(lane == i1, NEG_INF, cg), axis=-1, keepdims=True)
        gscore.append(m1 + m2)
    keep = jnp.zeros(logits.shape, f32)
    for g in range(N_GROUPS):
        rank = jnp.zeros_like(gscore[g])
        for o in range(N_GROUPS):
            if o == g:
                continue
            ahead = (gscore[o] > gscore[g]) if o > g else (gscore[o] >= gscore[g])
            rank = rank + jnp.where(ahead, 1.0, 0.0)
        keep = jnp.where((grp == g) & (rank < TOPK_GROUPS), 1.0, keep)
    cur = jnp.where(real, jnp.where(keep > 0.0, choice, MASK_SCORE), NEG_INF)
    idxs, ws = [], []
    for _ in range(TOP_K):
        _, ik = first_argmax(cur)
        hit = lane == ik
        ws.append(jnp.sum(jnp.where(hit, scores, 0.0), axis=-1, keepdims=True))
        idxs.append(ik)
        cur = jnp.where(hit, NEG_INF, cur)
    wsum = ws[0]
    for wk in ws[1:]:
        wsum = wsum + wk
    eidx_ref[...] = jnp.concatenate(idxs, axis=-1).astype(jnp.int32)
    wts_ref[...] = jnp.concatenate([wk / wsum * ROUTED_SCALE for wk in ws], axis=-1)


def _router(h, w_router_p, bias_p):
    n = h.shape[0]
    tm = ROUTER_TM
    const2 = lambda i: (0, 0)
    return pl.pallas_call(
        _router_kernel,
        grid=(n // tm,),
        in_specs=[pl.BlockSpec((tm, D_MODEL), lambda i: (i, 0)),
                  pl.BlockSpec((D_MODEL, LANES), const2),
                  pl.BlockSpec((1, LANES), const2)],
        out_specs=[pl.BlockSpec((tm, TOP_K), lambda i: (i, 0)),
                   pl.BlockSpec((tm, TOP_K), lambda i: (i, 0))],
        out_shape=[jax.ShapeDtypeStruct((n, TOP_K), jnp.int32),
                   jax.ShapeDtypeStruct((n, TOP_K), f32)],
        compiler_params=_cparams(1),
        name="router",
    )(h, w_router_p, bias_p)


GATHER_UNROLL = 8


def _row_copy(src_hbm, buf, sem, slot, row, k):
    src_row = pl.multiple_of(row * ROW_TILES, ROW_TILES)
    return pltpu.make_async_copy(src_hbm.at[pl.ds(src_row, ROW_TILES)],
                                 buf.at[slot, pl.ds(k * ROW_TILES, ROW_TILES)], sem.at[slot])


def _start_gather(src_hbm, idx_ref, buf, sem, slot, count):
    def body(o, carry):
        for u in range(GATHER_UNROLL):
            k = o * GATHER_UNROLL + u
            _row_copy(src_hbm, buf, sem, slot, idx_ref[0, 0, k], k).start()
        return carry
    lax.fori_loop(0, count // GATHER_UNROLL, body, 0)


def _expert_kernel(be_ref, nused_ref, idx_ref, idxn_ref, roww_ref, wg_ref, wu_ref, wd_ref, hrows_hbm,
                   o_ref, buf, sem):
    i = pl.program_id(0)
    nb = pl.num_programs(0)
    n_used = nused_ref[0]
    slot = i % 2

    @pl.when((i == 0) & (n_used > 0))
    def _():
        _start_gather(hrows_hbm, idx_ref, buf, sem, 0, EXPERT_ROWS)

    @pl.when((i + 1 < nb) & (i + 1 < n_used))
    def _():
        _start_gather(hrows_hbm, idxn_ref, buf, sem, 1 - slot, EXPERT_ROWS)

    @pl.when(i < n_used)
    def _():
        pltpu.make_async_copy(hrows_hbm.at[pl.ds(0, EXPERT_ROWS * ROW_TILES)], buf.at[slot], sem.at[slot]).wait()
        xb = _load_rows(buf, EXPERT_ROWS, lead=slot).astype(bf16)
        act = _silu(_dot(xb, wg_ref[0])) * _dot(xb, wu_ref[0])
        y = _dot(act.astype(bf16), wd_ref[0]) * roww_ref[...]
        _store_rows(o_ref, y, EXPERT_ROWS)

    @pl.when(i >= n_used)
    def _():
        o_ref[...] = jnp.zeros_like(o_ref)


def _expert_mlp(block_e, n_used, row_tok3, row_w, wg, wu, wd, h_rows2d, nb):
    r = nb * EXPERT_ROWS
    grid_spec = pltpu.PrefetchScalarGridSpec(
        num_scalar_prefetch=2,
        grid=(nb,),
        in_specs=[
            pl.BlockSpec((1, 1, EXPERT_ROWS), lambda i, be, nu: (i, 0, 0), memory_space=pltpu.SMEM),
            pl.BlockSpec((1, 1, EXPERT_ROWS), lambda i, be, nu: (jnp.minimum(i + 1, nb - 1), 0, 0),
                         memory_space=pltpu.SMEM),
            pl.BlockSpec((EXPERT_ROWS, 1), lambda i, be, nu: (i, 0)),
            pl.BlockSpec((1, D_MODEL, D_EXPERT), lambda i, be, nu: (be[i], 0, 0)),
            pl.BlockSpec((1, D_MODEL, D_EXPERT), lambda i, be, nu: (be[i], 0, 0)),
            pl.BlockSpec((1, D_EXPERT, D_MODEL), lambda i, be, nu: (be[i], 0, 0)),
            pl.BlockSpec(memory_space=pl.ANY),
        ],
        out_specs=pl.BlockSpec((EXPERT_ROWS * ROW_TILES, LANES), lambda i, be, nu: (i, 0)),
        scratch_shapes=[pltpu.VMEM((2, EXPERT_ROWS * ROW_TILES, LANES), f32),
                        pltpu.SemaphoreType.DMA((2,))],
    )
    return pl.pallas_call(
        _expert_kernel,
        grid_spec=grid_spec,
        out_shape=jax.ShapeDtypeStruct((r * ROW_TILES, LANES), f32),
        compiler_params=_cparams(1),
        name="expert_mlp",
    )(block_e, n_used, row_tok3, row_tok3, row_w, wg, wu, wd, h_rows2d)


COMB_TM = 64


def _combine_kernel(idx_ref, idxn_ref, h_ref, hb_ref, sg_ref, su_ref, sd_ref, g_ref, b_ref, ys_hbm,
                    hf_ref, hb_out_ref, hr_ref, buf, comb, sem):
    i = pl.program_id(0)
    nb = pl.num_programs(0)
    slot = i % 2
    cnt = COMB_TM * TOP_K

    @pl.when(i == 0)
    def _():
        _start_gather(ys_hbm, idx_ref, buf, sem, 0, cnt)

    @pl.when(i + 1 < nb)
    def _():
        _start_gather(ys_hbm, idxn_ref, buf, sem, 1 - slot, cnt)

    pltpu.make_async_copy(ys_hbm.at[pl.ds(0, cnt * ROW_TILES)], buf.at[slot], sem.at[slot]).wait()
    rows = COMB_TM * ROW_TILES
    acc = buf[slot, pl.ds(0, rows), :]
    for k in range(1, TOP_K):
        acc = acc + buf[slot, pl.ds(k * rows, rows), :]
    comb[...] = acc
    routed = _load_rows(comb, COMB_TM)
    hb = hb_ref[...]
    act = _silu(_dot(hb, sg_ref[...])) * _dot(hb, su_ref[...])
    shared = _dot(act.astype(bf16), sd_ref[...])
    hn = _layer_norm(ALPHA * h_ref[...] + (routed + shared), g_ref[...], b_ref[...])
    hf_ref[...] = hn
    hb_out_ref[...] = hn.astype(bf16)
    _store_rows(hr_ref, hn, COMB_TM)


def _combine(pos3, h, h_bf, ws_gate, ws_up, ws_down, g, b, ys2d):
    n = h.shape[0]
    tm = COMB_TM
    nb = n // tm
    cnt = tm * TOP_K
    rowspec = pl.BlockSpec((tm, D_MODEL), lambda i: (i, 0))
    const2 = lambda i: (0, 0)
    return pl.pallas_call(
        _combine_kernel,
        grid=(nb,),
        in_specs=[
            pl.BlockSpec((1, 1, cnt), lambda i: (i, 0, 0), memory_space=pltpu.SMEM),
            pl.BlockSpec((1, 1, cnt), lambda i: (jnp.minimum(i + 1, nb - 1), 0, 0), memory_space=pltpu.SMEM),
            rowspec, rowspec,
            pl.BlockSpec((D_MODEL, D_SHARED), const2),
            pl.BlockSpec((D_MODEL, D_SHARED), const2),
            pl.BlockSpec((D_SHARED, D_MODEL), const2),
            pl.BlockSpec((1, D_MODEL), const2),
            pl.BlockSpec((1, D_MODEL), const2),
            pl.BlockSpec(memory_space=pl.ANY),
        ],
        out_specs=[rowspec, rowspec, pl.BlockSpec((tm * ROW_TILES, LANES), lambda i: (i, 0))],
        out_shape=[jax.ShapeDtypeStruct((n, D_MODEL), f32),
                   jax.ShapeDtypeStruct((n, D_MODEL), bf16),
                   jax.ShapeDtypeStruct((n * ROW_TILES, LANES), f32)],
        scratch_shapes=[pltpu.VMEM((2, cnt * ROW_TILES, LANES), f32),
                        pltpu.VMEM((tm * ROW_TILES, LANES), f32),
                        pltpu.SemaphoreType.DMA((2,))],
        compiler_params=_cparams(1),
        name="moe_combine",
    )(pos3, pos3, h, h_bf, ws_gate, ws_up, ws_down, g, b, ys2d)


def _dispatch(eidx, wts, nb):
    n = eidx.shape[0]
    nk = n * TOP_K
    r = nb * EXPERT_ROWS
    flat_e = eidx.reshape(nk)
    onehot = (flat_e[:, None] == jnp.arange(N_EXPERTS, dtype=jnp.int32)[None, :]).astype(jnp.int32)
    csum = jnp.cumsum(onehot, axis=0)
    counts = csum[-1]
    rank = jnp.sum(onehot * csum, axis=1) - 1
    padded = (counts + EXPERT_ROWS - 1) // EXPERT_ROWS * EXPERT_ROWS
    pend = jnp.cumsum(padded)
    pstart = pend - padded
    dest = jnp.sum(onehot * pstart[None, :], axis=1) + rank
    flat_tok = jnp.arange(nk, dtype=jnp.int32) // TOP_K
    row_tok = jnp.zeros((r,), jnp.int32).at[dest].set(flat_tok)
    row_w = jnp.zeros((r,), f32).at[dest].set(wts.reshape(nk))
    block_e = jnp.minimum(jnp.searchsorted(pend, jnp.arange(nb, dtype=jnp.int32) * EXPERT_ROWS, side="right"),
                          N_EXPERTS - 1).astype(jnp.int32)
    n_used = (pend[-1] // EXPERT_ROWS).astype(jnp.int32).reshape(1)
    tm = COMB_TM
    pos3 = dest.astype(jnp.int32).reshape(n // tm, tm, TOP_K).transpose(0, 2, 1).reshape(n // tm, 1, tm * TOP_K)
    return block_e, n_used, row_tok.reshape(nb, 1, EXPERT_ROWS), row_w.reshape(r, 1), pos3


def _consts():
    r = jnp.arange(CHUNK)
    tri = (r[:, None] >= r[None, :]).astype(bf16)
    trit = (r[:, None] <= r[None, :]).astype(bf16)
    lane = jnp.arange(LANES)
    ch = jnp.arange(SSD_HEADS * SSD_HEADDIM)
    ssd_expand = (lane[:, None] == (ch[None, :] // SSD_HEADDIM)).astype(bf16)
    hc = jnp.arange(HGRN_HEADS * HGRN_DIM)
    hsum = ((hc[:, None] // HGRN_DIM) == lane[None, :]).astype(bf16)
    hexp = (lane[:, None] == (hc[None, :] // HGRN_DIM)).astype(bf16)
    return tri, trit, ssd_expand, hsum, hexp


def _hgrn_lower_bounds(lb_param):
    sm = jax.nn.softmax(lb_param.astype(f32), axis=0)
    return jnp.cumsum(sm, axis=0) - sm[0:1]


def kernel(x, mem, w_in, conv_w, conv_b, dt_bias, a_log, d_skip, ssd_norm, pool_w, pool_scale, hgrn_lb, hgrn_norm, w_mem_kv, w_branch, w_o, ln1_g, ln1_b, w_router, router_bias, w_exp_gate, w_exp_up, w_exp_down, w_sh_gate, w_sh_up, w_sh_down, ln2_g, ln2_b):
    bsz, seq, d = x.shape
    n = bsz * seq
    nk = n * TOP_K
    nb = -(-nk // EXPERT_ROWS) + N_EXPERTS
    tri, trit, ssd_expand, hsum, hexp = _consts()
    lb_all = _hgrn_lower_bounds(hgrn_lb)

    c_dt0 = BRANCH_WIDTH * 2 + SSD_CONV_DIM
    c_dt1 = c_dt0 + SSD_HEADS
    c_gate = c_dt1 + 5 * BRANCH_WIDTH
    pad_lanes = LANES - SSD_HEADS

    h = x.reshape(n, d)
    h_bf = h.astype(bf16)
    h_rows = None
    mem_bf = mem.reshape(bsz * MEM_TOKENS, d).astype(bf16)

    for l in range(DEPTH):
        w_main = jnp.concatenate([w_in[l, :, :c_dt0], w_in[l, :, c_dt1:c_gate]], axis=1).astype(bf16)
        w_dt = jnp.pad(w_in[l, :, c_dt0:c_dt1], ((0, 0), (0, pad_lanes))).astype(bf16)
        w_gate = w_in[l, :, c_gate:].reshape(d, N_BRANCH, d).transpose(1, 0, 2).astype(bf16)

        proj = _matmul(h_bf, w_main, f32, 512, 1024)
        dtp = _matmul(h_bf, w_dt, f32, 512, LANES)
        kv = _matmul(mem_bf, w_mem_kv[l].astype(bf16), bf16, 512, 1024)

        br_pool = _pool_mixer(proj, pool_w[l].astype(bf16), pool_scale[l].reshape(1, -1), bsz, seq)
        br_ssd = _ssd_mixer(
            proj, dtp, conv_w[l], conv_b[l].reshape(1, -1),
            jnp.pad(dt_bias[l], (0, pad_lanes)).reshape(1, -1),
            jnp.pad(a_log[l], (0, pad_lanes)).reshape(1, -1),
            jnp.repeat(d_skip[l], SSD_HEADDIM).reshape(1, -1),
            ssd_norm[l].reshape(1, -1), (tri, trit, ssd_expand), bsz, seq)
        br_hgrn = _hgrn_mixer(proj, lb_all[l].reshape(1, -1), hgrn_norm[l].reshape(1, -1),
                              (tri, hsum, hexp), bsz, seq)
        br_mem = _mem_attention(proj, kv, bsz, seq)

        merged = _merge(h_bf, (br_pool, br_ssd, br_hgrn, br_mem), w_gate, w_branch[l].astype(bf16))
        h, h_bf, h_rows = _wo_ln(merged, w_o[l].astype(bf16), h, ln1_g[l].reshape(1, -1), ln1_b[l].reshape(1, -1))

        eidx, wts = _router(h, jnp.pad(w_router[l], ((0, 0), (0, LANES - N_EXPERTS))),
                            jnp.pad(router_bias[l], (0, LANES - N_EXPERTS)).reshape(1, -1))
        block_e, n_used, row_tok3, row_w, pos3 = _dispatch(eidx, wts, nb)
        ys = _expert_mlp(block_e, n_used, row_tok3, row_w, w_exp_gate[l].astype(bf16), w_exp_up[l].astype(bf16),
                         w_exp_down[l].astype(bf16), h_rows, nb)
        h, h_bf, h_rows = _combine(pos3, h, h_bf, w_sh_gate[l].astype(bf16), w_sh_up[l].astype(bf16),
                                   w_sh_down[l].astype(bf16), ln2_g[l].reshape(1, -1), ln2_b[l].reshape(1, -1), ys)
    return h.reshape(bsz, seq, d)
```

```python
import functools

import jax
import jax.numpy as jnp
from jax import lax
from jax.experimental import pallas as pl
from jax.experimental.pallas import tpu as pltpu

f32 = jnp.float32
bf16 = jnp.bfloat16

D_MODEL = 2048
DEPTH = 4
CHUNK = 64
N_BRANCH = 4
BRANCH_WIDTH = 1024
POOL_WINDOWS = (2, 4, 8, 16)
POOL_GROUP = 256
SSD_HEADS = 16
SSD_HEADDIM = 64
SSD_STATE = 128
SSD_GROUPS = 4
SSD_CONV = 4
SSD_CONV_DIM = 2048
HGRN_HEADS = 8
HGRN_DIM = 128
MEM_TOKENS = 256
MEM_HEADS = 4
MEM_HEAD_DIM = 256
N_EXPERTS = 64
TOP_K = 8
N_GROUPS = 8
TOPK_GROUPS = 4
D_EXPERT = 256
D_SHARED = 256
ROUTED_SCALE = 2.5
MASK_SCORE = -1e4
ALPHA = (2 * DEPTH) ** 0.25
EPS = 1e-5

LANES = 128
ROW_TILES = D_MODEL // LANES
EXPERT_ROWS = 128
SUB = 16
VMEM_LIMIT = 56 * 1024 * 1024

NEG_INF = float("-inf")


def _cparams(n_axes):
    return pltpu.CompilerParams(dimension_semantics=("arbitrary",) * n_axes,
                                vmem_limit_bytes=VMEM_LIMIT)


def _sigmoid(x):
    return jax.nn.sigmoid(x)


def _silu(x):
    return x * jax.nn.sigmoid(x)


def _split3(x):
    hi = x.astype(bf16)
    r1 = x - hi.astype(f32)
    mid = r1.astype(bf16)
    lo = (r1 - mid.astype(f32)).astype(bf16)
    return hi, mid, lo


def _dot(a, b):
    return jnp.dot(a, b, preferred_element_type=f32)


def _dot_nt(a, b):
    return lax.dot_general(a, b, (((1,), (1,)), ((), ())), preferred_element_type=f32)


def _dot_tn(a, b):
    return lax.dot_general(a, b, (((0,), (0,)), ((), ())), preferred_element_type=f32)


def _sel_left(sel01, x):
    return sum(_dot(sel01, p) for p in _split3(x))


def _sel_right(x, sel01):
    return sum(_dot(p, sel01) for p in _split3(x))


def _mm_kernel(x_ref, w_ref, o_ref):
    o_ref[...] = _dot(x_ref[...], w_ref[...]).astype(o_ref.dtype)


def _matmul(x, w, out_dtype, tm, tn):
    m, k = x.shape
    n = w.shape[1]
    return pl.pallas_call(
        _mm_kernel,
        grid=(n // tn, m // tm),
        in_specs=[pl.BlockSpec((tm, k), lambda j, i: (i, 0)),
                  pl.BlockSpec((k, tn), lambda j, i: (0, j))],
        out_specs=pl.BlockSpec((tm, tn), lambda j, i: (i, j)),
        out_shape=jax.ShapeDtypeStruct((m, n), out_dtype),
        compiler_params=_cparams(2),
        name="matmul",
    )(x, w)


POOL_T = 512
POOL_HALO = 16


def _pool_kernel(u_ref, halo_ref, w_ref, scale_ref, o_ref):
    i = pl.program_id(1)
    u = u_ref[...]
    halo = jnp.where(i > 0, halo_ref[...], 0.0)
    ext = jnp.concatenate([halo, u], axis=0)
    t = i * POOL_T + lax.broadcasted_iota(jnp.int32, (POOL_T, 1), 0)
    outs = []
    for gi, w in enumerate(POOL_WINDOWS):
        xg = ext[:, gi * POOL_GROUP:(gi + 1) * POOL_GROUP]
        s = xg
        span = 1
        while span < w:
            s = s[span:, :] + s[:-span, :]
            span *= 2
        s = s[s.shape[0] - POOL_T:, :]
        cnt = jnp.minimum(t + 1, w).astype(f32)
        d = s / cnt - u[:, gi * POOL_GROUP:(gi + 1) * POOL_GROUP]
        outs.append(_dot(d.astype(bf16), w_ref[gi]))
    y = jnp.concatenate(outs, axis=-1) * scale_ref[...]
    o_ref[...] = y.astype(o_ref.dtype)


def _pool_mixer(proj, pool_w, pool_scale, bsz, seq):
    n = bsz * seq
    nt = seq // POOL_T
    return pl.pallas_call(
        _pool_kernel,
        grid=(bsz, nt),
        in_specs=[
            pl.BlockSpec((POOL_T, BRANCH_WIDTH), lambda b, i: (b * nt + i, 0)),
            pl.BlockSpec((POOL_HALO, BRANCH_WIDTH),
                         lambda b, i: (jnp.maximum((b * nt + i) * (POOL_T // POOL_HALO) - 1, 0), 0)),
            pl.BlockSpec((len(POOL_WINDOWS), POOL_GROUP, POOL_GROUP), lambda b, i: (0, 0, 0)),
            pl.BlockSpec((1, BRANCH_WIDTH), lambda b, i: (0, 0)),
        ],
        out_specs=pl.BlockSpec((POOL_T, BRANCH_WIDTH), lambda b, i: (b * nt + i, 0)),
        out_shape=jax.ShapeDtypeStruct((n, BRANCH_WIDTH), bf16),
        compiler_params=_cparams(2),
        name="pool_mixer",
    )(proj, proj, pool_w, pool_scale)


CONV_TAIL = 8


def _softplus(x):
    return jnp.maximum(x, 0.0) + jnp.log1p(jnp.exp(-jnp.abs(x)))


def _ssd_kernel(z_ref, xbc_ref, dt_ref, cw_ref, cb_ref, dtb_ref, alog_ref, dskip_ref, nw_ref,
                tri_ref, trit_ref, exp_ref, o_ref, state_ref, tail_ref):
    i = pl.program_id(1)

    @pl.when(i == 0)
    def _():
        state_ref[...] = jnp.zeros_like(state_ref)
        tail_ref[...] = jnp.zeros_like(tail_ref)

    xbc = xbc_ref[...]
    ext = jnp.concatenate([tail_ref[...], xbc], axis=0)
    cw = cw_ref[...]
    conv = cb_ref[...]
    for k in range(SSD_CONV):
        off = CONV_TAIL - (SSD_CONV - 1) + k
        conv = conv + cw[k:k + 1, :] * ext[off:off + CHUNK, :]
    tail_ref[...] = xbc[CHUNK - CONV_TAIL:, :]
    act = _silu(conv)
    inner = SSD_HEADS * SSD_HEADDIM
    gw = SSD_GROUPS * SSD_STATE
    xs = act[:, :inner]
    bmat = act[:, inner:inner + gw]
    cmat = act[:, inner + gw:]

    dt = _softplus(dt_ref[...] + dtb_ref[...])
    a = -jnp.exp(alog_ref[...])
    ad = dt * a
    ad_parts = _split3(ad)
    tri = tri_ref[...]
    acs = sum(_dot(tri, p) for p in ad_parts)
    acs_t = sum(_dot_tn(p, trit_ref[...]) for p in ad_parts)
    a_last = acs[CHUNK - 1:CHUNK, :]
    expand = exp_ref[...]
    dt_e = _sel_right(dt, expand)
    eacs_e = _sel_right(jnp.exp(acs), expand)
    dec_e = _sel_right(jnp.exp(a_last - acs), expand)
    x = xs * dt_e
    xb = x.astype(bf16)
    xd = (x * dec_e).astype(bf16)
    elast_e = eacs_e[CHUNK - 1:CHUNK, :]

    row = lax.broadcasted_iota(jnp.int32, (CHUNK, CHUNK), 0)
    col = lax.broadcasted_iota(jnp.int32, (CHUNK, CHUNK), 1)
    causal = row >= col
    hpg = SSD_HEADS // SSD_GROUPS
    gch = hpg * SSD_HEADDIM
    y_diag = []
    y_off = []
    for g in range(SSD_GROUPS):
        bg = bmat[:, g * SSD_STATE:(g + 1) * SSD_STATE].astype(bf16)
        cg = cmat[:, g * SSD_STATE:(g + 1) * SSD_STATE].astype(bf16)
        cb = _dot_nt(cg, bg)
        for hh in range(hpg):
            h = g * hpg + hh
            diff = acs[:, h:h + 1] - acs_t[h:h + 1, :]
            decay = jnp.where(causal, jnp.exp(jnp.where(causal, diff, 0.0)), 0.0)
            sc = (cb * decay).astype(bf16)
            y_diag.append(_dot(sc, xb[:, h * SSD_HEADDIM:(h + 1) * SSD_HEADDIM]))
        st = state_ref[:, g * gch:(g + 1) * gch]
        y_off.append(_dot(cg, st.astype(bf16)))
        upd = _dot_tn(bg, xd[:, g * gch:(g + 1) * gch])
        state_ref[:, g * gch:(g + 1) * gch] = st * elast_e[:, g * gch:(g + 1) * gch] + upd
    y = (jnp.concatenate(y_diag, axis=-1) + jnp.concatenate(y_off, axis=-1) * eacs_e
         + xs * dskip_ref[...])
    y = y * _silu(z_ref[...])
    nw = nw_ref[...]
    ngw = inner // SSD_GROUPS
    outs = []
    for g in range(SSD_GROUPS):
        yg = y[:, g * ngw:(g + 1) * ngw]
        ms = jnp.mean(yg * yg, axis=-1, keepdims=True)
        outs.append(yg * lax.rsqrt(ms + EPS) * nw[:, g * ngw:(g + 1) * ngw])
    o_ref[...] = jnp.concatenate(outs, axis=-1).astype(o_ref.dtype)


def _ssd_mixer(proj, dtp, conv_w, conv_b, dt_bias, a_log, d_skip_e, norm_w, consts, bsz, seq):
    n = bsz * seq
    nc = seq // CHUNK
    tri, trit, expand = consts
    rowmap = lambda b, i: (b * nc + i, 0)
    const2 = lambda b, i: (0, 0)
    inner = SSD_HEADS * SSD_HEADDIM
    return pl.pallas_call(
        _ssd_kernel,
        grid=(bsz, nc),
        in_specs=[
            pl.BlockSpec((CHUNK, inner), lambda b, i: (b * nc + i, 1)),
            pl.BlockSpec((CHUNK, SSD_CONV_DIM), lambda b, i: (b * nc + i, 1)),
            pl.BlockSpec((CHUNK, LANES), rowmap),
            pl.BlockSpec((SSD_CONV, SSD_CONV_DIM), const2),
            pl.BlockSpec((1, SSD_CONV_DIM), const2),
            pl.BlockSpec((1, LANES), const2),
            pl.BlockSpec((1, LANES), const2),
            pl.BlockSpec((1, inner), const2),
            pl.BlockSpec((1, inner), const2),
            pl.BlockSpec((CHUNK, CHUNK), const2),
            pl.BlockSpec((CHUNK, CHUNK), const2),
            pl.BlockSpec((LANES, inner), const2),
        ],
        out_specs=pl.BlockSpec((CHUNK, inner), rowmap),
        out_shape=jax.ShapeDtypeStruct((n, inner), bf16),
        scratch_shapes=[pltpu.VMEM((SSD_STATE, inner), f32),
                        pltpu.VMEM((CONV_TAIL, SSD_CONV_DIM), f32)],
        compiler_params=_cparams(2),
        name="ssd_mixer",
    )(proj, proj, dtp, conv_w, conv_b, dt_bias, a_log, d_skip_e, norm_w, tri, trit, expand)


LOG2E = 1.4426950408889634
HGRN_NBLK = CHUNK // SUB
HGRN_OFF_KEYS = SUB * (HGRN_NBLK - 1) * HGRN_NBLK // 2


def _hgrn_chunk(q, f, v, g, lb, nw, tri, hmask, offmask, state_ref):
    width = HGRN_HEADS * HGRN_DIM
    qf = _silu(q)
    kf = (1.0 - lb) * _sigmoid(-f)
    b2 = _sel_left(tri, jnp.log1p(-kf)) * LOG2E
    c2 = b2 - jnp.log2(kf)
    b_last = b2[CHUNK - 1:CHUNK, :]
    vb = v.astype(bf16)

    qd = (qf * jnp.exp2(b2)).astype(bf16)
    kd = jnp.exp2(b_last - c2).astype(bf16)
    eb_last = jnp.exp2(b_last)
    o_inter = []
    for h in range(HGRN_HEADS):
        sl = slice(h * HGRN_DIM, (h + 1) * HGRN_DIM)
        st = state_ref[h]
        o_inter.append(_dot_nt(qd[:, sl], st.astype(bf16)))
        state_ref[h] = st * eb_last[:, sl] + _dot_tn(vb[:, sl], kd[:, sl])
    o = jnp.concatenate(o_inter, axis=-1)

    qm, kk, vv = [], [], []
    for blk in range(1, HGRN_NBLK):
        r0 = blk * SUB
        anchor = b2[r0 - 1:r0, :]
        qq = qf[r0:r0 + SUB, :] * jnp.exp2(b2[r0:r0 + SUB, :] - anchor)
        qm.append((jnp.concatenate([qq] * HGRN_HEADS, axis=0) * hmask).astype(bf16))
        kk.append(jnp.exp2(anchor - c2[:r0, :]).astype(bf16))
        vv.append(vb[:r0, :])
    zpad = jnp.zeros((LANES - HGRN_OFF_KEYS, width), bf16)
    att = _dot_nt(jnp.concatenate(qm, axis=0), jnp.concatenate(kk + [zpad], axis=0))
    res = _dot((att * offmask).astype(bf16), jnp.concatenate(vv + [zpad], axis=0))
    rows = [jnp.zeros((SUB, width), f32)]
    for blk in range(1, HGRN_NBLK):
        base = (blk - 1) * HGRN_HEADS * SUB
        acc = None
        for h in range(HGRN_HEADS):
            part = res[base + h * SUB:base + (h + 1) * SUB, :] * hmask[h * SUB:(h + 1) * SUB, :]
            acc = part if acc is None else acc + part
        rows.append(acc)
    o = o + jnp.concatenate(rows, axis=0)

    tpos = lax.broadcasted_iota(jnp.int32, (CHUNK, 1), 0) & (SUB - 1)
    diag = [None] * HGRN_HEADS
    for d in range(SUB):
        cs = c2 if d == 0 else pltpu.roll(c2, d, 0)
        vs = v if d == 0 else pltpu.roll(v, d, 0)
        p = qf * jnp.exp2(b2 - cs)
        same_block = tpos >= d
        for h in range(HGRN_HEADS):
            sl = slice(h * HGRN_DIM, (h + 1) * HGRN_DIM)
            att = jnp.where(same_block, jnp.sum(p[:, sl], axis=-1, keepdims=True), 0.0)
            term = att * vs[:, sl]
            diag[h] = term if diag[h] is None else diag[h] + term
    o = o + jnp.concatenate(diag, axis=-1)

    gate = _silu(g)
    outs = []
    for h in range(HGRN_HEADS):
        sl = slice(h * HGRN_DIM, (h + 1) * HGRN_DIM)
        oh = o[:, sl]
        ms = jnp.mean(oh * oh, axis=-1, keepdims=True)
        outs.append(oh * lax.rsqrt(ms + EPS) * nw * gate[:, sl])
    return jnp.concatenate(outs, axis=-1)


def _hgrn_kernel(q_ref, f_ref, i_ref, g_ref, lb_ref, nw_ref, tri_ref, hmask_ref, offmask_ref, o_ref, state_ref):
    c = pl.program_id(0)

    @pl.when(c == 0)
    def _():
        state_ref[...] = jnp.zeros_like(state_ref)

    for bi in range(q_ref.shape[0]):
        out = _hgrn_chunk(q_ref[bi], f_ref[bi], i_ref[bi], g_ref[bi], lb_ref[...], nw_ref[...], tri_ref[...],
                          hmask_ref[...], offmask_ref[...], state_ref.at[bi])
        o_ref[bi] = out.astype(o_ref.dtype)


def _hgrn_mixer(proj, lb, norm_w, consts, bsz, seq):
    n = bsz * seq
    nc = seq // CHUNK
    tri, hmask, offmask = consts
    width = HGRN_HEADS * HGRN_DIM
    const2 = lambda i: (0, 0)
    col = lambda j: (lambda i: (0, i, j))
    proj3 = proj.reshape(bsz, seq, proj.shape[1])
    qrows = (HGRN_NBLK - 1) * HGRN_HEADS * SUB
    out = pl.pallas_call(
        _hgrn_kernel,
        grid=(nc,),
        in_specs=[
            pl.BlockSpec((bsz, CHUNK, width), col(4)),
            pl.BlockSpec((bsz, CHUNK, width), col(5)),
            pl.BlockSpec((bsz, CHUNK, width), col(6)),
            pl.BlockSpec((bsz, CHUNK, width), col(7)),
            pl.BlockSpec((1, width), const2),
            pl.BlockSpec((1, HGRN_DIM), const2),
            pl.BlockSpec((CHUNK, CHUNK), const2),
            pl.BlockSpec((HGRN_HEADS * SUB, width), const2),
            pl.BlockSpec((qrows, LANES), const2),
        ],
        out_specs=pl.BlockSpec((bsz, CHUNK, width), lambda i: (0, i, 0)),
        out_shape=jax.ShapeDtypeStruct((bsz, seq, width), bf16),
        scratch_shapes=[pltpu.VMEM((bsz, HGRN_HEADS, HGRN_DIM, HGRN_DIM), f32)],
        compiler_params=_cparams(1),
        name="hgrn_mixer",
    )(proj3, proj3, proj3, proj3, lb, norm_w, tri, hmask, offmask)
    return out.reshape(n, width)


MEM_T = 512


def _memattn_kernel(q_ref, k_ref, v_ref, o_ref):
    q = q_ref[...].astype(bf16)
    outs = []
    for h in range(MEM_HEADS):
        sl = slice(h * MEM_HEAD_DIM, (h + 1) * MEM_HEAD_DIM)
        s = _dot_nt(q[:, sl], k_ref[:, sl]) * (MEM_HEAD_DIM ** -0.5)
        m = jnp.max(s, axis=-1, keepdims=True)
        e = jnp.exp(s - m)
        p = e / jnp.sum(e, axis=-1, keepdims=True)
        outs.append(_dot(p.astype(bf16), v_ref[:, sl]))
    o_ref[...] = jnp.concatenate(outs, axis=-1).astype(o_ref.dtype)


def _mem_attention(proj, kv, bsz, seq):
    n = bsz * seq
    nt = seq // MEM_T
    width = MEM_HEADS * MEM_HEAD_DIM
    return pl.pallas_call(
        _memattn_kernel,
        grid=(bsz, nt),
        in_specs=[
            pl.BlockSpec((MEM_T, width), lambda b, i: (b * nt + i, 8)),
            pl.BlockSpec((MEM_TOKENS, width), lambda b, i: (b, 0)),
            pl.BlockSpec((MEM_TOKENS, width), lambda b, i: (b, 1)),
        ],
        out_specs=pl.BlockSpec((MEM_T, width), lambda b, i: (b * nt + i, 0)),
        out_shape=jax.ShapeDtypeStruct((n, width), bf16),
        compiler_params=_cparams(2),
        name="mem_attention",
    )(proj, kv, kv)


MERGE_TM = 512
MERGE_TN = 512


def _merge_kernel(h_ref, b0_ref, b1_ref, b2_ref, b3_ref, wg_ref, wb_ref, o_ref):
    h = h_ref[...]
    acc = None
    for n, br in enumerate((b0_ref, b1_ref, b2_ref, b3_ref)):
        gate = _sigmoid(_dot(h, wg_ref[n]))
        term = gate * _dot(br[...], wb_ref[n])
        acc = term if acc is None else acc + term
    o_ref[...] = acc.astype(o_ref.dtype)


def _merge(h_bf, branches, w_gate, w_branch):
    n = h_bf.shape[0]
    tm, tn = MERGE_TM, MERGE_TN
    bspec = pl.BlockSpec((tm, BRANCH_WIDTH), lambda j, i: (i, 0))
    return pl.pallas_call(
        _merge_kernel,
        grid=(D_MODEL // tn, n // tm),
        in_specs=[pl.BlockSpec((tm, D_MODEL), lambda j, i: (i, 0)), bspec, bspec, bspec, bspec,
                  pl.BlockSpec((N_BRANCH, D_MODEL, tn), lambda j, i: (0, 0, j)),
                  pl.BlockSpec((N_BRANCH, BRANCH_WIDTH, tn), lambda j, i: (0, 0, j))],
        out_specs=pl.BlockSpec((tm, tn), lambda j, i: (i, j)),
        out_shape=jax.ShapeDtypeStruct((n, D_MODEL), bf16),
        compiler_params=_cparams(2),
        name="gated_merge",
    )(h_bf, *branches, w_gate, w_branch)


def _layer_norm(x, g, b):
    mu = jnp.mean(x, axis=-1, keepdims=True)
    xc = x - mu
    var = jnp.mean(xc * xc, axis=-1, keepdims=True)
    return xc * lax.rsqrt(var + EPS) * g + b


def _store_rows(rows_ref, x, tm):
    for j in range(ROW_TILES):
        rows_ref[pl.ds(j, tm, stride=ROW_TILES), :] = x[:, j * LANES:(j + 1) * LANES]


def _load_rows(rows_ref, tm, lead=None):
    parts = []
    for j in range(ROW_TILES):
        if lead is None:
            parts.append(rows_ref[pl.ds(j, tm, stride=ROW_TILES), :])
        else:
            parts.append(rows_ref[lead, pl.ds(j, tm, stride=ROW_TILES), :])
    return jnp.concatenate(parts, axis=-1)


WO_TM = 256


def _wo_ln_kernel(m_ref, w_ref, h_ref, g_ref, b_ref, hf_ref, hb_ref, hr_ref):
    y = ALPHA * h_ref[...] + _dot(m_ref[...], w_ref[...])
    hn = _layer_norm(y, g_ref[...], b_ref[...])
    hf_ref[...] = hn
    hb_ref[...] = hn.astype(bf16)
    _store_rows(hr_ref, hn, WO_TM)


def _wo_ln(merged, w_o, h, g, b):
    n = h.shape[0]
    tm = WO_TM
    rowspec = pl.BlockSpec((tm, D_MODEL), lambda i: (i, 0))
    const2 = lambda i: (0, 0)
    return pl.pallas_call(
        _wo_ln_kernel,
        grid=(n // tm,),
        in_specs=[rowspec, pl.BlockSpec((D_MODEL, D_MODEL), const2), rowspec,
                  pl.BlockSpec((1, D_MODEL), const2), pl.BlockSpec((1, D_MODEL), const2)],
        out_specs=[rowspec, rowspec, pl.BlockSpec((tm * ROW_TILES, LANES), lambda i: (i, 0))],
        out_shape=[jax.ShapeDtypeStruct((n, D_MODEL), f32),
                   jax.ShapeDtypeStruct((n, D_MODEL), bf16),
                   jax.ShapeDtypeStruct((n * ROW_TILES, LANES), f32)],
        compiler_params=_cparams(1),
        name="wo_layernorm",
    )(merged, w_o, h, g, b)


ROUTER_TM = 256


def _router_kernel(h_ref, w_ref, bias_ref, ltri_ref, eidx_ref, wts_ref, rank_ref, counts_ref, cnt_ref):
    step = pl.program_id(0)

    @pl.when(step == 0)
    def _():
        cnt_ref[...] = jnp.zeros_like(cnt_ref)

    h = h_ref[...]
    w = w_ref[...]
    h_hi = h.astype(bf16)
    h_lo = (h - h_hi.astype(f32)).astype(bf16)
    w_hi = w.astype(bf16)
    w_lo = (w - w_hi.astype(f32)).astype(bf16)
    logits = _dot(h_hi, w_hi) + _dot(h_hi, w_lo) + _dot(h_lo, w_hi)
    scores = _sigmoid(logits)
    lane_i = lax.broadcasted_iota(jnp.int32, logits.shape, 1)
    lane = lane_i.astype(f32)
    real = lane_i < N_EXPERTS
    choice = jnp.where(real, scores + bias_ref[...], NEG_INF)
    gsize = N_EXPERTS // N_GROUPS
    grp = lax.shift_right_logical(lane_i, gsize.bit_length() - 1)
    big = float(LANES)

    def first_argmax(x):
        m = jnp.max(x, axis=-1, keepdims=True)
        idx = jnp.min(jnp.where(x == m, lane, big), axis=-1, keepdims=True)
        return m, idx

    gscore = []
    for g in range(N_GROUPS):
        cg = jnp.where(grp == g, choice, NEG_INF)
        m1, i1 = first_argmax(cg)
        m2 = jnp.max(jnp.where(lane == i1, NEG_INF, cg), axis=-1, keepdims=True)
        gscore.append(m1 + m2)
    keep = jnp.zeros(logits.shape, f32)
    for g in range(N_GROUPS):
        rank = jnp.zeros_like(gscore[g])
        for o in range(N_GROUPS):
            if o == g:
                continue
            ahead = (gscore[o] > gscore[g]) if o > g else (gscore[o] >= gscore[g])
            rank = rank + jnp.where(ahead, 1.0, 0.0)
        keep = jnp.where((grp == g) & (rank < TOPK_GROUPS), 1.0, keep)
    cur = jnp.where(real, jnp.where(keep > 0.0, choice, MASK_SCORE), NEG_INF)
    idxs, ws, hits = [], [], []
    sel = jnp.zeros(logits.shape, f32)
    for _ in range(TOP_K):
        _, ik = first_argmax(cur)
        hit = lane == ik
        ws.append(jnp.sum(jnp.where(hit, scores, 0.0), axis=-1, keepdims=True))
        idxs.append(ik)
        hits.append(hit)
        sel = jnp.where(hit, 1.0, sel)
        cur = jnp.where(hit, NEG_INF, cur)
    wsum = ws[0]
    for wk in ws[1:]:
        wsum = wsum + wk
    eidx_ref[...] = jnp.concatenate(idxs, axis=-1).astype(jnp.int32)
    wts_ref[...] = jnp.concatenate([wk / wsum * ROUTED_SCALE for wk in ws], axis=-1)
    before = _dot(ltri_ref[...], sel.astype(bf16)) + cnt_ref[...]
    ranks = [jnp.sum(jnp.where(hit, before, 0.0), axis=-1, keepdims=True) for hit in hits]
    rank_ref[...] = jnp.concatenate(ranks, axis=-1).astype(jnp.int32)
    cnt_ref[...] = cnt_ref[...] + jnp.sum(sel, axis=0, keepdims=True)
    counts_ref[...] = cnt_ref[...]


def _router(h, w_router_p, bias_p, ltri):
    n = h.shape[0]
    tm = ROUTER_TM
    const2 = lambda i: (0, 0)
    kspec = pl.BlockSpec((tm, TOP_K), lambda i: (i, 0))
    return pl.pallas_call(
        _router_kernel,
        grid=(n // tm,),
        in_specs=[pl.BlockSpec((tm, D_MODEL), lambda i: (i, 0)),
                  pl.BlockSpec((D_MODEL, LANES), const2),
                  pl.BlockSpec((1, LANES), const2),
                  pl.BlockSpec((tm, tm), const2)],
        out_specs=[kspec, kspec, kspec, pl.BlockSpec((1, LANES), const2)],
        out_shape=[jax.ShapeDtypeStruct((n, TOP_K), jnp.int32),
                   jax.ShapeDtypeStruct((n, TOP_K), f32),
                   jax.ShapeDtypeStruct((n, TOP_K), jnp.int32),
                   jax.ShapeDtypeStruct((1, LANES), f32)],
        scratch_shapes=[pltpu.VMEM((1, LANES), f32)],
        compiler_params=_cparams(1),
        name="router",
    )(h, w_router_p, bias_p, ltri)


GATHER_UNROLL = 8


def _row_copy(src_hbm, buf, sem, slot, row, k):
    src_row = pl.multiple_of(row * ROW_TILES, ROW_TILES)
    return pltpu.make_async_copy(src_hbm.at[pl.ds(src_row, ROW_TILES)],
                                 buf.at[slot, pl.ds(k * ROW_TILES, ROW_TILES)], sem.at[slot])


def _start_gather(src_hbm, idx_ref, buf, sem, slot, count):
    def body(o, carry):
        for u in range(GATHER_UNROLL):
            k = o * GATHER_UNROLL + u
            _row_copy(src_hbm, buf, sem, slot, idx_ref[0, 0, k], k).start()
        return carry
    lax.fori_loop(0, count // GATHER_UNROLL, body, 0)


SCATTER_TM = 128
BLOCK_TILE_ROWS = EXPERT_ROWS * ROW_TILES


def _scatter_kernel(padded_ref, pend_ref, pos_ref, hrows_hbm, xs_hbm, zbuf, zsem, sem):
    i = pl.program_id(0)
    nsteps = pl.num_programs(0)
    cnt = SCATTER_TM * TOP_K

    def zero_copy(e):
        dst = pl.multiple_of((pend_ref[e] - EXPERT_ROWS) * ROW_TILES, BLOCK_TILE_ROWS)
        return pltpu.make_async_copy(zbuf, xs_hbm.at[pl.ds(dst, BLOCK_TILE_ROWS)], zsem)

    @pl.when(i == 0)
    def _():
        zbuf[...] = jnp.zeros_like(zbuf)

        def start(e, carry):
            @pl.when(padded_ref[e] > 0)
            def _():
                zero_copy(e).start()
            return carry

        def wait(e, carry):
            @pl.when(padded_ref[e] > 0)
            def _():
                zero_copy(e).wait()
            return carry

        def tail_copy(j):
            dst = pl.multiple_of(j * BLOCK_TILE_ROWS, BLOCK_TILE_ROWS)
            return pltpu.make_async_copy(zbuf, xs_hbm.at[pl.ds(dst, BLOCK_TILE_ROWS)], zsem)

        def tail_start(j, carry):
            tail_copy(j).start()
            return carry

        def tail_wait(j, carry):
            tail_copy(j).wait()
            return carry

        n_used = pend_ref[N_EXPERTS - 1] // EXPERT_ROWS
        n_blocks = xs_hbm.shape[0] // BLOCK_TILE_ROWS
        lax.fori_loop(0, N_EXPERTS, start, 0)
        lax.fori_loop(n_used, n_blocks, tail_start, 0)
        lax.fori_loop(0, N_EXPERTS, wait, 0)
        lax.fori_loop(n_used, n_blocks, tail_wait, 0)

    def step_wait():
        pltpu.make_async_copy(hrows_hbm.at[pl.ds(0, cnt * ROW_TILES)], xs_hbm.at[pl.ds(0, cnt * ROW_TILES)], sem).wait()

    def body(t, carry):
        src = pl.multiple_of((i * SCATTER_TM + t) * ROW_TILES, ROW_TILES)
        for k in range(TOP_K):
            dst = pl.multiple_of(pos_ref[0, 0, t * TOP_K + k] * ROW_TILES, ROW_TILES)
            pltpu.make_async_copy(hrows_hbm.at[pl.ds(src, ROW_TILES)], xs_hbm.at[pl.ds(dst, ROW_TILES)], sem).start()
        return carry

    lax.fori_loop(0, SCATTER_TM, body, 0)

    @pl.when(i > 0)
    def _():
        step_wait()

    @pl.when(i == nsteps - 1)
    def _():
        step_wait()


def _scatter_rows(padded, pend, pos_tok3, h_rows2d, nb):
    n = h_rows2d.shape[0] // ROW_TILES
    grid_spec = pltpu.PrefetchScalarGridSpec(
        num_scalar_prefetch=2,
        grid=(n // SCATTER_TM,),
        in_specs=[
            pl.BlockSpec((1, 1, SCATTER_TM * TOP_K), lambda i, pa, pe: (i, 0, 0), memory_space=pltpu.SMEM),
            pl.BlockSpec(memory_space=pl.ANY),
        ],
        out_specs=pl.BlockSpec(memory_space=pl.ANY),
        scratch_shapes=[pltpu.VMEM((BLOCK_TILE_ROWS, LANES), f32),
                        pltpu.SemaphoreType.DMA(()),
                        pltpu.SemaphoreType.DMA(())],
    )
    return pl.pallas_call(
        _scatter_kernel,
        grid_spec=grid_spec,
        out_shape=jax.ShapeDtypeStruct((nb * BLOCK_TILE_ROWS, LANES), f32),
        compiler_params=_cparams(1),
        name="dispatch_scatter",
    )(padded, pend, pos_tok3, h_rows2d)


def _expert_kernel(be_ref, nused_ref, x_ref, wg_ref, wu_ref, wd_ref, o_ref, wgb, wub, wdb):
    i = pl.program_id(0)
    n_used = nused_ref[0]

    @pl.when(i < n_used)
    def _():
        prev = be_ref[jnp.maximum(i - 1, 0)]

        @pl.when((i == 0) | (be_ref[i] != prev))
        def _():
            wgb[...] = wg_ref[0].astype(bf16)
            wub[...] = wu_ref[0].astype(bf16)
            wdb[...] = wd_ref[0].astype(bf16)

        xb = _load_rows(x_ref, EXPERT_ROWS).astype(bf16)
        act = _silu(_dot(xb, wgb[...])) * _dot(xb, wub[...])
        y = _dot(act.astype(bf16), wdb[...])
        _store_rows(o_ref, y, EXPERT_ROWS)

    @pl.when(i >= n_used)
    def _():
        o_ref[...] = jnp.zeros_like(o_ref)


def _expert_mlp(block_e, n_used, xs, wg, wu, wd, nb):
    last = lambda i, nu: jnp.minimum(i, nu[0] - 1)
    wspec = lambda shape: pl.BlockSpec(shape, lambda i, be, nu: (be[last(i, nu)], 0, 0))
    grid_spec = pltpu.PrefetchScalarGridSpec(
        num_scalar_prefetch=2,
        grid=(nb,),
        in_specs=[
            pl.BlockSpec((BLOCK_TILE_ROWS, LANES), lambda i, be, nu: (last(i, nu), 0)),
            wspec((1, D_MODEL, D_EXPERT)), wspec((1, D_MODEL, D_EXPERT)), wspec((1, D_EXPERT, D_MODEL)),
        ],
        out_specs=pl.BlockSpec((BLOCK_TILE_ROWS, LANES), lambda i, be, nu: (i, 0)),
        scratch_shapes=[pltpu.VMEM((D_MODEL, D_EXPERT), bf16), pltpu.VMEM((D_MODEL, D_EXPERT), bf16),
                        pltpu.VMEM((D_EXPERT, D_MODEL), bf16)],
    )
    return pl.pallas_call(
        _expert_kernel,
        grid_spec=grid_spec,
        out_shape=jax.ShapeDtypeStruct((nb * BLOCK_TILE_ROWS, LANES), f32),
        compiler_params=_cparams(1),
        name="expert_mlp",
    )(block_e, n_used, xs, wg, wu, wd)


COMB_TM = 64


def _combine_kernel(idx_ref, idxn_ref, wts_ref, rep_ref, h_ref, hb_ref, sg_ref, su_ref, sd_ref, g_ref, b_ref, ys_hbm,
                    hf_ref, hb_out_ref, hr_ref, buf, comb, sem):
    i = pl.program_id(0)
    nb = pl.num_programs(0)
    slot = i % 2
    cnt = COMB_TM * TOP_K

    @pl.when(i == 0)
    def _():
        _start_gather(ys_hbm, idx_ref, buf, sem, 0, cnt)

    @pl.when(i + 1 < nb)
    def _():
        _start_gather(ys_hbm, idxn_ref, buf, sem, 1 - slot, cnt)

    pltpu.make_async_copy(ys_hbm.at[pl.ds(0, cnt * ROW_TILES)], buf.at[slot], sem.at[slot]).wait()
    rows = COMB_TM * ROW_TILES
    wts = wts_ref[...]
    wl = jnp.concatenate([jnp.broadcast_to(wts[:, k:k + 1], (COMB_TM, LANES)) for k in range(TOP_K)], axis=-1)
    w_rows = _sel_left(rep_ref[...], wl)
    acc = None
    for k in range(TOP_K):
        term = buf[slot, pl.ds(k * rows, rows), :] * w_rows[:, k * LANES:(k + 1) * LANES]
        acc = term if acc is None else acc + term
    comb[...] = acc
    routed = _load_rows(comb, COMB_TM)
    hb = hb_ref[...]
    act = _silu(_dot(hb, sg_ref[...])) * _dot(hb, su_ref[...])
    shared = _dot(act.astype(bf16), sd_ref[...])
    hn = _layer_norm(ALPHA * h_ref[...] + (routed + shared), g_ref[...], b_ref[...])
    hf_ref[...] = hn
    hb_out_ref[...] = hn.astype(bf16)
    _store_rows(hr_ref, hn, COMB_TM)


def _combine(pos3, wts, rep, h, h_bf, ws_gate, ws_up, ws_down, g, b, ys2d):
    n = h.shape[0]
    tm = COMB_TM
    nb = n // tm
    cnt = tm * TOP_K
    rowspec = pl.BlockSpec((tm, D_MODEL), lambda i: (i, 0))
    const2 = lambda i: (0, 0)
    return pl.pallas_call(
        _combine_kernel,
        grid=(nb,),
        in_specs=[
            pl.BlockSpec((1, 1, cnt), lambda i: (i, 0, 0), memory_space=pltpu.SMEM),
            pl.BlockSpec((1, 1, cnt), lambda i: (jnp.minimum(i + 1, nb - 1), 0, 0), memory_space=pltpu.SMEM),
            pl.BlockSpec((tm, TOP_K), lambda i: (i, 0)),
            pl.BlockSpec((tm * ROW_TILES, tm), const2),
            rowspec, rowspec,
            pl.BlockSpec((D_MODEL, D_SHARED), const2),
            pl.BlockSpec((D_MODEL, D_SHARED), const2),
            pl.BlockSpec((D_SHARED, D_MODEL), const2),
            pl.BlockSpec((1, D_MODEL), const2),
            pl.BlockSpec((1, D_MODEL), const2),
            pl.BlockSpec(memory_space=pl.ANY),
        ],
        out_specs=[rowspec, rowspec, pl.BlockSpec((tm * ROW_TILES, LANES), lambda i: (i, 0))],
        out_shape=[jax.ShapeDtypeStruct((n, D_MODEL), f32),
                   jax.ShapeDtypeStruct((n, D_MODEL), bf16),
                   jax.ShapeDtypeStruct((n * ROW_TILES, LANES), f32)],
        scratch_shapes=[pltpu.VMEM((2, cnt * ROW_TILES, LANES), f32),
                        pltpu.VMEM((tm * ROW_TILES, LANES), f32),
                        pltpu.SemaphoreType.DMA((2,))],
        compiler_params=_cparams(1),
        name="moe_combine",
    )(pos3, pos3, wts, rep, h, h_bf, ws_gate, ws_up, ws_down, g, b, ys2d)


def _dispatch(eidx, rank, counts_f, nb):
    n = eidx.shape[0]
    counts = counts_f[0, :N_EXPERTS].astype(jnp.int32)
    padded = (counts + EXPERT_ROWS - 1) // EXPERT_ROWS * EXPERT_ROWS
    pend = jnp.cumsum(padded)
    pstart = pend - padded
    starts = jnp.arange(nb, dtype=jnp.int32) * EXPERT_ROWS
    block_e = jnp.minimum(jnp.sum((pend[None, :] <= starts[:, None]).astype(jnp.int32), axis=1), N_EXPERTS - 1)
    n_used = (pend[-1] // EXPERT_ROWS).astype(jnp.int32).reshape(1)
    onehot = eidx[:, :, None] == jnp.arange(N_EXPERTS, dtype=jnp.int32)[None, None, :]
    pos = jnp.sum(jnp.where(onehot, pstart[None, None, :], 0), axis=-1) + rank
    pos_tok3 = pos.reshape(n // SCATTER_TM, 1, SCATTER_TM * TOP_K)
    tm = COMB_TM
    pos_k3 = pos.reshape(n // tm, tm, TOP_K).transpose(0, 2, 1).reshape(n // tm, 1, tm * TOP_K)
    return block_e.astype(jnp.int32), n_used, padded, pend, pos_tok3, pos_k3


def _consts():
    r = jnp.arange(CHUNK)
    tri = (r[:, None] >= r[None, :]).astype(bf16)
    trit = (r[:, None] <= r[None, :]).astype(bf16)
    lane = jnp.arange(LANES)
    ch = jnp.arange(SSD_HEADS * SSD_HEADDIM)
    ssd_expand = (lane[:, None] == (ch[None, :] // SSD_HEADDIM)).astype(bf16)
    hc = jnp.arange(HGRN_HEADS * HGRN_DIM)
    hm_rows = jnp.arange(HGRN_HEADS * SUB)
    hmask = ((hm_rows[:, None] // SUB) == (hc[None, :] // HGRN_DIM)).astype(f32)
    qr = jnp.arange((HGRN_NBLK - 1) * HGRN_HEADS * SUB)
    qblk = qr // (HGRN_HEADS * SUB) + 1
    kstart = SUB * jnp.arange(HGRN_NBLK - 1) * jnp.arange(1, HGRN_NBLK) // 2
    kblk = jnp.sum(lane[None, :] >= kstart[:, None], axis=0)
    offmask = ((qblk[:, None] == kblk[None, :]) & (lane[None, :] < HGRN_OFF_KEYS)).astype(f32)
    rt = jnp.arange(ROUTER_TM)
    ltri = (rt[:, None] > rt[None, :]).astype(bf16)
    rep = ((jnp.arange(COMB_TM * ROW_TILES)[:, None] // ROW_TILES) == jnp.arange(COMB_TM)[None, :]).astype(bf16)
    return tri, trit, ssd_expand, hmask, offmask, ltri, rep


def _hgrn_lower_bounds(lb_param):
    sm = jax.nn.softmax(lb_param.astype(f32), axis=0)
    return jnp.cumsum(sm, axis=0) - sm[0:1]


def kernel(x, mem, w_in, conv_w, conv_b, dt_bias, a_log, d_skip, ssd_norm, pool_w, pool_scale, hgrn_lb, hgrn_norm, w_mem_kv, w_branch, w_o, ln1_g, ln1_b, w_router, router_bias, w_exp_gate, w_exp_up, w_exp_down, w_sh_gate, w_sh_up, w_sh_down, ln2_g, ln2_b):
    bsz, seq, d = x.shape
    n = bsz * seq
    nk = n * TOP_K
    nb = -(-nk // EXPERT_ROWS) + N_EXPERTS
    tri, trit, ssd_expand, hmask, offmask, ltri, rep = _consts()
    lb_all = _hgrn_lower_bounds(hgrn_lb)

    c_dt0 = BRANCH_WIDTH * 2 + SSD_CONV_DIM
    c_dt1 = c_dt0 + SSD_HEADS
    c_gate = c_dt1 + 5 * BRANCH_WIDTH
    pad_lanes = LANES - SSD_HEADS

    h = x.reshape(n, d)
    h_bf = h.astype(bf16)
    h_rows = None
    mem_bf = mem.reshape(bsz * MEM_TOKENS, d).astype(bf16)

    for l in range(DEPTH):
        w_main = jnp.concatenate([w_in[l, :, :c_dt0], w_in[l, :, c_dt1:c_gate]], axis=1).astype(bf16)
        w_dt = jnp.pad(w_in[l, :, c_dt0:c_dt1], ((0, 0), (0, pad_lanes))).astype(bf16)
        w_gate = w_in[l, :, c_gate:].reshape(d, N_BRANCH, d).transpose(1, 0, 2).astype(bf16)

        proj = _matmul(h_bf, w_main, f32, 512, 1024)
        dtp = _matmul(h_bf, w_dt, f32, 512, LANES)
        kv = _matmul(mem_bf, w_mem_kv[l].astype(bf16), bf16, 512, 1024)

        br_pool = _pool_mixer(proj, pool_w[l].astype(bf16), pool_scale[l].reshape(1, -1), bsz, seq)
        br_ssd = _ssd_mixer(
            proj, dtp, conv_w[l], conv_b[l].reshape(1, -1),
            jnp.pad(dt_bias[l], (0, pad_lanes)).reshape(1, -1),
            jnp.pad(a_log[l], (0, pad_lanes)).reshape(1, -1),
            jnp.repeat(d_skip[l], SSD_HEADDIM).reshape(1, -1),
            ssd_norm[l].reshape(1, -1), (tri, trit, ssd_expand), bsz, seq)
        br_hgrn = _hgrn_mixer(proj, lb_all[l].reshape(1, -1), hgrn_norm[l].reshape(1, -1),
                              (tri, hmask, offmask), bsz, seq)
        br_mem = _mem_attention(proj, kv, bsz, seq)

        merged = _merge(h_bf, (br_pool, br_ssd, br_hgrn, br_mem), w_gate, w_branch[l].astype(bf16))
        h, h_bf, h_rows = _wo_ln(merged, w_o[l].astype(bf16), h, ln1_g[l].reshape(1, -1), ln1_b[l].reshape(1, -1))

        eidx, wts, rank, counts = _router(h, jnp.pad(w_router[l], ((0, 0), (0, LANES - N_EXPERTS))),
                                          jnp.pad(router_bias[l], (0, LANES - N_EXPERTS)).reshape(1, -1), ltri)
        block_e, n_used, padded, pend, pos_tok3, pos_k3 = _dispatch(eidx, rank, counts, nb)
        xs = _scatter_rows(padded, pend, pos_tok3, h_rows, nb)
        ys = _expert_mlp(block_e, n_used, xs, w_exp_gate[l], w_exp_up[l], w_exp_down[l], nb)
        h, h_bf, h_rows = _combine(pos_k3, wts, rep, h, h_bf, w_sh_gate[l].astype(bf16), w_sh_up[l].astype(bf16),
                                   w_sh_down[l].astype(bf16), ln2_g[l].reshape(1, -1), ln2_b[l].reshape(1, -1), ys)
    return h.reshape(bsz, seq, d)
```

```python
import functools

import jax
import jax.numpy as jnp
from jax import lax
from jax.experimental import pallas as pl
from jax.experimental.pallas import tpu as pltpu

f32 = jnp.float32
bf16 = jnp.bfloat16

D_MODEL = 2048
DEPTH = 4
CHUNK = 64
N_BRANCH = 4
BRANCH_WIDTH = 1024
POOL_WINDOWS = (2, 4, 8, 16)
POOL_GROUP = 256
SSD_HEADS = 16
SSD_HEADDIM = 64
SSD_STATE = 128
SSD_GROUPS = 4
SSD_CONV = 4
SSD_CONV_DIM = 2048
HGRN_HEADS = 8
HGRN_DIM = 128
MEM_TOKENS = 256
MEM_HEADS = 4
MEM_HEAD_DIM = 256
N_EXPERTS = 64
TOP_K = 8
N_GROUPS = 8
TOPK_GROUPS = 4
D_EXPERT = 256
D_SHARED = 256
ROUTED_SCALE = 2.5
MASK_SCORE = -1e4
ALPHA = (2 * DEPTH) ** 0.25
EPS = 1e-5

LANES = 128
ROW_TILES = D_MODEL // LANES
EXPERT_ROWS = 128
SUB = 16
VMEM_LIMIT = 56 * 1024 * 1024

NEG_INF = float("-inf")


def _cparams(n_axes):
    return pltpu.CompilerParams(dimension_semantics=("arbitrary",) * n_axes,
                                vmem_limit_bytes=VMEM_LIMIT)


def _sigmoid(x):
    return jax.nn.sigmoid(x)


def _silu(x):
    return x * jax.nn.sigmoid(x)


def _split3(x):
    hi = x.astype(bf16)
    r1 = x - hi.astype(f32)
    mid = r1.astype(bf16)
    lo = (r1 - mid.astype(f32)).astype(bf16)
    return hi, mid, lo


def _dot(a, b):
    return jnp.dot(a, b, preferred_element_type=f32)


def _dot_nt(a, b):
    return lax.dot_general(a, b, (((1,), (1,)), ((), ())), preferred_element_type=f32)


def _dot_tn(a, b):
    return lax.dot_general(a, b, (((0,), (0,)), ((), ())), preferred_element_type=f32)


def _sel_left(sel01, x):
    return sum(_dot(sel01, p) for p in _split3(x))


def _sel_right(x, sel01):
    return sum(_dot(p, sel01) for p in _split3(x))


def _mm_kernel(x_ref, w_ref, o_ref):
    o_ref[...] = _dot(x_ref[...], w_ref[...]).astype(o_ref.dtype)


def _matmul(x, w, out_dtype, tm, tn):
    m, k = x.shape
    n = w.shape[1]
    return pl.pallas_call(
        _mm_kernel,
        grid=(n // tn, m // tm),
        in_specs=[pl.BlockSpec((tm, k), lambda j, i: (i, 0)),
                  pl.BlockSpec((k, tn), lambda j, i: (0, j))],
        out_specs=pl.BlockSpec((tm, tn), lambda j, i: (i, j)),
        out_shape=jax.ShapeDtypeStruct((m, n), out_dtype),
        compiler_params=_cparams(2),
        name="matmul",
    )(x, w)


POOL_T = 512
POOL_HALO = 16


def _pool_kernel(u_ref, halo_ref, w_ref, scale_ref, o_ref):
    i = pl.program_id(1)
    u = u_ref[...]
    halo = jnp.where(i > 0, halo_ref[...], 0.0)
    ext = jnp.concatenate([halo, u], axis=0)
    t = i * POOL_T + lax.broadcasted_iota(jnp.int32, (POOL_T, 1), 0)
    outs = []
    for gi, w in enumerate(POOL_WINDOWS):
        xg = ext[:, gi * POOL_GROUP:(gi + 1) * POOL_GROUP]
        s = xg
        span = 1
        while span < w:
            s = s[span:, :] + s[:-span, :]
            span *= 2
        s = s[s.shape[0] - POOL_T:, :]
        cnt = jnp.minimum(t + 1, w).astype(f32)
        d = s / cnt - u[:, gi * POOL_GROUP:(gi + 1) * POOL_GROUP]
        outs.append(_dot(d.astype(bf16), w_ref[gi]))
    y = jnp.concatenate(outs, axis=-1) * scale_ref[...]
    o_ref[...] = y.astype(o_ref.dtype)


def _pool_mixer(proj, pool_w, pool_scale, bsz, seq):
    n = bsz * seq
    nt = seq // POOL_T
    return pl.pallas_call(
        _pool_kernel,
        grid=(bsz, nt),
        in_specs=[
            pl.BlockSpec((POOL_T, BRANCH_WIDTH), lambda b, i: (b * nt + i, 0)),
            pl.BlockSpec((POOL_HALO, BRANCH_WIDTH),
                         lambda b, i: (jnp.maximum((b * nt + i) * (POOL_T // POOL_HALO) - 1, 0), 0)),
            pl.BlockSpec((len(POOL_WINDOWS), POOL_GROUP, POOL_GROUP), lambda b, i: (0, 0, 0)),
            pl.BlockSpec((1, BRANCH_WIDTH), lambda b, i: (0, 0)),
        ],
        out_specs=pl.BlockSpec((POOL_T, BRANCH_WIDTH), lambda b, i: (b * nt + i, 0)),
        out_shape=jax.ShapeDtypeStruct((n, BRANCH_WIDTH), bf16),
        compiler_params=_cparams(2),
        name="pool_mixer",
    )(proj, proj, pool_w, pool_scale)


CONV_TAIL = 8


def _softplus(x):
    return jnp.maximum(x, 0.0) + jnp.log1p(jnp.exp(-jnp.abs(x)))


def _ssd_kernel(z_ref, xbc_ref, dt_ref, cw_ref, cb_ref, dtb_ref, alog_ref, dskip_ref, nw_ref,
                tri_ref, trit_ref, exp_ref, o_ref, state_ref, tail_ref):
    i = pl.program_id(1)

    @pl.when(i == 0)
    def _():
        state_ref[...] = jnp.zeros_like(state_ref)
        tail_ref[...] = jnp.zeros_like(tail_ref)

    xbc = xbc_ref[...]
    ext = jnp.concatenate([tail_ref[...], xbc], axis=0)
    cw = cw_ref[...]
    conv = cb_ref[...]
    for k in range(SSD_CONV):
        off = CONV_TAIL - (SSD_CONV - 1) + k
        conv = conv + cw[k:k + 1, :] * ext[off:off + CHUNK, :]
    tail_ref[...] = xbc[CHUNK - CONV_TAIL:, :]
    act = _silu(conv)
    inner = SSD_HEADS * SSD_HEADDIM
    gw = SSD_GROUPS * SSD_STATE
    xs = act[:, :inner]
    bmat = act[:, inner:inner + gw]
    cmat = act[:, inner + gw:]

    dt = _softplus(dt_ref[...] + dtb_ref[...])
    a = -jnp.exp(alog_ref[...])
    ad = dt * a
    ad_parts = _split3(ad)
    tri = tri_ref[...]
    acs = sum(_dot(tri, p) for p in ad_parts)
    acs_t = sum(_dot_tn(p, trit_ref[...]) for p in ad_parts)
    a_last = acs[CHUNK - 1:CHUNK, :]
    expand = exp_ref[...]
    dt_e = _sel_right(dt, expand)
    eacs_e = _sel_right(jnp.exp(acs), expand)
    dec_e = _sel_right(jnp.exp(a_last - acs), expand)
    x = xs * dt_e
    xb = x.astype(bf16)
    xd = (x * dec_e).astype(bf16)
    elast_e = eacs_e[CHUNK - 1:CHUNK, :]

    row = lax.broadcasted_iota(jnp.int32, (CHUNK, CHUNK), 0)
    col = lax.broadcasted_iota(jnp.int32, (CHUNK, CHUNK), 1)
    causal = row >= col
    hpg = SSD_HEADS // SSD_GROUPS
    gch = hpg * SSD_HEADDIM
    y_diag = []
    y_off = []
    for g in range(SSD_GROUPS):
        bg = bmat[:, g * SSD_STATE:(g + 1) * SSD_STATE].astype(bf16)
        cg = cmat[:, g * SSD_STATE:(g + 1) * SSD_STATE].astype(bf16)
        cb = _dot_nt(cg, bg)
        for hh in range(hpg):
            h = g * hpg + hh
            diff = acs[:, h:h + 1] - acs_t[h:h + 1, :]
            decay = jnp.where(causal, jnp.exp(jnp.where(causal, diff, 0.0)), 0.0)
            sc = (cb * decay).astype(bf16)
            y_diag.append(_dot(sc, xb[:, h * SSD_HEADDIM:(h + 1) * SSD_HEADDIM]))
        st = state_ref[:, g * gch:(g + 1) * gch]
        y_off.append(_dot(cg, st.astype(bf16)))
        upd = _dot_tn(bg, xd[:, g * gch:(g + 1) * gch])
        state_ref[:, g * gch:(g + 1) * gch] = st * elast_e[:, g * gch:(g + 1) * gch] + upd
    y = (jnp.concatenate(y_diag, axis=-1) + jnp.concatenate(y_off, axis=-1) * eacs_e
         + xs * dskip_ref[...])
    y = y * _silu(z_ref[...])
    nw = nw_ref[...]
    ngw = inner // SSD_GROUPS
    outs = []
    for g in range(SSD_GROUPS):
        yg = y[:, g * ngw:(g + 1) * ngw]
        ms = jnp.mean(yg * yg, axis=-1, keepdims=True)
        outs.append(yg * lax.rsqrt(ms + EPS) * nw[:, g * ngw:(g + 1) * ngw])
    o_ref[...] = jnp.concatenate(outs, axis=-1).astype(o_ref.dtype)


def _ssd_mixer(proj, dtp, conv_w, conv_b, dt_bias, a_log, d_skip_e, norm_w, consts, bsz, seq):
    n = bsz * seq
    nc = seq // CHUNK
    tri, trit, expand = consts
    rowmap = lambda b, i: (b * nc + i, 0)
    const2 = lambda b, i: (0, 0)
    inner = SSD_HEADS * SSD_HEADDIM
    return pl.pallas_call(
        _ssd_kernel,
        grid=(bsz, nc),
        in_specs=[
            pl.BlockSpec((CHUNK, inner), lambda b, i: (b * nc + i, 1)),
            pl.BlockSpec((CHUNK, SSD_CONV_DIM), lambda b, i: (b * nc + i, 1)),
            pl.BlockSpec((CHUNK, LANES), rowmap),
            pl.BlockSpec((SSD_CONV, SSD_CONV_DIM), const2),
            pl.BlockSpec((1, SSD_CONV_DIM), const2),
            pl.BlockSpec((1, LANES), const2),
            pl.BlockSpec((1, LANES), const2),
            pl.BlockSpec((1, inner), const2),
            pl.BlockSpec((1, inner), const2),
            pl.BlockSpec((CHUNK, CHUNK), const2),
            pl.BlockSpec((CHUNK, CHUNK), const2),
            pl.BlockSpec((LANES, inner), const2),
        ],
        out_specs=pl.BlockSpec((CHUNK, inner), rowmap),
        out_shape=jax.ShapeDtypeStruct((n, inner), bf16),
        scratch_shapes=[pltpu.VMEM((SSD_STATE, inner), f32),
                        pltpu.VMEM((CONV_TAIL, SSD_CONV_DIM), f32)],
        compiler_params=_cparams(2),
        name="ssd_mixer",
    )(proj, proj, dtp, conv_w, conv_b, dt_bias, a_log, d_skip_e, norm_w, tri, trit, expand)


LOG2E = 1.4426950408889634
HGRN_NBLK = CHUNK // SUB
HGRN_OFF_KEYS = SUB * (HGRN_NBLK - 1) * HGRN_NBLK // 2


def _hgrn_chunk(q, f, v, g, lb, nw, tri, hmask, offmask, state_ref):
    width = HGRN_HEADS * HGRN_DIM
    qf = _silu(q)
    kf = (1.0 - lb) * _sigmoid(-f)
    b2 = _sel_left(tri, jnp.log1p(-kf)) * LOG2E
    c2 = b2 - jnp.log2(kf)
    b_last = b2[CHUNK - 1:CHUNK, :]
    vb = v.astype(bf16)

    qd = (qf * jnp.exp2(b2)).astype(bf16)
    kd = jnp.exp2(b_last - c2).astype(bf16)
    eb_last = jnp.exp2(b_last)
    o_inter = []
    for h in range(HGRN_HEADS):
        sl = slice(h * HGRN_DIM, (h + 1) * HGRN_DIM)
        st = state_ref[h]
        o_inter.append(_dot_nt(qd[:, sl], st.astype(bf16)))
        state_ref[h] = st * eb_last[:, sl] + _dot_tn(vb[:, sl], kd[:, sl])
    o = jnp.concatenate(o_inter, axis=-1)

    qm, kk, vv = [], [], []
    for blk in range(1, HGRN_NBLK):
        r0 = blk * SUB
        anchor = b2[r0 - 1:r0, :]
        qq = qf[r0:r0 + SUB, :] * jnp.exp2(b2[r0:r0 + SUB, :] - anchor)
        qm.append((jnp.concatenate([qq] * HGRN_HEADS, axis=0) * hmask).astype(bf16))
        kk.append(jnp.exp2(anchor - c2[:r0, :]).astype(bf16))
        vv.append(vb[:r0, :])
    zpad = jnp.zeros((LANES - HGRN_OFF_KEYS, width), bf16)
    att = _dot_nt(jnp.concatenate(qm, axis=0), jnp.concatenate(kk + [zpad], axis=0))
    res = _dot((att * offmask).astype(bf16), jnp.concatenate(vv + [zpad], axis=0))
    rows = [jnp.zeros((SUB, width), f32)]
    for blk in range(1, HGRN_NBLK):
        base = (blk - 1) * HGRN_HEADS * SUB
        acc = None
        for h in range(HGRN_HEADS):
            part = res[base + h * SUB:base + (h + 1) * SUB, :] * hmask[h * SUB:(h + 1) * SUB, :]
            acc = part if acc is None else acc + part
        rows.append(acc)
    o = o + jnp.concatenate(rows, axis=0)

    tpos = lax.broadcasted_iota(jnp.int32, (CHUNK, 1), 0) & (SUB - 1)
    diag = [None] * HGRN_HEADS
    for d in range(SUB):
        cs = c2 if d == 0 else pltpu.roll(c2, d, 0)
        vs = v if d == 0 else pltpu.roll(v, d, 0)
        p = qf * jnp.exp2(b2 - cs)
        same_block = tpos >= d
        for h in range(HGRN_HEADS):
            sl = slice(h * HGRN_DIM, (h + 1) * HGRN_DIM)
            att = jnp.where(same_block, jnp.sum(p[:, sl], axis=-1, keepdims=True), 0.0)
            term = att * vs[:, sl]
            diag[h] = term if diag[h] is None else diag[h] + term
    o = o + jnp.concatenate(diag, axis=-1)

    gate = _silu(g)
    outs = []
    for h in range(HGRN_HEADS):
        sl = slice(h * HGRN_DIM, (h + 1) * HGRN_DIM)
        oh = o[:, sl]
        ms = jnp.mean(oh * oh, axis=-1, keepdims=True)
        outs.append(oh * lax.rsqrt(ms + EPS) * nw * gate[:, sl])
    return jnp.concatenate(outs, axis=-1)


def _hgrn_kernel(q_ref, f_ref, i_ref, g_ref, lb_ref, nw_ref, tri_ref, hmask_ref, offmask_ref, o_ref, state_ref):
    c = pl.program_id(0)

    @pl.when(c == 0)
    def _():
        state_ref[...] = jnp.zeros_like(state_ref)

    for bi in range(q_ref.shape[0]):
        out = _hgrn_chunk(q_ref[bi], f_ref[bi], i_ref[bi], g_ref[bi], lb_ref[...], nw_ref[...], tri_ref[...],
                          hmask_ref[...], offmask_ref[...], state_ref.at[bi])
        o_ref[bi] = out.astype(o_ref.dtype)


def _hgrn_mixer(proj, lb, norm_w, consts, bsz, seq):
    n = bsz * seq
    nc = seq // CHUNK
    tri, hmask, offmask = consts
    width = HGRN_HEADS * HGRN_DIM
    const2 = lambda i: (0, 0)
    col = lambda j: (lambda i: (0, i, j))
    proj3 = proj.reshape(bsz, seq, proj.shape[1])
    qrows = (HGRN_NBLK - 1) * HGRN_HEADS * SUB
    out = pl.pallas_call(
        _hgrn_kernel,
        grid=(nc,),
        in_specs=[
            pl.BlockSpec((bsz, CHUNK, width), col(4)),
            pl.BlockSpec((bsz, CHUNK, width), col(5)),
            pl.BlockSpec((bsz, CHUNK, width), col(6)),
            pl.BlockSpec((bsz, CHUNK, width), col(7)),
            pl.BlockSpec((1, width), const2),
            pl.BlockSpec((1, HGRN_DIM), const2),
            pl.BlockSpec((CHUNK, CHUNK), const2),
            pl.BlockSpec((HGRN_HEADS * SUB, width), const2),
            pl.BlockSpec((qrows, LANES), const2),
        ],
        out_specs=pl.BlockSpec((bsz, CHUNK, width), lambda i: (0, i, 0)),
        out_shape=jax.ShapeDtypeStruct((bsz, seq, width), bf16),
        scratch_shapes=[pltpu.VMEM((bsz, HGRN_HEADS, HGRN_DIM, HGRN_DIM), f32)],
        compiler_params=_cparams(1),
        name="hgrn_mixer",
    )(proj3, proj3, proj3, proj3, lb, norm_w, tri, hmask, offmask)
    return out.reshape(n, width)


MEM_T = 512


def _memattn_kernel(q_ref, k_ref, v_ref, o_ref):
    q = q_ref[...].astype(bf16)
    outs = []
    for h in range(MEM_HEADS):
        sl = slice(h * MEM_HEAD_DIM, (h + 1) * MEM_HEAD_DIM)
        s = _dot_nt(q[:, sl], k_ref[:, sl]) * (MEM_HEAD_DIM ** -0.5)
        m = jnp.max(s, axis=-1, keepdims=True)
        e = jnp.exp(s - m)
        p = e / jnp.sum(e, axis=-1, keepdims=True)
        outs.append(_dot(p.astype(bf16), v_ref[:, sl]))
    o_ref[...] = jnp.concatenate(outs, axis=-1).astype(o_ref.dtype)


def _mem_attention(proj, kv, bsz, seq):
    n = bsz * seq
    nt = seq // MEM_T
    width = MEM_HEADS * MEM_HEAD_DIM
    return pl.pallas_call(
        _memattn_kernel,
        grid=(bsz, nt),
        in_specs=[
            pl.BlockSpec((MEM_T, width), lambda b, i: (b * nt + i, 8)),
            pl.BlockSpec((MEM_TOKENS, width), lambda b, i: (b, 0)),
            pl.BlockSpec((MEM_TOKENS, width), lambda b, i: (b, 1)),
        ],
        out_specs=pl.BlockSpec((MEM_T, width), lambda b, i: (b * nt + i, 0)),
        out_shape=jax.ShapeDtypeStruct((n, width), bf16),
        compiler_params=_cparams(2),
        name="mem_attention",
    )(proj, kv, kv)


MERGE_TM = 512
MERGE_TN = 512


def _merge_kernel(h_ref, b0_ref, b1_ref, b2_ref, b3_ref, wg_ref, wb_ref, o_ref):
    h = h_ref[...]
    acc = None
    for n, br in enumerate((b0_ref, b1_ref, b2_ref, b3_ref)):
        gate = _sigmoid(_dot(h, wg_ref[n]))
        term = gate * _dot(br[...], wb_ref[n])
        acc = term if acc is None else acc + term
    o_ref[...] = acc.astype(o_ref.dtype)


def _merge(h_bf, branches, w_gate, w_branch):
    n = h_bf.shape[0]
    tm, tn = MERGE_TM, MERGE_TN
    bspec = pl.BlockSpec((tm, BRANCH_WIDTH), lambda j, i: (i, 0))
    return pl.pallas_call(
        _merge_kernel,
        grid=(D_MODEL // tn, n // tm),
        in_specs=[pl.BlockSpec((tm, D_MODEL), lambda j, i: (i, 0)), bspec, bspec, bspec, bspec,
                  pl.BlockSpec((N_BRANCH, D_MODEL, tn), lambda j, i: (0, 0, j)),
                  pl.BlockSpec((N_BRANCH, BRANCH_WIDTH, tn), lambda j, i: (0, 0, j))],
        out_specs=pl.BlockSpec((tm, tn), lambda j, i: (i, j)),
        out_shape=jax.ShapeDtypeStruct((n, D_MODEL), bf16),
        compiler_params=_cparams(2),
        name="gated_merge",
    )(h_bf, *branches, w_gate, w_branch)


def _layer_norm(x, g, b):
    mu = jnp.mean(x, axis=-1, keepdims=True)
    xc = x - mu
    var = jnp.mean(xc * xc, axis=-1, keepdims=True)
    return xc * lax.rsqrt(var + EPS) * g + b


def _store_rows(rows_ref, x, tm):
    for j in range(ROW_TILES):
        rows_ref[pl.ds(j, tm, stride=ROW_TILES), :] = x[:, j * LANES:(j + 1) * LANES]


def _load_rows(rows_ref, tm, lead=None):
    parts = []
    for j in range(ROW_TILES):
        if lead is None:
            parts.append(rows_ref[pl.ds(j, tm, stride=ROW_TILES), :])
        else:
            parts.append(rows_ref[lead, pl.ds(j, tm, stride=ROW_TILES), :])
    return jnp.concatenate(parts, axis=-1)


WO_TM = 256


def _wo_ln_kernel(m_ref, w_ref, h_ref, g_ref, b_ref, hf_ref, hb_ref, hr_ref):
    y = ALPHA * h_ref[...] + _dot(m_ref[...], w_ref[...])
    hn = _layer_norm(y, g_ref[...], b_ref[...])
    hf_ref[...] = hn
    hb_ref[...] = hn.astype(bf16)
    _store_rows(hr_ref, hn, WO_TM)


def _wo_ln(merged, w_o, h, g, b):
    n = h.shape[0]
    tm = WO_TM
    rowspec = pl.BlockSpec((tm, D_MODEL), lambda i: (i, 0))
    const2 = lambda i: (0, 0)
    return pl.pallas_call(
        _wo_ln_kernel,
        grid=(n // tm,),
        in_specs=[rowspec, pl.BlockSpec((D_MODEL, D_MODEL), const2), rowspec,
                  pl.BlockSpec((1, D_MODEL), const2), pl.BlockSpec((1, D_MODEL), const2)],
        out_specs=[rowspec, rowspec, pl.BlockSpec((tm * ROW_TILES, LANES), lambda i: (i, 0))],
        out_shape=[jax.ShapeDtypeStruct((n, D_MODEL), f32),
                   jax.ShapeDtypeStruct((n, D_MODEL), bf16),
                   jax.ShapeDtypeStruct((n * ROW_TILES, LANES), f32)],
        compiler_params=_cparams(1),
        name="wo_layernorm",
    )(merged, w_o, h, g, b)


ROUTER_TM = 256


def _router_kernel(h_ref, w_ref, bias_ref, ltri_ref, eidx_ref, wts_ref, rank_ref, counts_ref, cnt_ref):
    step = pl.program_id(0)

    @pl.when(step == 0)
    def _():
        cnt_ref[...] = jnp.zeros_like(cnt_ref)

    h = h_ref[...]
    w = w_ref[...]
    h_hi = h.astype(bf16)
    h_lo = (h - h_hi.astype(f32)).astype(bf16)
    w_hi = w.astype(bf16)
    w_lo = (w - w_hi.astype(f32)).astype(bf16)
    logits = _dot(h_hi, w_hi) + _dot(h_hi, w_lo) + _dot(h_lo, w_hi)
    scores = _sigmoid(logits)
    lane_i = lax.broadcasted_iota(jnp.int32, logits.shape, 1)
    lane = lane_i.astype(f32)
    real = lane_i < N_EXPERTS
    choice = jnp.where(real, scores + bias_ref[...], NEG_INF)
    gsize = N_EXPERTS // N_GROUPS
    grp = lax.shift_right_logical(lane_i, gsize.bit_length() - 1)
    big = float(LANES)

    def first_argmax(x):
        m = jnp.max(x, axis=-1, keepdims=True)
        idx = jnp.min(jnp.where(x == m, lane, big), axis=-1, keepdims=True)
        return m, idx

    gscore = []
    for g in range(N_GROUPS):
        cg = jnp.where(grp == g, choice, NEG_INF)
        m1, i1 = first_argmax(cg)
        m2 = jnp.max(jnp.where(lane == i1, NEG_INF, cg), axis=-1, keepdims=True)
        gscore.append(m1 + m2)
    keep = jnp.zeros(logits.shape, f32)
    for g in range(N_GROUPS):
        rank = jnp.zeros_like(gscore[g])
        for o in range(N_GROUPS):
            if o == g:
                continue
            ahead = (gscore[o] > gscore[g]) if o > g else (gscore[o] >= gscore[g])
            rank = rank + jnp.where(ahead, 1.0, 0.0)
        keep = jnp.where((grp == g) & (rank < TOPK_GROUPS), 1.0, keep)
    cur = jnp.where(real, jnp.where(keep > 0.0, choice, MASK_SCORE), NEG_INF)
    idxs, ws, hits = [], [], []
    sel = jnp.zeros(logits.shape, f32)
    for _ in range(TOP_K):
        _, ik = first_argmax(cur)
        hit = lane == ik
        ws.append(jnp.sum(jnp.where(hit, scores, 0.0), axis=-1, keepdims=True))
        idxs.append(ik)
        hits.append(hit)
        sel = jnp.where(hit, 1.0, sel)
        cur = jnp.where(hit, NEG_INF, cur)
    wsum = ws[0]
    for wk in ws[1:]:
        wsum = wsum + wk
    eidx_ref[...] = jnp.concatenate(idxs, axis=-1).astype(jnp.int32)
    wts_ref[...] = jnp.concatenate([wk / wsum * ROUTED_SCALE for wk in ws], axis=-1)
    before = _dot(ltri_ref[...], sel.astype(bf16)) + cnt_ref[...]
    ranks = [jnp.sum(jnp.where(hit, before, 0.0), axis=-1, keepdims=True) for hit in hits]
    rank_ref[...] = jnp.concatenate(ranks, axis=-1).astype(jnp.int32)
    cnt_ref[...] = cnt_ref[...] + jnp.sum(sel, axis=0, keepdims=True)
    counts_ref[...] = cnt_ref[...]


def _router(h, w_router_p, bias_p, ltri):
    n = h.shape[0]
    tm = ROUTER_TM
    const2 = lambda i: (0, 0)
    kspec = pl.BlockSpec((tm, TOP_K), lambda i: (i, 0))
    return pl.pallas_call(
        _router_kernel,
        grid=(n // tm,),
        in_specs=[pl.BlockSpec((tm, D_MODEL), lambda i: (i, 0)),
                  pl.BlockSpec((D_MODEL, LANES), const2),
                  pl.BlockSpec((1, LANES), const2),
                  pl.BlockSpec((tm, tm), const2)],
        out_specs=[kspec, kspec, kspec, pl.BlockSpec((1, LANES), const2)],
        out_shape=[jax.ShapeDtypeStruct((n, TOP_K), jnp.int32),
                   jax.ShapeDtypeStruct((n, TOP_K), f32),
                   jax.ShapeDtypeStruct((n, TOP_K), jnp.int32),
                   jax.ShapeDtypeStruct((1, LANES), f32)],
        scratch_shapes=[pltpu.VMEM((1, LANES), f32)],
        compiler_params=_cparams(1),
        name="router",
    )(h, w_router_p, bias_p, ltri)


GATHER_UNROLL = 8


def _row_copy(src_hbm, buf, sem, slot, row, k):
    src_row = pl.multiple_of(row * ROW_TILES, ROW_TILES)
    return pltpu.make_async_copy(src_hbm.at[pl.ds(src_row, ROW_TILES)],
                                 buf.at[slot, pl.ds(k * ROW_TILES, ROW_TILES)], sem.at[slot])


def _start_gather(src_hbm, idx_ref, buf, sem, slot, count):
    def body(o, carry):
        for u in range(GATHER_UNROLL):
            k = o * GATHER_UNROLL + u
            _row_copy(src_hbm, buf, sem, slot, idx_ref[0, 0, k], k).start()
        return carry
    lax.fori_loop(0, count // GATHER_UNROLL, body, 0)


SCATTER_TM = 128
SCATTER_SLOTS = 3
BLOCK_TILE_ROWS = EXPERT_ROWS * ROW_TILES


def _scatter_kernel(padded_ref, pend_ref, pos_ref, hrows_hbm, xs_hbm, stage, zbuf, lsem, ssem, zsem):
    i = pl.program_id(0)
    nsteps = pl.num_programs(0)
    cnt = SCATTER_TM * TOP_K

    def zero_copy(e):
        dst = pl.multiple_of((pend_ref[e] - EXPERT_ROWS) * ROW_TILES, BLOCK_TILE_ROWS)
        return pltpu.make_async_copy(zbuf, xs_hbm.at[pl.ds(dst, BLOCK_TILE_ROWS)], zsem)

    @pl.when(i == 0)
    def _():
        zbuf[...] = jnp.zeros_like(zbuf)

        def start(e, carry):
            @pl.when(padded_ref[e] > 0)
            def _():
                zero_copy(e).start()
            return carry

        def wait(e, carry):
            @pl.when(padded_ref[e] > 0)
            def _():
                zero_copy(e).wait()
            return carry

        def tail_copy(j):
            dst = pl.multiple_of(j * BLOCK_TILE_ROWS, BLOCK_TILE_ROWS)
            return pltpu.make_async_copy(zbuf, xs_hbm.at[pl.ds(dst, BLOCK_TILE_ROWS)], zsem)

        def tail_start(j, carry):
            tail_copy(j).start()
            return carry

        def tail_wait(j, carry):
            tail_copy(j).wait()
            return carry

        n_used = pend_ref[N_EXPERTS - 1] // EXPERT_ROWS
        n_blocks = xs_hbm.shape[0] // BLOCK_TILE_ROWS
        lax.fori_loop(0, N_EXPERTS, start, 0)
        lax.fori_loop(n_used, n_blocks, tail_start, 0)
        lax.fori_loop(0, N_EXPERTS, wait, 0)
        lax.fori_loop(n_used, n_blocks, tail_wait, 0)

    tile_rows = SCATTER_TM * ROW_TILES
    slot = lax.rem(i, SCATTER_SLOTS)
    nxt = lax.rem(i + 1, SCATTER_SLOTS)

    def load(j, s):
        src = pl.multiple_of(j * tile_rows, tile_rows)
        return pltpu.make_async_copy(hrows_hbm.at[pl.ds(src, tile_rows)], stage.at[s], lsem.at[s])

    def scatter_wait(s):
        pltpu.make_async_copy(xs_hbm.at[pl.ds(0, cnt * ROW_TILES)], xs_hbm.at[pl.ds(0, cnt * ROW_TILES)],
                              ssem.at[s]).wait()

    @pl.when(i == 0)
    def _():
        load(0, 0).start()

    @pl.when(i >= SCATTER_SLOTS - 1)
    def _():
        scatter_wait(nxt)

    @pl.when(i + 1 < nsteps)
    def _():
        load(i + 1, nxt).start()

    load(i, slot).wait()

    def body(t, carry):
        src = pl.multiple_of(t * ROW_TILES, ROW_TILES)
        for k in range(TOP_K):
            dst = pl.multiple_of(pos_ref[0, 0, t * TOP_K + k] * ROW_TILES, ROW_TILES)
            pltpu.make_async_copy(stage.at[slot, pl.ds(src, ROW_TILES)], xs_hbm.at[pl.ds(dst, ROW_TILES)],
                                  ssem.at[slot]).start()
        return carry

    lax.fori_loop(0, SCATTER_TM, body, 0)

    @pl.when(i == nsteps - 1)
    def _():
        for back in range(SCATTER_SLOTS - 2, -1, -1):
            @pl.when(i >= back)
            def _():
                scatter_wait(lax.rem(i - back + SCATTER_SLOTS, SCATTER_SLOTS))


def _scatter_rows(padded, pend, pos_tok3, h_rows2d, nb):
    n = h_rows2d.shape[0] // ROW_TILES
    grid_spec = pltpu.PrefetchScalarGridSpec(
        num_scalar_prefetch=2,
        grid=(n // SCATTER_TM,),
        in_specs=[
            pl.BlockSpec((1, 1, SCATTER_TM * TOP_K), lambda i, pa, pe: (i, 0, 0), memory_space=pltpu.SMEM),
            pl.BlockSpec(memory_space=pl.ANY),
        ],
        out_specs=pl.BlockSpec(memory_space=pl.ANY),
        scratch_shapes=[pltpu.VMEM((SCATTER_SLOTS, SCATTER_TM * ROW_TILES, LANES), f32),
                        pltpu.VMEM((BLOCK_TILE_ROWS, LANES), f32),
                        pltpu.SemaphoreType.DMA((SCATTER_SLOTS,)),
                        pltpu.SemaphoreType.DMA((SCATTER_SLOTS,)),
                        pltpu.SemaphoreType.DMA(())],
    )
    return pl.pallas_call(
        _scatter_kernel,
        grid_spec=grid_spec,
        out_shape=jax.ShapeDtypeStruct((nb * BLOCK_TILE_ROWS, LANES), f32),
        compiler_params=_cparams(1),
        name="dispatch_scatter",
    )(padded, pend, pos_tok3, h_rows2d)


def _expert_kernel(be_ref, nused_ref, x_ref, wg_ref, wu_ref, wd_ref, o_ref, wgb, wub, wdb):
    i = pl.program_id(0)
    n_used = nused_ref[0]

    @pl.when(i < n_used)
    def _():
        prev = be_ref[jnp.maximum(i - 1, 0)]

        @pl.when((i == 0) | (be_ref[i] != prev))
        def _():
            wgb[...] = wg_ref[0, 0].astype(bf16)
            wub[...] = wu_ref[0, 0].astype(bf16)
            wdb[...] = wd_ref[0, 0].astype(bf16)

        xb = _load_rows(x_ref, EXPERT_ROWS).astype(bf16)
        act = _silu(_dot(xb, wgb[...])) * _dot(xb, wub[...])
        y = _dot(act.astype(bf16), wdb[...])
        _store_rows(o_ref, y, EXPERT_ROWS)

    @pl.when(i >= n_used)
    def _():
        o_ref[...] = jnp.zeros_like(o_ref)


def _expert_mlp(block_e, n_used, xs, wg, wu, wd, layer, nb):
    last = lambda i, nu: jnp.minimum(i, nu[0] - 1)
    wspec = lambda shape: pl.BlockSpec(shape, lambda i, be, nu: (layer, be[last(i, nu)], 0, 0))
    grid_spec = pltpu.PrefetchScalarGridSpec(
        num_scalar_prefetch=2,
        grid=(nb,),
        in_specs=[
            pl.BlockSpec((BLOCK_TILE_ROWS, LANES), lambda i, be, nu: (last(i, nu), 0)),
            wspec((1, 1, D_MODEL, D_EXPERT)), wspec((1, 1, D_MODEL, D_EXPERT)), wspec((1, 1, D_EXPERT, D_MODEL)),
        ],
        out_specs=pl.BlockSpec((BLOCK_TILE_ROWS, LANES), lambda i, be, nu: (i, 0)),
        scratch_shapes=[pltpu.VMEM((D_MODEL, D_EXPERT), bf16), pltpu.VMEM((D_MODEL, D_EXPERT), bf16),
                        pltpu.VMEM((D_EXPERT, D_MODEL), bf16)],
    )
    return pl.pallas_call(
        _expert_kernel,
        grid_spec=grid_spec,
        out_shape=jax.ShapeDtypeStruct((nb * BLOCK_TILE_ROWS, LANES), f32),
        compiler_params=_cparams(1),
        name="expert_mlp",
    )(block_e, n_used, xs, wg, wu, wd)


COMB_TM = 64


def _combine_kernel(idx_ref, idxn_ref, wrow_ref, h_ref, hb_ref, sg_ref, su_ref, sd_ref, g_ref, b_ref, ys_hbm,
                    hf_ref, hb_out_ref, hr_ref, buf, comb, sem):
    i = pl.program_id(0)
    nb = pl.num_programs(0)
    slot = i % 2
    cnt = COMB_TM * TOP_K

    @pl.when(i == 0)
    def _():
        _start_gather(ys_hbm, idx_ref, buf, sem, 0, cnt)

    @pl.when(i + 1 < nb)
    def _():
        _start_gather(ys_hbm, idxn_ref, buf, sem, 1 - slot, cnt)

    pltpu.make_async_copy(ys_hbm.at[pl.ds(0, cnt * ROW_TILES)], buf.at[slot], sem.at[slot]).wait()
    rows = COMB_TM * ROW_TILES
    w_rows = wrow_ref[...]
    acc = None
    for k in range(TOP_K):
        term = buf[slot, pl.ds(k * rows, rows), :] * w_rows[:, k:k + 1]
        acc = term if acc is None else acc + term
    comb[...] = acc
    routed = _load_rows(comb, COMB_TM)
    hb = hb_ref[...]
    act = _silu(_dot(hb, sg_ref[...])) * _dot(hb, su_ref[...])
    shared = _dot(act.astype(bf16), sd_ref[...])
    hn = _layer_norm(ALPHA * h_ref[...] + (routed + shared), g_ref[...], b_ref[...])
    hf_ref[...] = hn
    hb_out_ref[...] = hn.astype(bf16)
    _store_rows(hr_ref, hn, COMB_TM)


def _combine(pos3, w_rows, h, h_bf, ws_gate, ws_up, ws_down, g, b, ys2d):
    n = h.shape[0]
    tm = COMB_TM
    nb = n // tm
    cnt = tm * TOP_K
    rowspec = pl.BlockSpec((tm, D_MODEL), lambda i: (i, 0))
    const2 = lambda i: (0, 0)
    return pl.pallas_call(
        _combine_kernel,
        grid=(nb,),
        in_specs=[
            pl.BlockSpec((1, 1, cnt), lambda i: (i, 0, 0), memory_space=pltpu.SMEM),
            pl.BlockSpec((1, 1, cnt), lambda i: (jnp.minimum(i + 1, nb - 1), 0, 0), memory_space=pltpu.SMEM),
            pl.BlockSpec((tm * ROW_TILES, TOP_K), lambda i: (i, 0)),
            rowspec, rowspec,
            pl.BlockSpec((D_MODEL, D_SHARED), const2),
            pl.BlockSpec((D_MODEL, D_SHARED), const2),
            pl.BlockSpec((D_SHARED, D_MODEL), const2),
            pl.BlockSpec((1, D_MODEL), const2),
            pl.BlockSpec((1, D_MODEL), const2),
            pl.BlockSpec(memory_space=pl.ANY),
        ],
        out_specs=[rowspec, rowspec, pl.BlockSpec((tm * ROW_TILES, LANES), lambda i: (i, 0))],
        out_shape=[jax.ShapeDtypeStruct((n, D_MODEL), f32),
                   jax.ShapeDtypeStruct((n, D_MODEL), bf16),
                   jax.ShapeDtypeStruct((n * ROW_TILES, LANES), f32)],
        scratch_shapes=[pltpu.VMEM((2, cnt * ROW_TILES, LANES), f32),
                        pltpu.VMEM((tm * ROW_TILES, LANES), f32),
                        pltpu.SemaphoreType.DMA((2,))],
        compiler_params=_cparams(1),
        name="moe_combine",
    )(pos3, pos3, w_rows, h, h_bf, ws_gate, ws_up, ws_down, g, b, ys2d)


def _dispatch(eidx, rank, counts_f, nb):
    n = eidx.shape[0]
    counts = counts_f[0, :N_EXPERTS].astype(jnp.int32)
    padded = (counts + EXPERT_ROWS - 1) // EXPERT_ROWS * EXPERT_ROWS
    pend = jnp.cumsum(padded)
    pstart = pend - padded
    starts = jnp.arange(nb, dtype=jnp.int32) * EXPERT_ROWS
    block_e = jnp.minimum(jnp.sum((pend[None, :] <= starts[:, None]).astype(jnp.int32), axis=1), N_EXPERTS - 1)
    n_used = (pend[-1] // EXPERT_ROWS).astype(jnp.int32).reshape(1)
    onehot = eidx[:, :, None] == jnp.arange(N_EXPERTS, dtype=jnp.int32)[None, None, :]
    pos = jnp.sum(jnp.where(onehot, pstart[None, None, :], 0), axis=-1) + rank
    pos_tok3 = pos.reshape(n // SCATTER_TM, 1, SCATTER_TM * TOP_K)
    tm = COMB_TM
    pos_k3 = pos.reshape(n // tm, tm, TOP_K).transpose(0, 2, 1).reshape(n // tm, 1, tm * TOP_K)
    return block_e.astype(jnp.int32), n_used, padded, pend, pos_tok3, pos_k3


def _consts():
    r = jnp.arange(CHUNK)
    tri = (r[:, None] >= r[None, :]).astype(bf16)
    trit = (r[:, None] <= r[None, :]).astype(bf16)
    lane = jnp.arange(LANES)
    ch = jnp.arange(SSD_HEADS * SSD_HEADDIM)
    ssd_expand = (lane[:, None] == (ch[None, :] // SSD_HEADDIM)).astype(bf16)
    hc = jnp.arange(HGRN_HEADS * HGRN_DIM)
    hm_rows = jnp.arange(HGRN_HEADS * SUB)
    hmask = ((hm_rows[:, None] // SUB) == (hc[None, :] // HGRN_DIM)).astype(f32)
    qr = jnp.arange((HGRN_NBLK - 1) * HGRN_HEADS * SUB)
    qblk = qr // (HGRN_HEADS * SUB) + 1
    kstart = SUB * jnp.arange(HGRN_NBLK - 1) * jnp.arange(1, HGRN_NBLK) // 2
    kblk = jnp.sum(lane[None, :] >= kstart[:, None], axis=0)
    offmask = ((qblk[:, None] == kblk[None, :]) & (lane[None, :] < HGRN_OFF_KEYS)).astype(f32)
    rt = jnp.arange(ROUTER_TM)
    ltri = (rt[:, None] > rt[None, :]).astype(bf16)
    return tri, trit, ssd_expand, hmask, offmask, ltri


def _hgrn_lower_bounds(lb_param):
    sm = jax.nn.softmax(lb_param.astype(f32), axis=0)
    return jnp.cumsum(sm, axis=0) - sm[0:1]


def kernel(x, mem, w_in, conv_w, conv_b, dt_bias, a_log, d_skip, ssd_norm, pool_w, pool_scale, hgrn_lb, hgrn_norm, w_mem_kv, w_branch, w_o, ln1_g, ln1_b, w_router, router_bias, w_exp_gate, w_exp_up, w_exp_down, w_sh_gate, w_sh_up, w_sh_down, ln2_g, ln2_b):
    bsz, seq, d = x.shape
    n = bsz * seq
    nk = n * TOP_K
    nb = -(-nk // EXPERT_ROWS) + N_EXPERTS
    tri, trit, ssd_expand, hmask, offmask, ltri = _consts()
    lb_all = _hgrn_lower_bounds(hgrn_lb)

    c_dt0 = BRANCH_WIDTH * 2 + SSD_CONV_DIM
    c_dt1 = c_dt0 + SSD_HEADS
    c_gate = c_dt1 + 5 * BRANCH_WIDTH
    pad_lanes = LANES - SSD_HEADS

    h = x.reshape(n, d)
    h_bf = h.astype(bf16)
    h_rows = None
    mem_bf = mem.reshape(bsz * MEM_TOKENS, d).astype(bf16)

    for l in range(DEPTH):
        w_main = jnp.concatenate([w_in[l, :, :c_dt0], w_in[l, :, c_dt1:c_gate]], axis=1).astype(bf16)
        w_dt = jnp.pad(w_in[l, :, c_dt0:c_dt1], ((0, 0), (0, pad_lanes))).astype(bf16)
        w_gate = w_in[l, :, c_gate:].reshape(d, N_BRANCH, d).transpose(1, 0, 2).astype(bf16)

        proj = _matmul(h_bf, w_main, f32, 512, 1024)
        dtp = _matmul(h_bf, w_dt, f32, 512, LANES)
        kv = _matmul(mem_bf, w_mem_kv[l].astype(bf16), bf16, 512, 1024)

        br_pool = _pool_mixer(proj, pool_w[l].astype(bf16), pool_scale[l].reshape(1, -1), bsz, seq)
        br_ssd = _ssd_mixer(
            proj, dtp, conv_w[l], conv_b[l].reshape(1, -1),
            jnp.pad(dt_bias[l], (0, pad_lanes)).reshape(1, -1),
            jnp.pad(a_log[l], (0, pad_lanes)).reshape(1, -1),
            jnp.repeat(d_skip[l], SSD_HEADDIM).reshape(1, -1),
            ssd_norm[l].reshape(1, -1), (tri, trit, ssd_expand), bsz, seq)
        br_hgrn = _hgrn_mixer(proj, lb_all[l].reshape(1, -1), hgrn_norm[l].reshape(1, -1),
                              (tri, hmask, offmask), bsz, seq)
        br_mem = _mem_attention(proj, kv, bsz, seq)

        merged = _merge(h_bf, (br_pool, br_ssd, br_hgrn, br_mem), w_gate, w_branch[l].astype(bf16))
        h, h_bf, h_rows = _wo_ln(merged, w_o[l].astype(bf16), h, ln1_g[l].reshape(1, -1), ln1_b[l].reshape(1, -1))

        eidx, wts, rank, counts = _router(h, jnp.pad(w_router[l], ((0, 0), (0, LANES - N_EXPERTS))),
                                          jnp.pad(router_bias[l], (0, LANES - N_EXPERTS)).reshape(1, -1), ltri)
        block_e, n_used, padded, pend, pos_tok3, pos_k3 = _dispatch(eidx, rank, counts, nb)
        xs = _scatter_rows(padded, pend, pos_tok3, h_rows, nb)
        ys = _expert_mlp(block_e, n_used, xs, w_exp_gate, w_exp_up, w_exp_down, l, nb)
        w_rows = jnp.repeat(wts, ROW_TILES, axis=0)
        h, h_bf, h_rows = _combine(pos_k3, w_rows, h, h_bf, w_sh_gate[l].astype(bf16), w_sh_up[l].astype(bf16),
                                   w_sh_down[l].astype(bf16), ln2_g[l].reshape(1, -1), ln2_b[l].reshape(1, -1), ys)
    return h.reshape(bsz, seq, d)
```

```python
import functools

import jax
import jax.numpy as jnp
from jax import lax
from jax.experimental import pallas as pl
from jax.experimental.pallas import tpu as pltpu

f32 = jnp.float32
bf16 = jnp.bfloat16

D_MODEL = 2048
DEPTH = 4
CHUNK = 64
N_BRANCH = 4
BRANCH_WIDTH = 1024
POOL_WINDOWS = (2, 4, 8, 16)
POOL_GROUP = 256
SSD_HEADS = 16
SSD_HEADDIM = 64
SSD_STATE = 128
SSD_GROUPS = 4
SSD_CONV = 4
SSD_CONV_DIM = 2048
HGRN_HEADS = 8
HGRN_DIM = 128
MEM_TOKENS = 256
MEM_HEADS = 4
MEM_HEAD_DIM = 256
N_EXPERTS = 64
TOP_K = 8
N_GROUPS = 8
TOPK_GROUPS = 4
D_EXPERT = 256
D_SHARED = 256
ROUTED_SCALE = 2.5
MASK_SCORE = -1e4
ALPHA = (2 * DEPTH) ** 0.25
EPS = 1e-5

LANES = 128
ROW_TILES = 1
EXPERT_ROWS = 128
SUB = 16
VMEM_LIMIT = 56 * 1024 * 1024

NEG_INF = float("-inf")


def _cparams(n_axes):
    return pltpu.CompilerParams(dimension_semantics=("arbitrary",) * n_axes,
                                vmem_limit_bytes=VMEM_LIMIT)


def _sigmoid(x):
    return jax.nn.sigmoid(x)


def _silu(x):
    return x * jax.nn.sigmoid(x)


def _split3(x):
    hi = x.astype(bf16)
    r1 = x - hi.astype(f32)
    mid = r1.astype(bf16)
    lo = (r1 - mid.astype(f32)).astype(bf16)
    return hi, mid, lo


def _dot(a, b):
    return jnp.dot(a, b, preferred_element_type=f32)


def _dot_nt(a, b):
    return lax.dot_general(a, b, (((1,), (1,)), ((), ())), preferred_element_type=f32)


def _dot_tn(a, b):
    return lax.dot_general(a, b, (((0,), (0,)), ((), ())), preferred_element_type=f32)


def _sel_left(sel01, x):
    return sum(_dot(sel01, p) for p in _split3(x))


def _sel_right(x, sel01):
    return sum(_dot(p, sel01) for p in _split3(x))


def _mm_kernel(x_ref, w_ref, o_ref):
    o_ref[...] = _dot(x_ref[...], w_ref[...]).astype(o_ref.dtype)


def _matmul(x, w, out_dtype, tm, tn, col0=0, ncols=None):
    m, k = x.shape
    n = w.shape[1] if ncols is None else ncols
    cb = col0 // tn
    return pl.pallas_call(
        _mm_kernel,
        grid=(n // tn, m // tm),
        in_specs=[pl.BlockSpec((tm, k), lambda j, i: (i, 0)),
                  pl.BlockSpec((k, tn), lambda j, i: (0, cb + j))],
        out_specs=pl.BlockSpec((tm, tn), lambda j, i: (i, j)),
        out_shape=jax.ShapeDtypeStruct((m, n), out_dtype),
        compiler_params=_cparams(2),
        name="matmul",
    )(x, w)


POOL_T = 512
POOL_HALO = 16


def _pool_kernel(u_ref, halo_ref, w_ref, scale_ref, o_ref):
    i = pl.program_id(1)
    u = u_ref[...]
    halo = jnp.where(i > 0, halo_ref[...], 0.0)
    ext = jnp.concatenate([halo, u], axis=0)
    t = i * POOL_T + lax.broadcasted_iota(jnp.int32, (POOL_T, 1), 0)
    outs = []
    for gi, w in enumerate(POOL_WINDOWS):
        xg = ext[:, gi * POOL_GROUP:(gi + 1) * POOL_GROUP]
        s = xg
        span = 1
        while span < w:
            s = s[span:, :] + s[:-span, :]
            span *= 2
        s = s[s.shape[0] - POOL_T:, :]
        cnt = jnp.minimum(t + 1, w).astype(f32)
        d = s / cnt - u[:, gi * POOL_GROUP:(gi + 1) * POOL_GROUP]
        outs.append(_dot(d.astype(bf16), w_ref[gi]))
    y = jnp.concatenate(outs, axis=-1) * scale_ref[...]
    o_ref[...] = y.astype(o_ref.dtype)


def _pool_mixer(proj, pool_w, pool_scale, bsz, seq):
    n = bsz * seq
    nt = seq // POOL_T
    return pl.pallas_call(
        _pool_kernel,
        grid=(bsz, nt),
        in_specs=[
            pl.BlockSpec((POOL_T, BRANCH_WIDTH), lambda b, i: (b * nt + i, 0)),
            pl.BlockSpec((POOL_HALO, BRANCH_WIDTH),
                         lambda b, i: (jnp.maximum((b * nt + i) * (POOL_T // POOL_HALO) - 1, 0), 0)),
            pl.BlockSpec((len(POOL_WINDOWS), POOL_GROUP, POOL_GROUP), lambda b, i: (0, 0, 0)),
            pl.BlockSpec((1, BRANCH_WIDTH), lambda b, i: (0, 0)),
        ],
        out_specs=pl.BlockSpec((POOL_T, BRANCH_WIDTH), lambda b, i: (b * nt + i, 0)),
        out_shape=jax.ShapeDtypeStruct((n, BRANCH_WIDTH), bf16),
        compiler_params=_cparams(2),
        name="pool_mixer",
    )(proj, proj, pool_w, pool_scale)


CONV_TAIL = 8


def _softplus(x):
    return jnp.maximum(x, 0.0) + jnp.log1p(jnp.exp(-jnp.abs(x)))


def _ssd_kernel(z_ref, xbc_ref, dt_ref, cw_ref, cb_ref, dtb_ref, alog_ref, dskip_ref, nw_ref,
                tri_ref, trit_ref, exp_ref, o_ref, state_ref, tail_ref):
    i = pl.program_id(1)

    @pl.when(i == 0)
    def _():
        state_ref[...] = jnp.zeros_like(state_ref)
        tail_ref[...] = jnp.zeros_like(tail_ref)

    xbc = xbc_ref[...]
    ext = jnp.concatenate([tail_ref[...], xbc], axis=0)
    cw = cw_ref[...]
    conv = cb_ref[...]
    for k in range(SSD_CONV):
        off = CONV_TAIL - (SSD_CONV - 1) + k
        conv = conv + cw[k:k + 1, :] * ext[off:off + CHUNK, :]
    tail_ref[...] = xbc[CHUNK - CONV_TAIL:, :]
    act = _silu(conv)
    inner = SSD_HEADS * SSD_HEADDIM
    gw = SSD_GROUPS * SSD_STATE
    xs = act[:, :inner]
    bmat = act[:, inner:inner + gw]
    cmat = act[:, inner + gw:]

    dt = _softplus(dt_ref[...] + dtb_ref[...])
    a = -jnp.exp(alog_ref[...])
    ad = dt * a
    ad_parts = _split3(ad)
    tri = tri_ref[...]
    acs = sum(_dot(tri, p) for p in ad_parts)
    acs_t = sum(_dot_tn(p, trit_ref[...]) for p in ad_parts)
    a_last = acs[CHUNK - 1:CHUNK, :]
    expand = exp_ref[...]
    dt_e = _sel_right(dt, expand)
    eacs_e = _sel_right(jnp.exp(acs), expand)
    dec_e = _sel_right(jnp.exp(a_last - acs), expand)
    x = xs * dt_e
    xb = x.astype(bf16)
    xd = (x * dec_e).astype(bf16)
    elast_e = eacs_e[CHUNK - 1:CHUNK, :]

    row = lax.broadcasted_iota(jnp.int32, (CHUNK, CHUNK), 0)
    col = lax.broadcasted_iota(jnp.int32, (CHUNK, CHUNK), 1)
    causal = row >= col
    hpg = SSD_HEADS // SSD_GROUPS
    gch = hpg * SSD_HEADDIM
    y_diag = []
    y_off = []
    for g in range(SSD_GROUPS):
        bg = bmat[:, g * SSD_STATE:(g + 1) * SSD_STATE].astype(bf16)
        cg = cmat[:, g * SSD_STATE:(g + 1) * SSD_STATE].astype(bf16)
        cb = _dot_nt(cg, bg)
        for hh in range(hpg):
            h = g * hpg + hh
            diff = acs[:, h:h + 1] - acs_t[h:h + 1, :]
            decay = jnp.where(causal, jnp.exp(jnp.where(causal, diff, 0.0)), 0.0)
            sc = (cb * decay).astype(bf16)
            y_diag.append(_dot(sc, xb[:, h * SSD_HEADDIM:(h + 1) * SSD_HEADDIM]))
        st = state_ref[:, g * gch:(g + 1) * gch]
        y_off.append(_dot(cg, st.astype(bf16)))
        upd = _dot_tn(bg, xd[:, g * gch:(g + 1) * gch])
        state_ref[:, g * gch:(g + 1) * gch] = st * elast_e[:, g * gch:(g + 1) * gch] + upd
    y = (jnp.concatenate(y_diag, axis=-1) + jnp.concatenate(y_off, axis=-1) * eacs_e
         + xs * dskip_ref[...])
    y = y * _silu(z_ref[...])
    nw = nw_ref[...]
    ngw = inner // SSD_GROUPS
    outs = []
    for g in range(SSD_GROUPS):
        yg = y[:, g * ngw:(g + 1) * ngw]
        ms = jnp.mean(yg * yg, axis=-1, keepdims=True)
        outs.append(yg * lax.rsqrt(ms + EPS) * nw[:, g * ngw:(g + 1) * ngw])
    o_ref[...] = jnp.concatenate(outs, axis=-1).astype(o_ref.dtype)


def _ssd_mixer(proj, dtp, conv_w, conv_b, dt_bias, a_log, d_skip_e, norm_w, consts, bsz, seq):
    n = bsz * seq
    nc = seq // CHUNK
    tri, trit, expand = consts
    rowmap = lambda b, i: (b * nc + i, 0)
    const2 = lambda b, i: (0, 0)
    inner = SSD_HEADS * SSD_HEADDIM
    return pl.pallas_call(
        _ssd_kernel,
        grid=(bsz, nc),
        in_specs=[
            pl.BlockSpec((CHUNK, inner), lambda b, i: (b * nc + i, 1)),
            pl.BlockSpec((CHUNK, SSD_CONV_DIM), lambda b, i: (b * nc + i, 1)),
            pl.BlockSpec((CHUNK, LANES), rowmap),
            pl.BlockSpec((SSD_CONV, SSD_CONV_DIM), const2),
            pl.BlockSpec((1, SSD_CONV_DIM), const2),
            pl.BlockSpec((1, LANES), const2),
            pl.BlockSpec((1, LANES), const2),
            pl.BlockSpec((1, inner), const2),
            pl.BlockSpec((1, inner), const2),
            pl.BlockSpec((CHUNK, CHUNK), const2),
            pl.BlockSpec((CHUNK, CHUNK), const2),
            pl.BlockSpec((LANES, inner), const2),
        ],
        out_specs=pl.BlockSpec((CHUNK, inner), rowmap),
        out_shape=jax.ShapeDtypeStruct((n, inner), bf16),
        scratch_shapes=[pltpu.VMEM((SSD_STATE, inner), f32),
                        pltpu.VMEM((CONV_TAIL, SSD_CONV_DIM), f32)],
        compiler_params=_cparams(2),
        name="ssd_mixer",
    )(proj, proj, dtp, conv_w, conv_b, dt_bias, a_log, d_skip_e, norm_w, tri, trit, expand)


LOG2E = 1.4426950408889634
HGRN_NBLK = CHUNK // SUB
HGRN_OFF_KEYS = SUB * (HGRN_NBLK - 1) * HGRN_NBLK // 2


def _hgrn_chunk(q, f, v, g, lb, nw, tri, hmask, offmask, state_ref):
    width = HGRN_HEADS * HGRN_DIM
    qf = _silu(q)
    kf = (1.0 - lb) * _sigmoid(-f)
    b2 = _sel_left(tri, jnp.log1p(-kf)) * LOG2E
    c2 = b2 - jnp.log2(kf)
    b_last = b2[CHUNK - 1:CHUNK, :]
    vb = v.astype(bf16)

    qd = (qf * jnp.exp2(b2)).astype(bf16)
    kd = jnp.exp2(b_last - c2).astype(bf16)
    eb_last = jnp.exp2(b_last)
    o_inter = []
    for h in range(HGRN_HEADS):
        sl = slice(h * HGRN_DIM, (h + 1) * HGRN_DIM)
        st = state_ref[h]
        o_inter.append(_dot_nt(qd[:, sl], st.astype(bf16)))
        state_ref[h] = st * eb_last[:, sl] + _dot_tn(vb[:, sl], kd[:, sl])
    o = jnp.concatenate(o_inter, axis=-1)

    qm, kk, vv = [], [], []
    for blk in range(1, HGRN_NBLK):
        r0 = blk * SUB
        anchor = b2[r0 - 1:r0, :]
        qq = qf[r0:r0 + SUB, :] * jnp.exp2(b2[r0:r0 + SUB, :] - anchor)
        qm.append((jnp.concatenate([qq] * HGRN_HEADS, axis=0) * hmask).astype(bf16))
        kk.append(jnp.exp2(anchor - c2[:r0, :]).astype(bf16))
        vv.append(vb[:r0, :])
    zpad = jnp.zeros((LANES - HGRN_OFF_KEYS, width), bf16)
    att = _dot_nt(jnp.concatenate(qm, axis=0), jnp.concatenate(kk + [zpad], axis=0))
    res = _dot((att * offmask).astype(bf16), jnp.concatenate(vv + [zpad], axis=0))
    rows = [jnp.zeros((SUB, width), f32)]
    for blk in range(1, HGRN_NBLK):
        base = (blk - 1) * HGRN_HEADS * SUB
        acc = None
        for h in range(HGRN_HEADS):
            part = res[base + h * SUB:base + (h + 1) * SUB, :] * hmask[h * SUB:(h + 1) * SUB, :]
            acc = part if acc is None else acc + part
        rows.append(acc)
    o = o + jnp.concatenate(rows, axis=0)

    tpos = lax.broadcasted_iota(jnp.int32, (CHUNK, 1), 0) & (SUB - 1)
    diag = [None] * HGRN_HEADS
    for d in range(SUB):
        cs = c2 if d == 0 else pltpu.roll(c2, d, 0)
        vs = v if d == 0 else pltpu.roll(v, d, 0)
        p = qf * jnp.exp2(b2 - cs)
        same_block = tpos >= d
        for h in range(HGRN_HEADS):
            sl = slice(h * HGRN_DIM, (h + 1) * HGRN_DIM)
            att = jnp.where(same_block, jnp.sum(p[:, sl], axis=-1, keepdims=True), 0.0)
            term = att * vs[:, sl]
            diag[h] = term if diag[h] is None else diag[h] + term
    o = o + jnp.concatenate(diag, axis=-1)

    gate = _silu(g)
    outs = []
    for h in range(HGRN_HEADS):
        sl = slice(h * HGRN_DIM, (h + 1) * HGRN_DIM)
        oh = o[:, sl]
        ms = jnp.mean(oh * oh, axis=-1, keepdims=True)
        outs.append(oh * lax.rsqrt(ms + EPS) * nw * gate[:, sl])
    return jnp.concatenate(outs, axis=-1)


def _hgrn_kernel(q_ref, f_ref, i_ref, g_ref, lb_ref, nw_ref, tri_ref, hmask_ref, offmask_ref, o_ref, state_ref):
    c = pl.program_id(0)

    @pl.when(c == 0)
    def _():
        state_ref[...] = jnp.zeros_like(state_ref)

    for bi in range(q_ref.shape[0]):
        out = _hgrn_chunk(q_ref[bi], f_ref[bi], i_ref[bi], g_ref[bi], lb_ref[...], nw_ref[...], tri_ref[...],
                          hmask_ref[...], offmask_ref[...], state_ref.at[bi])
        o_ref[bi] = out.astype(o_ref.dtype)


def _hgrn_mixer(proj, lb, norm_w, consts, bsz, seq):
    n = bsz * seq
    nc = seq // CHUNK
    tri, hmask, offmask = consts
    width = HGRN_HEADS * HGRN_DIM
    const2 = lambda i: (0, 0)
    col = lambda j: (lambda i: (0, i, j))
    proj3 = proj.reshape(bsz, seq, proj.shape[1])
    qrows = (HGRN_NBLK - 1) * HGRN_HEADS * SUB
    out = pl.pallas_call(
        _hgrn_kernel,
        grid=(nc,),
        in_specs=[
            pl.BlockSpec((bsz, CHUNK, width), col(4)),
            pl.BlockSpec((bsz, CHUNK, width), col(5)),
            pl.BlockSpec((bsz, CHUNK, width), col(6)),
            pl.BlockSpec((bsz, CHUNK, width), col(7)),
            pl.BlockSpec((1, width), const2),
            pl.BlockSpec((1, HGRN_DIM), const2),
            pl.BlockSpec((CHUNK, CHUNK), const2),
            pl.BlockSpec((HGRN_HEADS * SUB, width), const2),
            pl.BlockSpec((qrows, LANES), const2),
        ],
        out_specs=pl.BlockSpec((bsz, CHUNK, width), lambda i: (0, i, 0)),
        out_shape=jax.ShapeDtypeStruct((bsz, seq, width), bf16),
        scratch_shapes=[pltpu.VMEM((bsz, HGRN_HEADS, HGRN_DIM, HGRN_DIM), f32)],
        compiler_params=_cparams(1),
        name="hgrn_mixer",
    )(proj3, proj3, proj3, proj3, lb, norm_w, tri, hmask, offmask)
    return out.reshape(n, width)


MEM_T = 512


def _memattn_kernel(q_ref, k_ref, v_ref, o_ref):
    q = q_ref[...].astype(bf16)
    outs = []
    for h in range(MEM_HEADS):
        sl = slice(h * MEM_HEAD_DIM, (h + 1) * MEM_HEAD_DIM)
        s = _dot_nt(q[:, sl], k_ref[:, sl]) * (MEM_HEAD_DIM ** -0.5)
        m = jnp.max(s, axis=-1, keepdims=True)
        e = jnp.exp(s - m)
        p = e / jnp.sum(e, axis=-1, keepdims=True)
        outs.append(_dot(p.astype(bf16), v_ref[:, sl]))
    o_ref[...] = jnp.concatenate(outs, axis=-1).astype(o_ref.dtype)


def _mem_attention(proj, kv, bsz, seq):
    n = bsz * seq
    nt = seq // MEM_T
    width = MEM_HEADS * MEM_HEAD_DIM
    return pl.pallas_call(
        _memattn_kernel,
        grid=(bsz, nt),
        in_specs=[
            pl.BlockSpec((MEM_T, width), lambda b, i: (b * nt + i, 8)),
            pl.BlockSpec((MEM_TOKENS, width), lambda b, i: (b, 0)),
            pl.BlockSpec((MEM_TOKENS, width), lambda b, i: (b, 1)),
        ],
        out_specs=pl.BlockSpec((MEM_T, width), lambda b, i: (b * nt + i, 0)),
        out_shape=jax.ShapeDtypeStruct((n, width), bf16),
        compiler_params=_cparams(2),
        name="mem_attention",
    )(proj, kv, kv)


MERGE_TM = 512
MERGE_TN = 512


def _merge_kernel(h_ref, b0_ref, b1_ref, b2_ref, b3_ref, g0_ref, g1_ref, g2_ref, g3_ref, wb_ref, o_ref):
    h = h_ref[...]
    acc = None
    for n, (br, wg) in enumerate(zip((b0_ref, b1_ref, b2_ref, b3_ref), (g0_ref, g1_ref, g2_ref, g3_ref))):
        gate = _sigmoid(_dot(h, wg[...]))
        term = gate * _dot(br[...], wb_ref[n])
        acc = term if acc is None else acc + term
    o_ref[...] = acc.astype(o_ref.dtype)


def _merge(h_bf, branches, w_packed, gate_col0, w_branch):
    n = h_bf.shape[0]
    tm, tn = MERGE_TM, MERGE_TN
    bspec = pl.BlockSpec((tm, BRANCH_WIDTH), lambda j, i: (i, 0))
    gspec = lambda nbr: pl.BlockSpec((D_MODEL, tn), lambda j, i: (0, (gate_col0 + nbr * D_MODEL) // tn + j))
    return pl.pallas_call(
        _merge_kernel,
        grid=(D_MODEL // tn, n // tm),
        in_specs=[pl.BlockSpec((tm, D_MODEL), lambda j, i: (i, 0)), bspec, bspec, bspec, bspec,
                  gspec(0), gspec(1), gspec(2), gspec(3),
                  pl.BlockSpec((N_BRANCH, BRANCH_WIDTH, tn), lambda j, i: (0, 0, j))],
        out_specs=pl.BlockSpec((tm, tn), lambda j, i: (i, j)),
        out_shape=jax.ShapeDtypeStruct((n, D_MODEL), bf16),
        compiler_params=_cparams(2),
        name="gated_merge",
    )(h_bf, *branches, w_packed, w_packed, w_packed, w_packed, w_branch)


def _layer_norm(x, g, b):
    mu = jnp.mean(x, axis=-1, keepdims=True)
    xc = x - mu
    var = jnp.mean(xc * xc, axis=-1, keepdims=True)
    return xc * lax.rsqrt(var + EPS) * g + b


WO_TM = 256


def _wo_ln_kernel(m_ref, w_ref, h_ref, g_ref, b_ref, hf_ref, hb_ref):
    y = ALPHA * h_ref[...] + _dot(m_ref[...], w_ref[...])
    hn = _layer_norm(y, g_ref[...], b_ref[...])
    hf_ref[...] = hn
    hb_ref[...] = hn.astype(bf16)


def _wo_ln(merged, w_o, h, g, b):
    n = h.shape[0]
    tm = WO_TM
    rowspec = pl.BlockSpec((tm, D_MODEL), lambda i: (i, 0))
    const2 = lambda i: (0, 0)
    return pl.pallas_call(
        _wo_ln_kernel,
        grid=(n // tm,),
        in_specs=[rowspec, pl.BlockSpec((D_MODEL, D_MODEL), const2), rowspec,
                  pl.BlockSpec((1, D_MODEL), const2), pl.BlockSpec((1, D_MODEL), const2)],
        out_specs=[rowspec, rowspec],
        out_shape=[jax.ShapeDtypeStruct((n, D_MODEL), f32),
                   jax.ShapeDtypeStruct((n, D_MODEL), bf16)],
        compiler_params=_cparams(1),
        name="wo_layernorm",
    )(merged, w_o, h, g, b)


ROUTER_TM = 256


def _router_kernel(h_ref, w_ref, bias_ref, ltri_ref, eidx_ref, wts_ref, rank_ref, counts_ref, cnt_ref):
    step = pl.program_id(0)

    @pl.when(step == 0)
    def _():
        cnt_ref[...] = jnp.zeros_like(cnt_ref)

    h = h_ref[...]
    w = w_ref[...]
    h_hi = h.astype(bf16)
    h_lo = (h - h_hi.astype(f32)).astype(bf16)
    w_hi = w.astype(bf16)
    w_lo = (w - w_hi.astype(f32)).astype(bf16)
    logits = _dot(h_hi, w_hi) + _dot(h_hi, w_lo) + _dot(h_lo, w_hi)
    scores = _sigmoid(logits)
    lane_i = lax.broadcasted_iota(jnp.int32, logits.shape, 1)
    lane = lane_i.astype(f32)
    real = lane_i < N_EXPERTS
    choice = jnp.where(real, scores + bias_ref[...], NEG_INF)
    gsize = N_EXPERTS // N_GROUPS
    grp = lax.shift_right_logical(lane_i, gsize.bit_length() - 1)
    big = float(LANES)

    def first_argmax(x):
        m = jnp.max(x, axis=-1, keepdims=True)
        idx = jnp.min(jnp.where(x == m, lane, big), axis=-1, keepdims=True)
        return m, idx

    gscore = []
    for g in range(N_GROUPS):
        cg = jnp.where(grp == g, choice, NEG_INF)
        m1, i1 = first_argmax(cg)
        m2 = jnp.max(jnp.where(lane == i1, NEG_INF, cg), axis=-1, keepdims=True)
        gscore.append(m1 + m2)
    keep = jnp.zeros(logits.shape, f32)
    for g in range(N_GROUPS):
        rank = jnp.zeros_like(gscore[g])
        for o in range(N_GROUPS):
            if o == g:
                continue
            ahead = (gscore[o] > gscore[g]) if o > g else (gscore[o] >= gscore[g])
            rank = rank + jnp.where(ahead, 1.0, 0.0)
        keep = jnp.where((grp == g) & (rank < TOPK_GROUPS), 1.0, keep)
    cur = jnp.where(real, jnp.where(keep > 0.0, choice, MASK_SCORE), NEG_INF)
    idxs, ws, hits = [], [], []
    sel = jnp.zeros(logits.shape, f32)
    for _ in range(TOP_K):
        _, ik = first_argmax(cur)
        hit = lane == ik
        ws.append(jnp.sum(jnp.where(hit, scores, 0.0), axis=-1, keepdims=True))
        idxs.append(ik)
        hits.append(hit)
        sel = jnp.where(hit, 1.0, sel)
        cur = jnp.where(hit, NEG_INF, cur)
    wsum = ws[0]
    for wk in ws[1:]:
        wsum = wsum + wk
    eidx_ref[...] = jnp.concatenate(idxs, axis=-1).astype(jnp.int32)
    wts_ref[...] = jnp.concatenate([wk / wsum * ROUTED_SCALE for wk in ws], axis=-1)
    before = _dot(ltri_ref[...], sel.astype(bf16)) + cnt_ref[...]
    ranks = [jnp.sum(jnp.where(hit, before, 0.0), axis=-1, keepdims=True) for hit in hits]
    rank_ref[...] = jnp.concatenate(ranks, axis=-1).astype(jnp.int32)
    cnt_ref[...] = cnt_ref[...] + jnp.sum(sel, axis=0, keepdims=True)
    counts_ref[...] = cnt_ref[...]


def _router(h, w_router_p, bias_p, ltri):
    n = h.shape[0]
    tm = ROUTER_TM
    const2 = lambda i: (0, 0)
    kspec = pl.BlockSpec((tm, TOP_K), lambda i: (i, 0))
    return pl.pallas_call(
        _router_kernel,
        grid=(n // tm,),
        in_specs=[pl.BlockSpec((tm, D_MODEL), lambda i: (i, 0)),
                  pl.BlockSpec((D_MODEL, LANES), const2),
                  pl.BlockSpec((1, LANES), const2),
                  pl.BlockSpec((tm, tm), const2)],
        out_specs=[kspec, kspec, kspec, pl.BlockSpec((1, LANES), const2)],
        out_shape=[jax.ShapeDtypeStruct((n, TOP_K), jnp.int32),
                   jax.ShapeDtypeStruct((n, TOP_K), f32),
                   jax.ShapeDtypeStruct((n, TOP_K), jnp.int32),
                   jax.ShapeDtypeStruct((1, LANES), f32)],
        scratch_shapes=[pltpu.VMEM((1, LANES), f32)],
        compiler_params=_cparams(1),
        name="router",
    )(h, w_router_p, bias_p, ltri)


GATHER_UNROLL = 8


def _start_gather(src_hbm, idx_ref, buf, sem, slot, count):
    def body(o, carry):
        for u in range(GATHER_UNROLL):
            k = o * GATHER_UNROLL + u
            pltpu.make_async_copy(src_hbm.at[pl.ds(idx_ref[0, 0, k], 1)], buf.at[slot, pl.ds(k, 1)],
                                  sem.at[slot]).start()
        return carry
    lax.fori_loop(0, count // GATHER_UNROLL, body, 0)


SCATTER_TM = 128
SCATTER_SLOTS = 3
BLOCK_TILE_ROWS = EXPERT_ROWS * ROW_TILES


def _scatter_kernel(padded_ref, pend_ref, pos_ref, hrows_hbm, xs_hbm, stage, zbuf, lsem, ssem, zsem):
    i = pl.program_id(0)
    nsteps = pl.num_programs(0)
    cnt = SCATTER_TM * TOP_K

    def zero_copy(e):
        dst = pl.multiple_of((pend_ref[e] - EXPERT_ROWS) * ROW_TILES, BLOCK_TILE_ROWS)
        return pltpu.make_async_copy(zbuf, xs_hbm.at[pl.ds(dst, BLOCK_TILE_ROWS)], zsem)

    @pl.when(i == 0)
    def _():
        zbuf[...] = jnp.zeros_like(zbuf)

        def start(e, carry):
            @pl.when(padded_ref[e] > 0)
            def _():
                zero_copy(e).start()
            return carry

        def wait(e, carry):
            @pl.when(padded_ref[e] > 0)
            def _():
                zero_copy(e).wait()
            return carry

        def tail_copy(j):
            dst = pl.multiple_of(j * BLOCK_TILE_ROWS, BLOCK_TILE_ROWS)
            return pltpu.make_async_copy(zbuf, xs_hbm.at[pl.ds(dst, BLOCK_TILE_ROWS)], zsem)

        def tail_start(j, carry):
            tail_copy(j).start()
            return carry

        def tail_wait(j, carry):
            tail_copy(j).wait()
            return carry

        n_used = pend_ref[N_EXPERTS - 1] // EXPERT_ROWS
        n_blocks = xs_hbm.shape[0] // BLOCK_TILE_ROWS
        lax.fori_loop(0, N_EXPERTS, start, 0)
        lax.fori_loop(n_used, n_blocks, tail_start, 0)
        lax.fori_loop(0, N_EXPERTS, wait, 0)
        lax.fori_loop(n_used, n_blocks, tail_wait, 0)

    tile_rows = SCATTER_TM * ROW_TILES
    slot = lax.rem(i, SCATTER_SLOTS)
    nxt = lax.rem(i + 1, SCATTER_SLOTS)

    def load(j, s):
        src = pl.multiple_of(j * tile_rows, tile_rows)
        return pltpu.make_async_copy(hrows_hbm.at[pl.ds(src, tile_rows)], stage.at[s], lsem.at[s])

    def scatter_wait(s):
        pltpu.make_async_copy(xs_hbm.at[pl.ds(0, cnt * ROW_TILES)], xs_hbm.at[pl.ds(0, cnt * ROW_TILES)],
                              ssem.at[s]).wait()

    @pl.when(i == 0)
    def _():
        load(0, 0).start()

    @pl.when(i >= SCATTER_SLOTS - 1)
    def _():
        scatter_wait(nxt)

    @pl.when(i + 1 < nsteps)
    def _():
        load(i + 1, nxt).start()

    load(i, slot).wait()

    def body(t, carry):
        src = pl.multiple_of(t * ROW_TILES, ROW_TILES)
        for k in range(TOP_K):
            dst = pl.multiple_of(pos_ref[0, 0, t * TOP_K + k] * ROW_TILES, ROW_TILES)
            pltpu.make_async_copy(stage.at[slot, pl.ds(src, ROW_TILES)], xs_hbm.at[pl.ds(dst, ROW_TILES)],
                                  ssem.at[slot]).start()
        return carry

    lax.fori_loop(0, SCATTER_TM, body, 0)

    @pl.when(i == nsteps - 1)
    def _():
        for back in range(SCATTER_SLOTS - 2, -1, -1):
            @pl.when(i >= back)
            def _():
                scatter_wait(lax.rem(i - back + SCATTER_SLOTS, SCATTER_SLOTS))


def _scatter_rows(padded, pend, pos_tok3, h_rows2d, nb):
    n = h_rows2d.shape[0] // ROW_TILES
    grid_spec = pltpu.PrefetchScalarGridSpec(
        num_scalar_prefetch=2,
        grid=(n // SCATTER_TM,),
        in_specs=[
            pl.BlockSpec((1, 1, SCATTER_TM * TOP_K), lambda i, pa, pe: (i, 0, 0), memory_space=pltpu.SMEM),
            pl.BlockSpec(memory_space=pl.ANY),
        ],
        out_specs=pl.BlockSpec(memory_space=pl.ANY),
        scratch_shapes=[pltpu.VMEM((SCATTER_SLOTS, SCATTER_TM * ROW_TILES, D_MODEL), f32),
                        pltpu.VMEM((BLOCK_TILE_ROWS, D_MODEL), f32),
                        pltpu.SemaphoreType.DMA((SCATTER_SLOTS,)),
                        pltpu.SemaphoreType.DMA((SCATTER_SLOTS,)),
                        pltpu.SemaphoreType.DMA(())],
    )
    return pl.pallas_call(
        _scatter_kernel,
        grid_spec=grid_spec,
        out_shape=jax.ShapeDtypeStruct((nb * BLOCK_TILE_ROWS, D_MODEL), f32),
        compiler_params=_cparams(1),
        name="dispatch_scatter",
    )(padded, pend, pos_tok3, h_rows2d)


def _expert_kernel(be_ref, nused_ref, x_ref, wg_ref, wu_ref, wd_ref, o_ref, wgb, wub, wdb):
    i = pl.program_id(0)
    n_used = nused_ref[0]

    @pl.when(i < n_used)
    def _():
        prev = be_ref[jnp.maximum(i - 1, 0)]

        @pl.when((i == 0) | (be_ref[i] != prev))
        def _():
            wgb[...] = wg_ref[0, 0].astype(bf16)
            wub[...] = wu_ref[0, 0].astype(bf16)
            wdb[...] = wd_ref[0, 0].astype(bf16)

        xb = x_ref[...].astype(bf16)
        act = _silu(_dot(xb, wgb[...])) * _dot(xb, wub[...])
        o_ref[...] = _dot(act.astype(bf16), wdb[...])

    @pl.when(i >= n_used)
    def _():
        o_ref[...] = jnp.zeros_like(o_ref)


def _expert_mlp(block_e, n_used, xs, wg, wu, wd, layer, nb):
    last = lambda i, nu: jnp.minimum(i, nu[0] - 1)
    wspec = lambda shape: pl.BlockSpec(shape, lambda i, be, nu: (layer, be[last(i, nu)], 0, 0))
    grid_spec = pltpu.PrefetchScalarGridSpec(
        num_scalar_prefetch=2,
        grid=(nb,),
        in_specs=[
            pl.BlockSpec((BLOCK_TILE_ROWS, D_MODEL), lambda i, be, nu: (last(i, nu), 0)),
            wspec((1, 1, D_MODEL, D_EXPERT)), wspec((1, 1, D_MODEL, D_EXPERT)), wspec((1, 1, D_EXPERT, D_MODEL)),
        ],
        out_specs=pl.BlockSpec((BLOCK_TILE_ROWS, D_MODEL), lambda i, be, nu: (i, 0)),
        scratch_shapes=[pltpu.VMEM((D_MODEL, D_EXPERT), bf16), pltpu.VMEM((D_MODEL, D_EXPERT), bf16),
                        pltpu.VMEM((D_EXPERT, D_MODEL), bf16)],
    )
    return pl.pallas_call(
        _expert_kernel,
        grid_spec=grid_spec,
        out_shape=jax.ShapeDtypeStruct((nb * BLOCK_TILE_ROWS, D_MODEL), f32),
        compiler_params=_cparams(1),
        name="expert_mlp",
    )(block_e, n_used, xs, wg, wu, wd)


COMB_TM = 64


def _combine_kernel(idx_ref, idxn_ref, wts_ref, h_ref, hb_ref, sg_ref, su_ref, sd_ref, g_ref, b_ref, ys_hbm,
                    hf_ref, hb_out_ref, buf, sem):
    i = pl.program_id(0)
    nb = pl.num_programs(0)
    slot = i % 2
    cnt = COMB_TM * TOP_K

    @pl.when(i == 0)
    def _():
        _start_gather(ys_hbm, idx_ref, buf, sem, 0, cnt)

    @pl.when(i + 1 < nb)
    def _():
        _start_gather(ys_hbm, idxn_ref, buf, sem, 1 - slot, cnt)

    pltpu.make_async_copy(ys_hbm.at[pl.ds(0, cnt * ROW_TILES)], buf.at[slot], sem.at[slot]).wait()
    wts = wts_ref[...]
    routed = None
    for k in range(TOP_K):
        term = buf[slot, pl.ds(k * COMB_TM, COMB_TM), :] * wts[:, k:k + 1]
        routed = term if routed is None else routed + term
    hb = hb_ref[...]
    act = _silu(_dot(hb, sg_ref[...])) * _dot(hb, su_ref[...])
    shared = _dot(act.astype(bf16), sd_ref[...])
    hn = _layer_norm(ALPHA * h_ref[...] + (routed + shared), g_ref[...], b_ref[...])
    hf_ref[...] = hn
    hb_out_ref[...] = hn.astype(bf16)


def _combine(pos3, wts, h, h_bf, ws_gate, ws_up, ws_down, g, b, ys2d):
    n = h.shape[0]
    tm = COMB_TM
    nb = n // tm
    cnt = tm * TOP_K
    rowspec = pl.BlockSpec((tm, D_MODEL), lambda i: (i, 0))
    const2 = lambda i: (0, 0)
    return pl.pallas_call(
        _combine_kernel,
        grid=(nb,),
        in_specs=[
            pl.BlockSpec((1, 1, cnt), lambda i: (i, 0, 0), memory_space=pltpu.SMEM),
            pl.BlockSpec((1, 1, cnt), lambda i: (jnp.minimum(i + 1, nb - 1), 0, 0), memory_space=pltpu.SMEM),
            pl.BlockSpec((tm, TOP_K), lambda i: (i, 0)),
            rowspec, rowspec,
            pl.BlockSpec((D_MODEL, D_SHARED), const2),
            pl.BlockSpec((D_MODEL, D_SHARED), const2),
            pl.BlockSpec((D_SHARED, D_MODEL), const2),
            pl.BlockSpec((1, D_MODEL), const2),
            pl.BlockSpec((1, D_MODEL), const2),
            pl.BlockSpec(memory_space=pl.ANY),
        ],
        out_specs=[rowspec, rowspec],
        out_shape=[jax.ShapeDtypeStruct((n, D_MODEL), f32),
                   jax.ShapeDtypeStruct((n, D_MODEL), bf16)],
        scratch_shapes=[pltpu.VMEM((2, cnt, D_MODEL), f32),
                        pltpu.SemaphoreType.DMA((2,))],
        compiler_params=_cparams(1),
        name="moe_combine",
    )(pos3, pos3, wts, h, h_bf, ws_gate, ws_up, ws_down, g, b, ys2d)


def _dispatch(eidx, rank, counts_f, nb):
    n = eidx.shape[0]
    counts = counts_f[0, :N_EXPERTS].astype(jnp.int32)
    padded = (counts + EXPERT_ROWS - 1) // EXPERT_ROWS * EXPERT_ROWS
    pend = jnp.cumsum(padded)
    pstart = pend - padded
    starts = jnp.arange(nb, dtype=jnp.int32) * EXPERT_ROWS
    block_e = jnp.minimum(jnp.sum((pend[None, :] <= starts[:, None]).astype(jnp.int32), axis=1), N_EXPERTS - 1)
    n_used = (pend[-1] // EXPERT_ROWS).astype(jnp.int32).reshape(1)
    onehot = eidx[:, :, None] == jnp.arange(N_EXPERTS, dtype=jnp.int32)[None, None, :]
    pos = jnp.sum(jnp.where(onehot, pstart[None, None, :], 0), axis=-1) + rank
    pos_tok3 = pos.reshape(n // SCATTER_TM, 1, SCATTER_TM * TOP_K)
    tm = COMB_TM
    pos_k3 = pos.reshape(n // tm, tm, TOP_K).transpose(0, 2, 1).reshape(n // tm, 1, tm * TOP_K)
    return block_e.astype(jnp.int32), n_used, padded, pend, pos_tok3, pos_k3


def _consts():
    r = jnp.arange(CHUNK)
    tri = (r[:, None] >= r[None, :]).astype(bf16)
    trit = (r[:, None] <= r[None, :]).astype(bf16)
    lane = jnp.arange(LANES)
    ch = jnp.arange(SSD_HEADS * SSD_HEADDIM)
    ssd_expand = (lane[:, None] == (ch[None, :] // SSD_HEADDIM)).astype(bf16)
    hc = jnp.arange(HGRN_HEADS * HGRN_DIM)
    hm_rows = jnp.arange(HGRN_HEADS * SUB)
    hmask = ((hm_rows[:, None] // SUB) == (hc[None, :] // HGRN_DIM)).astype(f32)
    qr = jnp.arange((HGRN_NBLK - 1) * HGRN_HEADS * SUB)
    qblk = qr // (HGRN_HEADS * SUB) + 1
    kstart = SUB * jnp.arange(HGRN_NBLK - 1) * jnp.arange(1, HGRN_NBLK) // 2
    kblk = jnp.sum(lane[None, :] >= kstart[:, None], axis=0)
    offmask = ((qblk[:, None] == kblk[None, :]) & (lane[None, :] < HGRN_OFF_KEYS)).astype(f32)
    rt = jnp.arange(ROUTER_TM)
    ltri = (rt[:, None] > rt[None, :]).astype(bf16)
    return tri, trit, ssd_expand, hmask, offmask, ltri


def _hgrn_lower_bounds(lb_param):
    sm = jax.nn.softmax(lb_param.astype(f32), axis=0)
    return jnp.cumsum(sm, axis=0) - sm[0:1]


def kernel(x, mem, w_in, conv_w, conv_b, dt_bias, a_log, d_skip, ssd_norm, pool_w, pool_scale, hgrn_lb, hgrn_norm, w_mem_kv, w_branch, w_o, ln1_g, ln1_b, w_router, router_bias, w_exp_gate, w_exp_up, w_exp_down, w_sh_gate, w_sh_up, w_sh_down, ln2_g, ln2_b):
    bsz, seq, d = x.shape
    n = bsz * seq
    nk = n * TOP_K
    nb = -(-nk // EXPERT_ROWS) + N_EXPERTS
    tri, trit, ssd_expand, hmask, offmask, ltri = _consts()
    lb_all = _hgrn_lower_bounds(hgrn_lb)

    c_dt0 = BRANCH_WIDTH * 2 + SSD_CONV_DIM
    c_dt1 = c_dt0 + SSD_HEADS
    c_gate = c_dt1 + 5 * BRANCH_WIDTH
    pad_lanes = LANES - SSD_HEADS

    h = x.reshape(n, d)
    h_bf = h.astype(bf16)
    mem_bf = mem.reshape(bsz * MEM_TOKENS, d).astype(bf16)

    for l in range(DEPTH):
        w_packed = jnp.concatenate(
            [w_in[l, :, :c_dt0], w_in[l, :, c_dt1:], jnp.pad(w_in[l, :, c_dt0:c_dt1], ((0, 0), (0, pad_lanes)))],
            axis=1).astype(bf16)
        n_main = c_gate - SSD_HEADS
        c_dtp = n_main + N_BRANCH * d

        proj = _matmul(h_bf, w_packed, f32, 512, 1024, 0, n_main)
        dtp = _matmul(h_bf, w_packed, f32, 512, LANES, c_dtp, LANES)
        kv = _matmul(mem_bf, w_mem_kv[l].astype(bf16), bf16, 512, 1024)

        br_pool = _pool_mixer(proj, pool_w[l].astype(bf16), pool_scale[l].reshape(1, -1), bsz, seq)
        br_ssd = _ssd_mixer(
            proj, dtp, conv_w[l], conv_b[l].reshape(1, -1),
            jnp.pad(dt_bias[l], (0, pad_lanes)).reshape(1, -1),
            jnp.pad(a_log[l], (0, pad_lanes)).reshape(1, -1),
            jnp.repeat(d_skip[l], SSD_HEADDIM).reshape(1, -1),
            ssd_norm[l].reshape(1, -1), (tri, trit, ssd_expand), bsz, seq)
        br_hgrn = _hgrn_mixer(proj, lb_all[l].reshape(1, -1), hgrn_norm[l].reshape(1, -1),
                              (tri, hmask, offmask), bsz, seq)
        br_mem = _mem_attention(proj, kv, bsz, seq)

        merged = _merge(h_bf, (br_pool, br_ssd, br_hgrn, br_mem), w_packed, n_main, w_branch[l].astype(bf16))
        h, h_bf = _wo_ln(merged, w_o[l].astype(bf16), h, ln1_g[l].reshape(1, -1), ln1_b[l].reshape(1, -1))

        eidx, wts, rank, counts = _router(h, jnp.pad(w_router[l], ((0, 0), (0, LANES - N_EXPERTS))),
                                          jnp.pad(router_bias[l], (0, LANES - N_EXPERTS)).reshape(1, -1), ltri)
        block_e, n_used, padded, pend, pos_tok3, pos_k3 = _dispatch(eidx, rank, counts, nb)
        xs = _scatter_rows(padded, pend, pos_tok3, h, nb)
        ys = _expert_mlp(block_e, n_used, xs, w_exp_gate, w_exp_up, w_exp_down, l, nb)
        h, h_bf = _combine(pos_k3, wts, h, h_bf, w_sh_gate[l].astype(bf16), w_sh_up[l].astype(bf16),
                                   w_sh_down[l].astype(bf16), ln2_g[l].reshape(1, -1), ln2_b[l].reshape(1, -1), ys)
    return h.reshape(bsz, seq, d)
```

```python
import functools

import jax
import jax.numpy as jnp
from jax import lax
from jax.experimental import pallas as pl
from jax.experimental.pallas import tpu as pltpu

f32 = jnp.float32
bf16 = jnp.bfloat16

D_MODEL = 2048
DEPTH = 4
CHUNK = 64
N_BRANCH = 4
BRANCH_WIDTH = 1024
POOL_WINDOWS = (2, 4, 8, 16)
POOL_GROUP = 256
SSD_HEADS = 16
SSD_HEADDIM = 64
SSD_STATE = 128
SSD_GROUPS = 4
SSD_CONV = 4
SSD_CONV_DIM = 2048
HGRN_HEADS = 8
HGRN_DIM = 128
MEM_TOKENS = 256
MEM_HEADS = 4
MEM_HEAD_DIM = 256
N_EXPERTS = 64
TOP_K = 8
N_GROUPS = 8
TOPK_GROUPS = 4
D_EXPERT = 256
D_SHARED = 256
ROUTED_SCALE = 2.5
MASK_SCORE = -1e4
ALPHA = (2 * DEPTH) ** 0.25
EPS = 1e-5

LANES = 128
ROW_TILES = 1
EXPERT_ROWS = 512
ZERO_ROWS = 128
SUB = 16
VMEM_LIMIT = 56 * 1024 * 1024

NEG_INF = float("-inf")


def _cparams(n_axes):
    return pltpu.CompilerParams(dimension_semantics=("arbitrary",) * n_axes,
                                vmem_limit_bytes=VMEM_LIMIT)


def _sigmoid(x):
    return jax.nn.sigmoid(x)


def _silu(x):
    return x * jax.nn.sigmoid(x)


def _split3(x):
    hi = x.astype(bf16)
    r1 = x - hi.astype(f32)
    mid = r1.astype(bf16)
    lo = (r1 - mid.astype(f32)).astype(bf16)
    return hi, mid, lo


def _dot(a, b):
    return jnp.dot(a, b, preferred_element_type=f32)


def _dot_nt(a, b):
    return lax.dot_general(a, b, (((1,), (1,)), ((), ())), preferred_element_type=f32)


def _dot_tn(a, b):
    return lax.dot_general(a, b, (((0,), (0,)), ((), ())), preferred_element_type=f32)


def _sel_left(sel01, x):
    return sum(_dot(sel01, p) for p in _split3(x))


def _sel_right(x, sel01):
    return sum(_dot(p, sel01) for p in _split3(x))


def _mm_kernel(x_ref, w_ref, o_ref):
    o_ref[...] = _dot(x_ref[...], w_ref[...]).astype(o_ref.dtype)


def _matmul(x, w, out_dtype, tm, tn, col0=0, ncols=None):
    m, k = x.shape
    n = w.shape[1] if ncols is None else ncols
    cb = col0 // tn
    return pl.pallas_call(
        _mm_kernel,
        grid=(n // tn, m // tm),
        in_specs=[pl.BlockSpec((tm, k), lambda j, i: (i, 0)),
                  pl.BlockSpec((k, tn), lambda j, i: (0, cb + j))],
        out_specs=pl.BlockSpec((tm, tn), lambda j, i: (i, j)),
        out_shape=jax.ShapeDtypeStruct((m, n), out_dtype),
        compiler_params=_cparams(2),
        name="matmul",
    )(x, w)


POOL_T = 512
POOL_HALO = 16


def _pool_kernel(u_ref, halo_ref, w_ref, scale_ref, o_ref):
    i = pl.program_id(1)
    u = u_ref[...]
    halo = jnp.where(i > 0, halo_ref[...], 0.0)
    ext = jnp.concatenate([halo, u], axis=0)
    t = i * POOL_T + lax.broadcasted_iota(jnp.int32, (POOL_T, 1), 0)
    outs = []
    for gi, w in enumerate(POOL_WINDOWS):
        xg = ext[:, gi * POOL_GROUP:(gi + 1) * POOL_GROUP]
        s = xg
        span = 1
        while span < w:
            s = s[span:, :] + s[:-span, :]
            span *= 2
        s = s[s.shape[0] - POOL_T:, :]
        cnt = jnp.minimum(t + 1, w).astype(f32)
        d = s / cnt - u[:, gi * POOL_GROUP:(gi + 1) * POOL_GROUP]
        outs.append(_dot(d.astype(bf16), w_ref[gi]))
    y = jnp.concatenate(outs, axis=-1) * scale_ref[...]
    o_ref[...] = y.astype(o_ref.dtype)


def _pool_mixer(proj, pool_w, pool_scale, bsz, seq):
    n = bsz * seq
    nt = seq // POOL_T
    return pl.pallas_call(
        _pool_kernel,
        grid=(bsz, nt),
        in_specs=[
            pl.BlockSpec((POOL_T, BRANCH_WIDTH), lambda b, i: (b * nt + i, 0)),
            pl.BlockSpec((POOL_HALO, BRANCH_WIDTH),
                         lambda b, i: (jnp.maximum((b * nt + i) * (POOL_T // POOL_HALO) - 1, 0), 0)),
            pl.BlockSpec((len(POOL_WINDOWS), POOL_GROUP, POOL_GROUP), lambda b, i: (0, 0, 0)),
            pl.BlockSpec((1, BRANCH_WIDTH), lambda b, i: (0, 0)),
        ],
        out_specs=pl.BlockSpec((POOL_T, BRANCH_WIDTH), lambda b, i: (b * nt + i, 0)),
        out_shape=jax.ShapeDtypeStruct((n, BRANCH_WIDTH), bf16),
        compiler_params=_cparams(2),
        name="pool_mixer",
    )(proj, proj, pool_w, pool_scale)


CONV_TAIL = 8


def _softplus(x):
    return jnp.maximum(x, 0.0) + jnp.log1p(jnp.exp(-jnp.abs(x)))


def _ssd_kernel(z_ref, xbc_ref, dt_ref, cw_ref, cb_ref, dtb_ref, alog_ref, dskip_ref, nw_ref,
                tri_ref, trit_ref, exp_ref, o_ref, state_ref, tail_ref):
    i = pl.program_id(0)

    @pl.when(i == 0)
    def _():
        state_ref[...] = jnp.zeros_like(state_ref)
        tail_ref[...] = jnp.zeros_like(tail_ref)

    for bi in range(z_ref.shape[0]):
        _ssd_chunk(z_ref.at[bi], xbc_ref.at[bi], dt_ref.at[bi], cw_ref, cb_ref, dtb_ref, alog_ref, dskip_ref, nw_ref,
                   tri_ref, trit_ref, exp_ref, o_ref.at[bi], state_ref.at[bi], tail_ref.at[bi])


def _ssd_chunk(z_ref, xbc_ref, dt_ref, cw_ref, cb_ref, dtb_ref, alog_ref, dskip_ref, nw_ref,
               tri_ref, trit_ref, exp_ref, o_ref, state_ref, tail_ref):
    xbc = xbc_ref[...]
    ext = jnp.concatenate([tail_ref[...], xbc], axis=0)
    cw = cw_ref[...]
    conv = cb_ref[...]
    for k in range(SSD_CONV):
        off = CONV_TAIL - (SSD_CONV - 1) + k
        conv = conv + cw[k:k + 1, :] * ext[off:off + CHUNK, :]
    tail_ref[...] = xbc[CHUNK - CONV_TAIL:, :]
    act = _silu(conv)
    inner = SSD_HEADS * SSD_HEADDIM
    gw = SSD_GROUPS * SSD_STATE
    xs = act[:, :inner]
    bmat = act[:, inner:inner + gw]
    cmat = act[:, inner + gw:]

    dt = _softplus(dt_ref[...] + dtb_ref[...])
    a = -jnp.exp(alog_ref[...])
    ad = dt * a
    ad_parts = _split3(ad)
    tri = tri_ref[...]
    acs = sum(_dot(tri, p) for p in ad_parts)
    acs_t = sum(_dot_tn(p, trit_ref[...]) for p in ad_parts)
    a_last = acs[CHUNK - 1:CHUNK, :]
    expand = exp_ref[...]
    dt_e = _sel_right(dt, expand)
    eacs_e = _sel_right(jnp.exp(acs), expand)
    dec_e = _sel_right(jnp.exp(a_last - acs), expand)
    x = xs * dt_e
    xb = x.astype(bf16)
    xd = (x * dec_e).astype(bf16)
    elast_e = eacs_e[CHUNK - 1:CHUNK, :]

    row = lax.broadcasted_iota(jnp.int32, (CHUNK, CHUNK), 0)
    col = lax.broadcasted_iota(jnp.int32, (CHUNK, CHUNK), 1)
    causal = row >= col
    hpg = SSD_HEADS // SSD_GROUPS
    gch = hpg * SSD_HEADDIM
    y_diag = []
    y_off = []
    for g in range(SSD_GROUPS):
        bg = bmat[:, g * SSD_STATE:(g + 1) * SSD_STATE].astype(bf16)
        cg = cmat[:, g * SSD_STATE:(g + 1) * SSD_STATE].astype(bf16)
        cb = _dot_nt(cg, bg)
        for hh in range(hpg):
            h = g * hpg + hh
            diff = acs[:, h:h + 1] - acs_t[h:h + 1, :]
            decay = jnp.where(causal, jnp.exp(jnp.where(causal, diff, 0.0)), 0.0)
            sc = (cb * decay).astype(bf16)
            y_diag.append(_dot(sc, xb[:, h * SSD_HEADDIM:(h + 1) * SSD_HEADDIM]))
        st = state_ref[:, g * gch:(g + 1) * gch]
        y_off.append(_dot(cg, st.astype(bf16)))
        upd = _dot_tn(bg, xd[:, g * gch:(g + 1) * gch])
        state_ref[:, g * gch:(g + 1) * gch] = st * elast_e[:, g * gch:(g + 1) * gch] + upd
    y = (jnp.concatenate(y_diag, axis=-1) + jnp.concatenate(y_off, axis=-1) * eacs_e
         + xs * dskip_ref[...])
    y = y * _silu(z_ref[...])
    nw = nw_ref[...]
    ngw = inner // SSD_GROUPS
    outs = []
    for g in range(SSD_GROUPS):
        yg = y[:, g * ngw:(g + 1) * ngw]
        ms = jnp.mean(yg * yg, axis=-1, keepdims=True)
        outs.append(yg * lax.rsqrt(ms + EPS) * nw[:, g * ngw:(g + 1) * ngw])
    o_ref[...] = jnp.concatenate(outs, axis=-1).astype(o_ref.dtype)


def _ssd_mixer(proj, dtp, conv_w, conv_b, dt_bias, a_log, d_skip_e, norm_w, consts, bsz, seq):
    n = bsz * seq
    nc = seq // CHUNK
    tri, trit, expand = consts
    const2 = lambda i: (0, 0)
    inner = SSD_HEADS * SSD_HEADDIM
    proj3 = proj.reshape(bsz, seq, proj.shape[1])
    dtp3 = dtp.reshape(bsz, seq, LANES)
    out = pl.pallas_call(
        _ssd_kernel,
        grid=(nc,),
        in_specs=[
            pl.BlockSpec((bsz, CHUNK, inner), lambda i: (0, i, 1)),
            pl.BlockSpec((bsz, CHUNK, SSD_CONV_DIM), lambda i: (0, i, 1)),
            pl.BlockSpec((bsz, CHUNK, LANES), lambda i: (0, i, 0)),
            pl.BlockSpec((SSD_CONV, SSD_CONV_DIM), const2),
            pl.BlockSpec((1, SSD_CONV_DIM), const2),
            pl.BlockSpec((1, LANES), const2),
            pl.BlockSpec((1, LANES), const2),
            pl.BlockSpec((1, inner), const2),
            pl.BlockSpec((1, inner), const2),
            pl.BlockSpec((CHUNK, CHUNK), const2),
            pl.BlockSpec((CHUNK, CHUNK), const2),
            pl.BlockSpec((LANES, inner), const2),
        ],
        out_specs=pl.BlockSpec((bsz, CHUNK, inner), lambda i: (0, i, 0)),
        out_shape=jax.ShapeDtypeStruct((bsz, seq, inner), bf16),
        scratch_shapes=[pltpu.VMEM((bsz, SSD_STATE, inner), f32),
                        pltpu.VMEM((bsz, CONV_TAIL, SSD_CONV_DIM), f32)],
        compiler_params=_cparams(1),
        name="ssd_mixer",
    )(proj3, proj3, dtp3, conv_w, conv_b, dt_bias, a_log, d_skip_e, norm_w, tri, trit, expand)
    return out.reshape(n, inner)


LOG2E = 1.4426950408889634
HGRN_NBLK = CHUNK // SUB
HGRN_OFF_KEYS = SUB * (HGRN_NBLK - 1) * HGRN_NBLK // 2


def _hgrn_chunk(q, f, v, g, lb, nw, tri, hmask, offmask, state_ref):
    width = HGRN_HEADS * HGRN_DIM
    qf = _silu(q)
    kf = (1.0 - lb) * _sigmoid(-f)
    b2 = _sel_left(tri, jnp.log1p(-kf)) * LOG2E
    c2 = b2 - jnp.log2(kf)
    b_last = b2[CHUNK - 1:CHUNK, :]
    vb = v.astype(bf16)

    qd = (qf * jnp.exp2(b2)).astype(bf16)
    kd = jnp.exp2(b_last - c2).astype(bf16)
    eb_last = jnp.exp2(b_last)
    o_inter = []
    for h in range(HGRN_HEADS):
        sl = slice(h * HGRN_DIM, (h + 1) * HGRN_DIM)
        st = state_ref[h]
        o_inter.append(_dot_nt(qd[:, sl], st.astype(bf16)))
        state_ref[h] = st * eb_last[:, sl] + _dot_tn(vb[:, sl], kd[:, sl])
    o = jnp.concatenate(o_inter, axis=-1)

    qm, kk, vv = [], [], []
    for blk in range(1, HGRN_NBLK):
        r0 = blk * SUB
        anchor = b2[r0 - 1:r0, :]
        qq = qf[r0:r0 + SUB, :] * jnp.exp2(b2[r0:r0 + SUB, :] - anchor)
        qm.append((jnp.concatenate([qq] * HGRN_HEADS, axis=0) * hmask).astype(bf16))
        kk.append(jnp.exp2(anchor - c2[:r0, :]).astype(bf16))
        vv.append(vb[:r0, :])
    zpad = jnp.zeros((LANES - HGRN_OFF_KEYS, width), bf16)
    att = _dot_nt(jnp.concatenate(qm, axis=0), jnp.concatenate(kk + [zpad], axis=0))
    res = _dot((att * offmask).astype(bf16), jnp.concatenate(vv + [zpad], axis=0))
    rows = [jnp.zeros((SUB, width), f32)]
    for blk in range(1, HGRN_NBLK):
        base = (blk - 1) * HGRN_HEADS * SUB
        acc = None
        for h in range(HGRN_HEADS):
            part = res[base + h * SUB:base + (h + 1) * SUB, :] * hmask[h * SUB:(h + 1) * SUB, :]
            acc = part if acc is None else acc + part
        rows.append(acc)
    o = o + jnp.concatenate(rows, axis=0)

    tpos = lax.broadcasted_iota(jnp.int32, (CHUNK, 1), 0) & (SUB - 1)
    diag = [None] * HGRN_HEADS
    for d in range(SUB):
        cs = c2 if d == 0 else pltpu.roll(c2, d, 0)
        vs = v if d == 0 else pltpu.roll(v, d, 0)
        p = qf * jnp.exp2(b2 - cs)
        same_block = tpos >= d
        for h in range(HGRN_HEADS):
            sl = slice(h * HGRN_DIM, (h + 1) * HGRN_DIM)
            att = jnp.where(same_block, jnp.sum(p[:, sl], axis=-1, keepdims=True), 0.0)
            term = att * vs[:, sl]
            diag[h] = term if diag[h] is None else diag[h] + term
    o = o + jnp.concatenate(diag, axis=-1)

    gate = _silu(g)
    outs = []
    for h in range(HGRN_HEADS):
        sl = slice(h * HGRN_DIM, (h + 1) * HGRN_DIM)
        oh = o[:, sl]
        ms = jnp.mean(oh * oh, axis=-1, keepdims=True)
        outs.append(oh * lax.rsqrt(ms + EPS) * nw * gate[:, sl])
    return jnp.concatenate(outs, axis=-1)


def _hgrn_kernel(q_ref, f_ref, i_ref, g_ref, lb_ref, nw_ref, tri_ref, hmask_ref, offmask_ref, o_ref, state_ref):
    c = pl.program_id(0)

    @pl.when(c == 0)
    def _():
        state_ref[...] = jnp.zeros_like(state_ref)

    for bi in range(q_ref.shape[0]):
        out = _hgrn_chunk(q_ref[bi], f_ref[bi], i_ref[bi], g_ref[bi], lb_ref[...], nw_ref[...], tri_ref[...],
                          hmask_ref[...], offmask_ref[...], state_ref.at[bi])
        o_ref[bi] = out.astype(o_ref.dtype)


def _hgrn_mixer(proj, lb, norm_w, consts, bsz, seq):
    n = bsz * seq
    nc = seq // CHUNK
    tri, hmask, offmask = consts
    width = HGRN_HEADS * HGRN_DIM
    const2 = lambda i: (0, 0)
    col = lambda j: (lambda i: (0, i, j))
    proj3 = proj.reshape(bsz, seq, proj.shape[1])
    qrows = (HGRN_NBLK - 1) * HGRN_HEADS * SUB
    out = pl.pallas_call(
        _hgrn_kernel,
        grid=(nc,),
        in_specs=[
            pl.BlockSpec((bsz, CHUNK, width), col(4)),
            pl.BlockSpec((bsz, CHUNK, width), col(5)),
            pl.BlockSpec((bsz, CHUNK, width), col(6)),
            pl.BlockSpec((bsz, CHUNK, width), col(7)),
            pl.BlockSpec((1, width), const2),
            pl.BlockSpec((1, HGRN_DIM), const2),
            pl.BlockSpec((CHUNK, CHUNK), const2),
            pl.BlockSpec((HGRN_HEADS * SUB, width), const2),
            pl.BlockSpec((qrows, LANES), const2),
        ],
        out_specs=pl.BlockSpec((bsz, CHUNK, width), lambda i: (0, i, 0)),
        out_shape=jax.ShapeDtypeStruct((bsz, seq, width), bf16),
        scratch_shapes=[pltpu.VMEM((bsz, HGRN_HEADS, HGRN_DIM, HGRN_DIM), f32)],
        compiler_params=_cparams(1),
        name="hgrn_mixer",
    )(proj3, proj3, proj3, proj3, lb, norm_w, tri, hmask, offmask)
    return out.reshape(n, width)


MEM_T = 512


def _memattn_kernel(q_ref, k_ref, v_ref, o_ref):
    q = q_ref[...].astype(bf16)
    outs = []
    for h in range(MEM_HEADS):
        sl = slice(h * MEM_HEAD_DIM, (h + 1) * MEM_HEAD_DIM)
        s = _dot_nt(q[:, sl], k_ref[:, sl]) * (MEM_HEAD_DIM ** -0.5)
        m = jnp.max(s, axis=-1, keepdims=True)
        e = jnp.exp(s - m)
        p = e / jnp.sum(e, axis=-1, keepdims=True)
        outs.append(_dot(p.astype(bf16), v_ref[:, sl]))
    o_ref[...] = jnp.concatenate(outs, axis=-1).astype(o_ref.dtype)


def _mem_attention(proj, kv, bsz, seq):
    n = bsz * seq
    nt = seq // MEM_T
    width = MEM_HEADS * MEM_HEAD_DIM
    return pl.pallas_call(
        _memattn_kernel,
        grid=(bsz, nt),
        in_specs=[
            pl.BlockSpec((MEM_T, width), lambda b, i: (b * nt + i, 8)),
            pl.BlockSpec((MEM_TOKENS, width), lambda b, i: (b, 0)),
            pl.BlockSpec((MEM_TOKENS, width), lambda b, i: (b, 1)),
        ],
        out_specs=pl.BlockSpec((MEM_T, width), lambda b, i: (b * nt + i, 0)),
        out_shape=jax.ShapeDtypeStruct((n, width), bf16),
        compiler_params=_cparams(2),
        name="mem_attention",
    )(proj, kv, kv)


MERGE_TM = 512
MERGE_TN = 512


def _merge_kernel(h_ref, b0_ref, b1_ref, b2_ref, b3_ref, g0_ref, g1_ref, g2_ref, g3_ref, wb_ref, o_ref):
    h = h_ref[...]
    acc = None
    for n, (br, wg) in enumerate(zip((b0_ref, b1_ref, b2_ref, b3_ref), (g0_ref, g1_ref, g2_ref, g3_ref))):
        gate = _sigmoid(_dot(h, wg[...]))
        term = gate * _dot(br[...], wb_ref[n])
        acc = term if acc is None else acc + term
    o_ref[...] = acc.astype(o_ref.dtype)


def _merge(h_bf, branches, w_packed, gate_col0, w_branch):
    n = h_bf.shape[0]
    tm, tn = MERGE_TM, MERGE_TN
    bspec = pl.BlockSpec((tm, BRANCH_WIDTH), lambda j, i: (i, 0))
    gspec = lambda nbr: pl.BlockSpec((D_MODEL, tn), lambda j, i: (0, (gate_col0 + nbr * D_MODEL) // tn + j))
    return pl.pallas_call(
        _merge_kernel,
        grid=(D_MODEL // tn, n // tm),
        in_specs=[pl.BlockSpec((tm, D_MODEL), lambda j, i: (i, 0)), bspec, bspec, bspec, bspec,
                  gspec(0), gspec(1), gspec(2), gspec(3),
                  pl.BlockSpec((N_BRANCH, BRANCH_WIDTH, tn), lambda j, i: (0, 0, j))],
        out_specs=pl.BlockSpec((tm, tn), lambda j, i: (i, j)),
        out_shape=jax.ShapeDtypeStruct((n, D_MODEL), bf16),
        compiler_params=_cparams(2),
        name="gated_merge",
    )(h_bf, *branches, w_packed, w_packed, w_packed, w_packed, w_branch)


def _layer_norm(x, g, b):
    mu = jnp.mean(x, axis=-1, keepdims=True)
    xc = x - mu
    var = jnp.mean(xc * xc, axis=-1, keepdims=True)
    return xc * lax.rsqrt(var + EPS) * g + b


WO_TM = 256


def _wo_ln_kernel(m_ref, w_ref, h_ref, g_ref, b_ref, hf_ref, hb_ref):
    y = ALPHA * h_ref[...] + _dot(m_ref[...], w_ref[...])
    hn = _layer_norm(y, g_ref[...], b_ref[...])
    hf_ref[...] = hn
    hb_ref[...] = hn.astype(bf16)


def _wo_ln(merged, w_o, h, g, b):
    n = h.shape[0]
    tm = WO_TM
    rowspec = pl.BlockSpec((tm, D_MODEL), lambda i: (i, 0))
    const2 = lambda i: (0, 0)
    return pl.pallas_call(
        _wo_ln_kernel,
        grid=(n // tm,),
        in_specs=[rowspec, pl.BlockSpec((D_MODEL, D_MODEL), const2), rowspec,
                  pl.BlockSpec((1, D_MODEL), const2), pl.BlockSpec((1, D_MODEL), const2)],
        out_specs=[rowspec, rowspec],
        out_shape=[jax.ShapeDtypeStruct((n, D_MODEL), f32),
                   jax.ShapeDtypeStruct((n, D_MODEL), bf16)],
        compiler_params=_cparams(1),
        name="wo_layernorm",
    )(merged, w_o, h, g, b)


ROUTER_TM = 256


def _router_kernel(h_ref, w_ref, bias_ref, ltri_ref, eidx_ref, wts_ref, rank_ref, counts_ref, cnt_ref):
    step = pl.program_id(0)

    @pl.when(step == 0)
    def _():
        cnt_ref[...] = jnp.zeros_like(cnt_ref)

    h = h_ref[...]
    w = w_ref[...]
    h_hi = h.astype(bf16)
    h_lo = (h - h_hi.astype(f32)).astype(bf16)
    w_hi = w.astype(bf16)
    w_lo = (w - w_hi.astype(f32)).astype(bf16)
    logits = _dot(h_hi, w_hi) + _dot(h_hi, w_lo) + _dot(h_lo, w_hi)
    scores = _sigmoid(logits)
    lane_i = lax.broadcasted_iota(jnp.int32, logits.shape, 1)
    lane = lane_i.astype(f32)
    real = lane_i < N_EXPERTS
    choice = jnp.where(real, scores + bias_ref[...], NEG_INF)
    gsize = N_EXPERTS // N_GROUPS
    grp = lax.shift_right_logical(lane_i, gsize.bit_length() - 1)
    big = float(LANES)

    def first_argmax(x):
        m = jnp.max(x, axis=-1, keepdims=True)
        idx = jnp.min(jnp.where(x == m, lane, big), axis=-1, keepdims=True)
        return m, idx

    gscore = []
    for g in range(N_GROUPS):
        cg = jnp.where(grp == g, choice, NEG_INF)
        m1, i1 = first_argmax(cg)
        m2 = jnp.max(jnp.where(lane == i1, NEG_INF, cg), axis=-1, keepdims=True)
        gscore.append(m1 + m2)
    keep = jnp.zeros(logits.shape, f32)
    for g in range(N_GROUPS):
        rank = jnp.zeros_like(gscore[g])
        for o in range(N_GROUPS):
            if o == g:
                continue
            ahead = (gscore[o] > gscore[g]) if o > g else (gscore[o] >= gscore[g])
            rank = rank + jnp.where(ahead, 1.0, 0.0)
        keep = jnp.where((grp == g) & (rank < TOPK_GROUPS), 1.0, keep)
    cur = jnp.where(real, jnp.where(keep > 0.0, choice, MASK_SCORE), NEG_INF)
    idxs, ws, hits = [], [], []
    sel = jnp.zeros(logits.shape, f32)
    for _ in range(TOP_K):
        _, ik = first_argmax(cur)
        hit = lane == ik
        ws.append(jnp.sum(jnp.where(hit, scores, 0.0), axis=-1, keepdims=True))
        idxs.append(ik)
        hits.append(hit)
        sel = jnp.where(hit, 1.0, sel)
        cur = jnp.where(hit, NEG_INF, cur)
    wsum = ws[0]
    for wk in ws[1:]:
        wsum = wsum + wk
    eidx_ref[...] = jnp.concatenate(idxs, axis=-1).astype(jnp.int32)
    wts_ref[...] = jnp.concatenate([wk / wsum * ROUTED_SCALE for wk in ws], axis=-1)
    before = _dot(ltri_ref[...], sel.astype(bf16)) + cnt_ref[...]
    ranks = [jnp.sum(jnp.where(hit, before, 0.0), axis=-1, keepdims=True) for hit in hits]
    rank_ref[...] = jnp.concatenate(ranks, axis=-1).astype(jnp.int32)
    cnt_ref[...] = cnt_ref[...] + jnp.sum(sel, axis=0, keepdims=True)
    counts_ref[...] = cnt_ref[...]


def _router(h, w_router_p, bias_p, ltri):
    n = h.shape[0]
    tm = ROUTER_TM
    const2 = lambda i: (0, 0)
    kspec = pl.BlockSpec((tm, TOP_K), lambda i: (i, 0))
    return pl.pallas_call(
        _router_kernel,
        grid=(n // tm,),
        in_specs=[pl.BlockSpec((tm, D_MODEL), lambda i: (i, 0)),
                  pl.BlockSpec((D_MODEL, LANES), const2),
                  pl.BlockSpec((1, LANES), const2),
                  pl.BlockSpec((tm, tm), const2)],
        out_specs=[kspec, kspec, kspec, pl.BlockSpec((1, LANES), const2)],
        out_shape=[jax.ShapeDtypeStruct((n, TOP_K), jnp.int32),
                   jax.ShapeDtypeStruct((n, TOP_K), f32),
                   jax.ShapeDtypeStruct((n, TOP_K), jnp.int32),
                   jax.ShapeDtypeStruct((1, LANES), f32)],
        scratch_shapes=[pltpu.VMEM((1, LANES), f32)],
        compiler_params=_cparams(1),
        name="router",
    )(h, w_router_p, bias_p, ltri)


GATHER_UNROLL = 8


def _start_gather(src_hbm, idx_ref, buf, sem, slot, count):
    def body(o, carry):
        for u in range(GATHER_UNROLL):
            pltpu.make_async_copy(src_hbm.at[pl.ds(idx_ref[0, 0, o * GATHER_UNROLL + u], 1)],
                                  buf.at[slot, o, pl.ds(u, 1)], sem.at[slot]).start()
        return carry
    lax.fori_loop(0, count // GATHER_UNROLL, body, 0)


SCATTER_TM = 128
SCATTER_SLOTS = 3
BLOCK_TILE_ROWS = EXPERT_ROWS * ROW_TILES


def _scatter_kernel(vend_ref, pend_ref, pos_ref, hrows_hbm, xs_hbm, stage, zbuf, lsem, ssem, zsem):
    i = pl.program_id(0)
    nsteps = pl.num_programs(0)
    cnt = SCATTER_TM * TOP_K

    @pl.when(i == 0)
    def _():
        zbuf[...] = jnp.zeros_like(zbuf)

        def piece(r0):
            return pltpu.make_async_copy(zbuf, xs_hbm.at[pl.ds(pl.multiple_of(r0, ZERO_ROWS), ZERO_ROWS)], zsem)

        def span(e):
            first = lax.shift_right_logical(vend_ref[e], ZERO_ROWS.bit_length() - 1)
            last = lax.shift_right_logical(pend_ref[e], ZERO_ROWS.bit_length() - 1)
            return first, last

        def start(e, carry):
            first, last = span(e)
            lax.fori_loop(first, last, lambda p, c: (piece(p * ZERO_ROWS).start(), c)[1], 0)
            return carry

        def wait(e, carry):
            first, last = span(e)
            lax.fori_loop(first, last, lambda p, c: (piece(p * ZERO_ROWS).wait(), c)[1], 0)
            return carry

        tail0 = lax.shift_right_logical(pend_ref[N_EXPERTS - 1], ZERO_ROWS.bit_length() - 1)
        tail1 = xs_hbm.shape[0] // ZERO_ROWS
        lax.fori_loop(0, N_EXPERTS, start, 0)
        lax.fori_loop(tail0, tail1, lambda p, c: (piece(p * ZERO_ROWS).start(), c)[1], 0)
        lax.fori_loop(0, N_EXPERTS, wait, 0)
        lax.fori_loop(tail0, tail1, lambda p, c: (piece(p * ZERO_ROWS).wait(), c)[1], 0)

    groups = SCATTER_TM // GATHER_UNROLL
    slot = lax.rem(i, SCATTER_SLOTS)
    nxt = lax.rem(i + 1, SCATTER_SLOTS)

    def load(j, s):
        src = pl.multiple_of(j * groups, groups)
        return pltpu.make_async_copy(hrows_hbm.at[pl.ds(src, groups)], stage.at[s], lsem.at[s])

    def scatter_wait(s):
        pltpu.make_async_copy(xs_hbm.at[pl.ds(0, cnt * ROW_TILES)], xs_hbm.at[pl.ds(0, cnt * ROW_TILES)],
                              ssem.at[s]).wait()

    @pl.when(i == 0)
    def _():
        load(0, 0).start()

    @pl.when(i >= SCATTER_SLOTS - 1)
    def _():
        scatter_wait(nxt)

    @pl.when(i + 1 < nsteps)
    def _():
        load(i + 1, nxt).start()

    load(i, slot).wait()

    def body(g, carry):
        for u in range(GATHER_UNROLL):
            for k in range(TOP_K):
                dst = pos_ref[0, 0, (g * GATHER_UNROLL + u) * TOP_K + k]
                pltpu.make_async_copy(stage.at[slot, g, pl.ds(u, 1)], xs_hbm.at[pl.ds(dst, 1)],
                                      ssem.at[slot]).start()
        return carry

    lax.fori_loop(0, groups, body, 0)

    @pl.when(i == nsteps - 1)
    def _():
        for back in range(SCATTER_SLOTS - 2, -1, -1):
            @pl.when(i >= back)
            def _():
                scatter_wait(lax.rem(i - back + SCATTER_SLOTS, SCATTER_SLOTS))


def _scatter_rows(vend, pend, pos_tok3, h, nb):
    n = h.shape[0]
    h_groups = h.reshape(n // GATHER_UNROLL, GATHER_UNROLL, D_MODEL)
    grid_spec = pltpu.PrefetchScalarGridSpec(
        num_scalar_prefetch=2,
        grid=(n // SCATTER_TM,),
        in_specs=[
            pl.BlockSpec((1, 1, SCATTER_TM * TOP_K), lambda i, pa, pe: (i, 0, 0), memory_space=pltpu.SMEM),
            pl.BlockSpec(memory_space=pl.ANY),
        ],
        out_specs=pl.BlockSpec(memory_space=pl.ANY),
        scratch_shapes=[pltpu.VMEM((SCATTER_SLOTS, SCATTER_TM // GATHER_UNROLL, GATHER_UNROLL, D_MODEL), f32),
                        pltpu.VMEM((ZERO_ROWS, D_MODEL), f32),
                        pltpu.SemaphoreType.DMA((SCATTER_SLOTS,)),
                        pltpu.SemaphoreType.DMA((SCATTER_SLOTS,)),
                        pltpu.SemaphoreType.DMA(())],
    )
    return pl.pallas_call(
        _scatter_kernel,
        grid_spec=grid_spec,
        out_shape=jax.ShapeDtypeStruct((nb * BLOCK_TILE_ROWS, D_MODEL), f32),
        compiler_params=_cparams(1),
        name="dispatch_scatter",
    )(vend, pend, pos_tok3, h_groups)


def _expert_kernel(be_ref, nused_ref, x_ref, wg_ref, wu_ref, wd_ref, o_ref, wgb, wub, wdb):
    i = pl.program_id(0)
    n_used = nused_ref[0]

    @pl.when(i < n_used)
    def _():
        prev = be_ref[jnp.maximum(i - 1, 0)]

        @pl.when((i == 0) | (be_ref[i] != prev))
        def _():
            wgb[...] = wg_ref[0, 0].astype(bf16)
            wub[...] = wu_ref[0, 0].astype(bf16)
            wdb[...] = wd_ref[0, 0].astype(bf16)

        xb = x_ref[...].astype(bf16)
        act = _silu(_dot(xb, wgb[...])) * _dot(xb, wub[...])
        o_ref[...] = _dot(act.astype(bf16), wdb[...])

    @pl.when(i >= n_used)
    def _():
        o_ref[...] = jnp.zeros_like(o_ref)


def _expert_mlp(block_e, n_used, xs, wg, wu, wd, layer, nb):
    last = lambda i, nu: jnp.minimum(i, nu[0] - 1)
    wspec = lambda shape: pl.BlockSpec(shape, lambda i, be, nu: (layer, be[last(i, nu)], 0, 0))
    grid_spec = pltpu.PrefetchScalarGridSpec(
        num_scalar_prefetch=2,
        grid=(nb,),
        in_specs=[
            pl.BlockSpec((BLOCK_TILE_ROWS, D_MODEL), lambda i, be, nu: (last(i, nu), 0)),
            wspec((1, 1, D_MODEL, D_EXPERT)), wspec((1, 1, D_MODEL, D_EXPERT)), wspec((1, 1, D_EXPERT, D_MODEL)),
        ],
        out_specs=pl.BlockSpec((BLOCK_TILE_ROWS, D_MODEL), lambda i, be, nu: (i, 0)),
        scratch_shapes=[pltpu.VMEM((D_MODEL, D_EXPERT), bf16), pltpu.VMEM((D_MODEL, D_EXPERT), bf16),
                        pltpu.VMEM((D_EXPERT, D_MODEL), bf16)],
    )
    return pl.pallas_call(
        _expert_kernel,
        grid_spec=grid_spec,
        out_shape=jax.ShapeDtypeStruct((nb * BLOCK_TILE_ROWS, D_MODEL), f32),
        compiler_params=_cparams(1),
        name="expert_mlp",
    )(block_e, n_used, xs, wg, wu, wd)


COMB_TM = 64


def _combine_kernel(idx_ref, idxn_ref, wts_ref, h_ref, hb_ref, sg_ref, su_ref, sd_ref, g_ref, b_ref, ys_hbm,
                    hf_ref, hb_out_ref, buf, sem):
    i = pl.program_id(0)
    nb = pl.num_programs(0)
    slot = i % 2
    cnt = COMB_TM * TOP_K

    @pl.when(i == 0)
    def _():
        _start_gather(ys_hbm, idx_ref, buf, sem, 0, cnt)

    @pl.when(i + 1 < nb)
    def _():
        _start_gather(ys_hbm, idxn_ref, buf, sem, 1 - slot, cnt)

    pltpu.make_async_copy(buf.at[slot], buf.at[slot], sem.at[slot]).wait()
    wts = wts_ref[...]
    groups = COMB_TM // GATHER_UNROLL
    routed = None
    for k in range(TOP_K):
        rows = buf[slot, pl.ds(k * groups, groups)].reshape(COMB_TM, D_MODEL)
        term = rows * wts[:, k:k + 1]
        routed = term if routed is None else routed + term
    hb = hb_ref[...]
    act = _silu(_dot(hb, sg_ref[...])) * _dot(hb, su_ref[...])
    shared = _dot(act.astype(bf16), sd_ref[...])
    hn = _layer_norm(ALPHA * h_ref[...] + (routed + shared), g_ref[...], b_ref[...])
    hf_ref[...] = hn
    hb_out_ref[...] = hn.astype(bf16)


def _combine(pos3, wts, h, h_bf, ws_gate, ws_up, ws_down, g, b, ys2d):
    n = h.shape[0]
    tm = COMB_TM
    nb = n // tm
    cnt = tm * TOP_K
    rowspec = pl.BlockSpec((tm, D_MODEL), lambda i: (i, 0))
    const2 = lambda i: (0, 0)
    return pl.pallas_call(
        _combine_kernel,
        grid=(nb,),
        in_specs=[
            pl.BlockSpec((1, 1, cnt), lambda i: (i, 0, 0), memory_space=pltpu.SMEM),
            pl.BlockSpec((1, 1, cnt), lambda i: (jnp.minimum(i + 1, nb - 1), 0, 0), memory_space=pltpu.SMEM),
            pl.BlockSpec((tm, TOP_K), lambda i: (i, 0)),
            rowspec, rowspec,
            pl.BlockSpec((D_MODEL, D_SHARED), const2),
            pl.BlockSpec((D_MODEL, D_SHARED), const2),
            pl.BlockSpec((D_SHARED, D_MODEL), const2),
            pl.BlockSpec((1, D_MODEL), const2),
            pl.BlockSpec((1, D_MODEL), const2),
            pl.BlockSpec(memory_space=pl.ANY),
        ],
        out_specs=[rowspec, rowspec],
        out_shape=[jax.ShapeDtypeStruct((n, D_MODEL), f32),
                   jax.ShapeDtypeStruct((n, D_MODEL), bf16)],
        scratch_shapes=[pltpu.VMEM((2, cnt // GATHER_UNROLL, GATHER_UNROLL, D_MODEL), f32),
                        pltpu.SemaphoreType.DMA((2,))],
        compiler_params=_cparams(1),
        name="moe_combine",
    )(pos3, pos3, wts, h, h_bf, ws_gate, ws_up, ws_down, g, b, ys2d)


def _dispatch(eidx, rank, counts_f, nb):
    n = eidx.shape[0]
    counts = counts_f[0, :N_EXPERTS].astype(jnp.int32)
    padded = (counts + EXPERT_ROWS - 1) // EXPERT_ROWS * EXPERT_ROWS
    pend = jnp.cumsum(padded)
    pstart = pend - padded
    starts = jnp.arange(nb, dtype=jnp.int32) * EXPERT_ROWS
    block_e = jnp.minimum(jnp.sum((pend[None, :] <= starts[:, None]).astype(jnp.int32), axis=1), N_EXPERTS - 1)
    n_used = (pend[-1] // EXPERT_ROWS).astype(jnp.int32).reshape(1)
    onehot = eidx[:, :, None] == jnp.arange(N_EXPERTS, dtype=jnp.int32)[None, None, :]
    pos = jnp.sum(jnp.where(onehot, pstart[None, None, :], 0), axis=-1) + rank
    pos_tok3 = pos.reshape(n // SCATTER_TM, 1, SCATTER_TM * TOP_K)
    tm = COMB_TM
    pos_k3 = pos.reshape(n // tm, tm, TOP_K).transpose(0, 2, 1).reshape(n // tm, 1, tm * TOP_K)
    return block_e.astype(jnp.int32), n_used, pstart + counts, pend, pos_tok3, pos_k3


def _consts():
    r = jnp.arange(CHUNK)
    tri = (r[:, None] >= r[None, :]).astype(bf16)
    trit = (r[:, None] <= r[None, :]).astype(bf16)
    lane = jnp.arange(LANES)
    ch = jnp.arange(SSD_HEADS * SSD_HEADDIM)
    ssd_expand = (lane[:, None] == (ch[None, :] // SSD_HEADDIM)).astype(bf16)
    hc = jnp.arange(HGRN_HEADS * HGRN_DIM)
    hm_rows = jnp.arange(HGRN_HEADS * SUB)
    hmask = ((hm_rows[:, None] // SUB) == (hc[None, :] // HGRN_DIM)).astype(f32)
    qr = jnp.arange((HGRN_NBLK - 1) * HGRN_HEADS * SUB)
    qblk = qr // (HGRN_HEADS * SUB) + 1
    kstart = SUB * jnp.arange(HGRN_NBLK - 1) * jnp.arange(1, HGRN_NBLK) // 2
    kblk = jnp.sum(lane[None, :] >= kstart[:, None], axis=0)
    offmask = ((qblk[:, None] == kblk[None, :]) & (lane[None, :] < HGRN_OFF_KEYS)).astype(f32)
    rt = jnp.arange(ROUTER_TM)
    ltri = (rt[:, None] > rt[None, :]).astype(bf16)
    return tri, trit, ssd_expand, hmask, offmask, ltri


def _hgrn_lower_bounds(lb_param):
    sm = jax.nn.softmax(lb_param.astype(f32), axis=0)
    return jnp.cumsum(sm, axis=0) - sm[0:1]


def kernel(x, mem, w_in, conv_w, conv_b, dt_bias, a_log, d_skip, ssd_norm, pool_w, pool_scale, hgrn_lb, hgrn_norm, w_mem_kv, w_branch, w_o, ln1_g, ln1_b, w_router, router_bias, w_exp_gate, w_exp_up, w_exp_down, w_sh_gate, w_sh_up, w_sh_down, ln2_g, ln2_b):
    bsz, seq, d = x.shape
    n = bsz * seq
    nk = n * TOP_K
    nb = -(-nk // EXPERT_ROWS) + N_EXPERTS
    tri, trit, ssd_expand, hmask, offmask, ltri = _consts()
    lb_all = _hgrn_lower_bounds(hgrn_lb)

    c_dt0 = BRANCH_WIDTH * 2 + SSD_CONV_DIM
    c_dt1 = c_dt0 + SSD_HEADS
    c_gate = c_dt1 + 5 * BRANCH_WIDTH
    pad_lanes = LANES - SSD_HEADS

    h = x.reshape(n, d)
    h_bf = h.astype(bf16)
    mem_bf = mem.reshape(bsz * MEM_TOKENS, d).astype(bf16)

    for l in range(DEPTH):
        w_packed = jnp.concatenate(
            [w_in[l, :, :c_dt0], w_in[l, :, c_dt1:], jnp.pad(w_in[l, :, c_dt0:c_dt1], ((0, 0), (0, pad_lanes)))],
            axis=1).astype(bf16)
        n_main = c_gate - SSD_HEADS
        c_dtp = n_main + N_BRANCH * d

        proj = _matmul(h_bf, w_packed, f32, 512, 1024, 0, n_main)
        dtp = _matmul(h_bf, w_packed, f32, 512, LANES, c_dtp, LANES)
        kv = _matmul(mem_bf, w_mem_kv[l].astype(bf16), bf16, 512, 1024)

        br_pool = _pool_mixer(proj, pool_w[l].astype(bf16), pool_scale[l].reshape(1, -1), bsz, seq)
        br_ssd = _ssd_mixer(
            proj, dtp, conv_w[l], conv_b[l].reshape(1, -1),
            jnp.pad(dt_bias[l], (0, pad_lanes)).reshape(1, -1),
            jnp.pad(a_log[l], (0, pad_lanes)).reshape(1, -1),
            jnp.repeat(d_skip[l], SSD_HEADDIM).reshape(1, -1),
            ssd_norm[l].reshape(1, -1), (tri, trit, ssd_expand), bsz, seq)
        br_hgrn = _hgrn_mixer(proj, lb_all[l].reshape(1, -1), hgrn_norm[l].reshape(1, -1),
                              (tri, hmask, offmask), bsz, seq)
        br_mem = _mem_attention(proj, kv, bsz, seq)

        merged = _merge(h_bf, (br_pool, br_ssd, br_hgrn, br_mem), w_packed, n_main, w_branch[l].astype(bf16))
        h, h_bf = _wo_ln(merged, w_o[l].astype(bf16), h, ln1_g[l].reshape(1, -1), ln1_b[l].reshape(1, -1))

        eidx, wts, rank, counts = _router(h, jnp.pad(w_router[l], ((0, 0), (0, LANES - N_EXPERTS))),
                                          jnp.pad(router_bias[l], (0, LANES - N_EXPERTS)).reshape(1, -1), ltri)
        block_e, n_used, vend, pend, pos_tok3, pos_k3 = _dispatch(eidx, rank, counts, nb)
        xs = _scatter_rows(vend, pend, pos_tok3, h, nb)
        ys = _expert_mlp(block_e, n_used, xs, w_exp_gate, w_exp_up, w_exp_down, l, nb)
        h, h_bf = _combine(pos_k3, wts, h, h_bf, w_sh_gate[l].astype(bf16), w_sh_up[l].astype(bf16),
                                   w_sh_down[l].astype(bf16), ln2_g[l].reshape(1, -1), ln2_b[l].reshape(1, -1), ys)
    return h.reshape(bsz, seq, d)
```

```python
import functools

import jax
import jax.numpy as jnp
from jax import lax
from jax.experimental import pallas as pl
from jax.experimental.pallas import tpu as pltpu

f32 = jnp.float32
bf16 = jnp.bfloat16

D_MODEL = 2048
DEPTH = 4
CHUNK = 64
N_BRANCH = 4
BRANCH_WIDTH = 1024
POOL_WINDOWS = (2, 4, 8, 16)
POOL_GROUP = 256
SSD_HEADS = 16
SSD_HEADDIM = 64
SSD_STATE = 128
SSD_GROUPS = 4
SSD_CONV = 4
SSD_CONV_DIM = 2048
HGRN_HEADS = 8
HGRN_DIM = 128
MEM_TOKENS = 256
MEM_HEADS = 4
MEM_HEAD_DIM = 256
N_EXPERTS = 64
TOP_K = 8
N_GROUPS = 8
TOPK_GROUPS = 4
D_EXPERT = 256
D_SHARED = 256
ROUTED_SCALE = 2.5
MASK_SCORE = -1e4
ALPHA = (2 * DEPTH) ** 0.25
EPS = 1e-5

LANES = 128
ROW_TILES = 1
EXPERT_ROWS = 512
ZERO_ROWS = 128
SUB = 16
VMEM_LIMIT = 56 * 1024 * 1024

NEG_INF = float("-inf")


def _cparams(n_axes):
    return pltpu.CompilerParams(dimension_semantics=("arbitrary",) * n_axes,
                                vmem_limit_bytes=VMEM_LIMIT)


def _sigmoid(x):
    return jax.nn.sigmoid(x)


def _silu(x):
    return x * jax.nn.sigmoid(x)


def _split3(x):
    hi = x.astype(bf16)
    r1 = x - hi.astype(f32)
    mid = r1.astype(bf16)
    lo = (r1 - mid.astype(f32)).astype(bf16)
    return hi, mid, lo


def _dot(a, b):
    return jnp.dot(a, b, preferred_element_type=f32)


def _dot_nt(a, b):
    return lax.dot_general(a, b, (((1,), (1,)), ((), ())), preferred_element_type=f32)


def _dot_tn(a, b):
    return lax.dot_general(a, b, (((0,), (0,)), ((), ())), preferred_element_type=f32)


def _sel_left(sel01, x):
    return sum(_dot(sel01, p) for p in _split3(x))


def _sel_right(x, sel01):
    return sum(_dot(p, sel01) for p in _split3(x))


def _mm_kernel(x_ref, w_ref, o_ref):
    o_ref[...] = _dot(x_ref[...], w_ref[0]).astype(o_ref.dtype)


def _matmul(x, w, layer, out_dtype, tm, tn, col0=0, ncols=None):
    m, k = x.shape
    n = w.shape[2] if ncols is None else ncols
    cb = col0 // tn
    return pl.pallas_call(
        _mm_kernel,
        grid=(n // tn, m // tm),
        in_specs=[pl.BlockSpec((tm, k), lambda j, i: (i, 0)),
                  pl.BlockSpec((1, k, tn), lambda j, i: (layer, 0, cb + j))],
        out_specs=pl.BlockSpec((tm, tn), lambda j, i: (i, j)),
        out_shape=jax.ShapeDtypeStruct((m, n), out_dtype),
        compiler_params=_cparams(2),
        name="matmul",
    )(x, w)


POOL_T = 512
POOL_HALO = 16


def _pool_kernel(u_ref, halo_ref, w_ref, scale_ref, o_ref):
    i = pl.program_id(1)
    u = u_ref[...]
    halo = jnp.where(i > 0, halo_ref[...], 0.0)
    ext = jnp.concatenate([halo, u], axis=0)
    t = i * POOL_T + lax.broadcasted_iota(jnp.int32, (POOL_T, 1), 0)
    outs = []
    for gi, w in enumerate(POOL_WINDOWS):
        xg = ext[:, gi * POOL_GROUP:(gi + 1) * POOL_GROUP]
        s = xg
        span = 1
        while span < w:
            s = s[span:, :] + s[:-span, :]
            span *= 2
        s = s[s.shape[0] - POOL_T:, :]
        cnt = jnp.minimum(t + 1, w).astype(f32)
        d = s / cnt - u[:, gi * POOL_GROUP:(gi + 1) * POOL_GROUP]
        outs.append(_dot(d.astype(bf16), w_ref[gi]))
    y = jnp.concatenate(outs, axis=-1) * scale_ref[...]
    o_ref[...] = y.astype(o_ref.dtype)


def _pool_mixer(proj, pool_w, pool_scale, bsz, seq):
    n = bsz * seq
    nt = seq // POOL_T
    return pl.pallas_call(
        _pool_kernel,
        grid=(bsz, nt),
        in_specs=[
            pl.BlockSpec((POOL_T, BRANCH_WIDTH), lambda b, i: (b * nt + i, 0)),
            pl.BlockSpec((POOL_HALO, BRANCH_WIDTH),
                         lambda b, i: (jnp.maximum((b * nt + i) * (POOL_T // POOL_HALO) - 1, 0), 0)),
            pl.BlockSpec((len(POOL_WINDOWS), POOL_GROUP, POOL_GROUP), lambda b, i: (0, 0, 0)),
            pl.BlockSpec((1, BRANCH_WIDTH), lambda b, i: (0, 0)),
        ],
        out_specs=pl.BlockSpec((POOL_T, BRANCH_WIDTH), lambda b, i: (b * nt + i, 0)),
        out_shape=jax.ShapeDtypeStruct((n, BRANCH_WIDTH), bf16),
        compiler_params=_cparams(2),
        name="pool_mixer",
    )(proj, proj, pool_w, pool_scale)


CONV_TAIL = 8


def _softplus(x):
    return jnp.maximum(x, 0.0) + jnp.log1p(jnp.exp(-jnp.abs(x)))


def _ssd_kernel(z_ref, xbc_ref, dt_ref, cw_ref, cb_ref, dtb_ref, alog_ref, dskip_ref, nw_ref,
                tri_ref, trit_ref, exp_ref, o_ref, state_ref, tail_ref):
    i = pl.program_id(0)

    @pl.when(i == 0)
    def _():
        state_ref[...] = jnp.zeros_like(state_ref)
        tail_ref[...] = jnp.zeros_like(tail_ref)

    for bi in range(z_ref.shape[0]):
        _ssd_chunk(z_ref.at[bi], xbc_ref.at[bi], dt_ref.at[bi], cw_ref, cb_ref, dtb_ref, alog_ref, dskip_ref, nw_ref,
                   tri_ref, trit_ref, exp_ref, o_ref.at[bi], state_ref.at[bi], tail_ref.at[bi])


def _ssd_chunk(z_ref, xbc_ref, dt_ref, cw_ref, cb_ref, dtb_ref, alog_ref, dskip_ref, nw_ref,
               tri_ref, trit_ref, exp_ref, o_ref, state_ref, tail_ref):
    xbc = xbc_ref[...]
    ext = jnp.concatenate([tail_ref[...], xbc], axis=0)
    cw = cw_ref[...]
    conv = cb_ref[...]
    for k in range(SSD_CONV):
        off = CONV_TAIL - (SSD_CONV - 1) + k
        conv = conv + cw[k:k + 1, :] * ext[off:off + CHUNK, :]
    tail_ref[...] = xbc[CHUNK - CONV_TAIL:, :]
    act = _silu(conv)
    inner = SSD_HEADS * SSD_HEADDIM
    gw = SSD_GROUPS * SSD_STATE
    xs = act[:, :inner]
    bmat = act[:, inner:inner + gw]
    cmat = act[:, inner + gw:]

    dt = _softplus(dt_ref[...] + dtb_ref[...])
    a = -jnp.exp(alog_ref[...])
    ad = dt * a
    ad_parts = _split3(ad)
    tri = tri_ref[...]
    acs = sum(_dot(tri, p) for p in ad_parts)
    acs_t = sum(_dot_tn(p, trit_ref[...]) for p in ad_parts)
    a_last = acs[CHUNK - 1:CHUNK, :]
    expand = exp_ref[...]
    dt_e = _sel_right(dt, expand)
    eacs_e = _sel_right(jnp.exp(acs), expand)
    dec_e = _sel_right(jnp.exp(a_last - acs), expand)
    x = xs * dt_e
    xb = x.astype(bf16)
    xd = (x * dec_e).astype(bf16)
    elast_e = eacs_e[CHUNK - 1:CHUNK, :]

    row = lax.broadcasted_iota(jnp.int32, (CHUNK, CHUNK), 0)
    col = lax.broadcasted_iota(jnp.int32, (CHUNK, CHUNK), 1)
    causal = row >= col
    hpg = SSD_HEADS // SSD_GROUPS
    gch = hpg * SSD_HEADDIM
    y_diag = []
    y_off = []
    for g in range(SSD_GROUPS):
        bg = bmat[:, g * SSD_STATE:(g + 1) * SSD_STATE].astype(bf16)
        cg = cmat[:, g * SSD_STATE:(g + 1) * SSD_STATE].astype(bf16)
        cb = _dot_nt(cg, bg)
        for hh in range(hpg):
            h = g * hpg + hh
            diff = acs[:, h:h + 1] - acs_t[h:h + 1, :]
            decay = jnp.where(causal, jnp.exp(jnp.where(causal, diff, 0.0)), 0.0)
            sc = (cb * decay).astype(bf16)
            y_diag.append(_dot(sc, xb[:, h * SSD_HEADDIM:(h + 1) * SSD_HEADDIM]))
        st = state_ref[:, g * gch:(g + 1) * gch]
        y_off.append(_dot(cg, st.astype(bf16)))
        upd = _dot_tn(bg, xd[:, g * gch:(g + 1) * gch])
        state_ref[:, g * gch:(g + 1) * gch] = st * elast_e[:, g * gch:(g + 1) * gch] + upd
    y = (jnp.concatenate(y_diag, axis=-1) + jnp.concatenate(y_off, axis=-1) * eacs_e
         + xs * dskip_ref[...])
    y = y * _silu(z_ref[...])
    nw = nw_ref[...]
    ngw = inner // SSD_GROUPS
    outs = []
    for g in range(SSD_GROUPS):
        yg = y[:, g * ngw:(g + 1) * ngw]
        ms = jnp.mean(yg * yg, axis=-1, keepdims=True)
        outs.append(yg * lax.rsqrt(ms + EPS) * nw[:, g * ngw:(g + 1) * ngw])
    o_ref[...] = jnp.concatenate(outs, axis=-1).astype(o_ref.dtype)


def _ssd_mixer(proj, dtp, conv_w, conv_b, dt_bias, a_log, d_skip_e, norm_w, consts, bsz, seq):
    n = bsz * seq
    nc = seq // CHUNK
    tri, trit, expand = consts
    const2 = lambda i: (0, 0)
    inner = SSD_HEADS * SSD_HEADDIM
    proj3 = proj.reshape(bsz, seq, proj.shape[1])
    dtp3 = dtp.reshape(bsz, seq, LANES)
    out = pl.pallas_call(
        _ssd_kernel,
        grid=(nc,),
        in_specs=[
            pl.BlockSpec((bsz, CHUNK, inner), lambda i: (0, i, 1)),
            pl.BlockSpec((bsz, CHUNK, SSD_CONV_DIM), lambda i: (0, i, 1)),
            pl.BlockSpec((bsz, CHUNK, LANES), lambda i: (0, i, 0)),
            pl.BlockSpec((SSD_CONV, SSD_CONV_DIM), const2),
            pl.BlockSpec((1, SSD_CONV_DIM), const2),
            pl.BlockSpec((1, LANES), const2),
            pl.BlockSpec((1, LANES), const2),
            pl.BlockSpec((1, inner), const2),
            pl.BlockSpec((1, inner), const2),
            pl.BlockSpec((CHUNK, CHUNK), const2),
            pl.BlockSpec((CHUNK, CHUNK), const2),
            pl.BlockSpec((LANES, inner), const2),
        ],
        out_specs=pl.BlockSpec((bsz, CHUNK, inner), lambda i: (0, i, 0)),
        out_shape=jax.ShapeDtypeStruct((bsz, seq, inner), bf16),
        scratch_shapes=[pltpu.VMEM((bsz, SSD_STATE, inner), f32),
                        pltpu.VMEM((bsz, CONV_TAIL, SSD_CONV_DIM), f32)],
        compiler_params=_cparams(1),
        name="ssd_mixer",
    )(proj3, proj3, dtp3, conv_w, conv_b, dt_bias, a_log, d_skip_e, norm_w, tri, trit, expand)
    return out.reshape(n, inner)


LOG2E = 1.4426950408889634
HGRN_NBLK = CHUNK // SUB
HGRN_OFF_KEYS = SUB * (HGRN_NBLK - 1) * HGRN_NBLK // 2


def _hgrn_chunk(q, f, v, g, lb, nw, tri, hmask, offmask, state_ref):
    width = HGRN_HEADS * HGRN_DIM
    qf = _silu(q)
    kf = (1.0 - lb) * _sigmoid(-f)
    b2 = _sel_left(tri, jnp.log1p(-kf)) * LOG2E
    c2 = b2 - jnp.log2(kf)
    b_last = b2[CHUNK - 1:CHUNK, :]
    vb = v.astype(bf16)

    qd = (qf * jnp.exp2(b2)).astype(bf16)
    kd = jnp.exp2(b_last - c2).astype(bf16)
    eb_last = jnp.exp2(b_last)
    o_inter = []
    for h in range(HGRN_HEADS):
        sl = slice(h * HGRN_DIM, (h + 1) * HGRN_DIM)
        st = state_ref[h]
        o_inter.append(_dot_nt(qd[:, sl], st.astype(bf16)))
        state_ref[h] = st * eb_last[:, sl] + _dot_tn(vb[:, sl], kd[:, sl])
    o = jnp.concatenate(o_inter, axis=-1)

    qm, kk, vv = [], [], []
    for blk in range(1, HGRN_NBLK):
        r0 = blk * SUB
        anchor = b2[r0 - 1:r0, :]
        qq = qf[r0:r0 + SUB, :] * jnp.exp2(b2[r0:r0 + SUB, :] - anchor)
        qm.append((jnp.concatenate([qq] * HGRN_HEADS, axis=0) * hmask).astype(bf16))
        kk.append(jnp.exp2(anchor - c2[:r0, :]).astype(bf16))
        vv.append(vb[:r0, :])
    zpad = jnp.zeros((LANES - HGRN_OFF_KEYS, width), bf16)
    att = _dot_nt(jnp.concatenate(qm, axis=0), jnp.concatenate(kk + [zpad], axis=0))
    res = _dot((att * offmask).astype(bf16), jnp.concatenate(vv + [zpad], axis=0))
    rows = [jnp.zeros((SUB, width), f32)]
    for blk in range(1, HGRN_NBLK):
        base = (blk - 1) * HGRN_HEADS * SUB
        acc = None
        for h in range(HGRN_HEADS):
            part = res[base + h * SUB:base + (h + 1) * SUB, :] * hmask[h * SUB:(h + 1) * SUB, :]
            acc = part if acc is None else acc + part
        rows.append(acc)
    o = o + jnp.concatenate(rows, axis=0)

    tpos = lax.broadcasted_iota(jnp.int32, (CHUNK, 1), 0) & (SUB - 1)
    diag = [None] * HGRN_HEADS
    for d in range(SUB):
        cs = c2 if d == 0 else pltpu.roll(c2, d, 0)
        vs = v if d == 0 else pltpu.roll(v, d, 0)
        p = qf * jnp.exp2(b2 - cs)
        same_block = tpos >= d
        for h in range(HGRN_HEADS):
            sl = slice(h * HGRN_DIM, (h + 1) * HGRN_DIM)
            att = jnp.where(same_block, jnp.sum(p[:, sl], axis=-1, keepdims=True), 0.0)
            term = att * vs[:, sl]
            diag[h] = term if diag[h] is None else diag[h] + term
    o = o + jnp.concatenate(diag, axis=-1)

    gate = _silu(g)
    outs = []
    for h in range(HGRN_HEADS):
        sl = slice(h * HGRN_DIM, (h + 1) * HGRN_DIM)
        oh = o[:, sl]
        ms = jnp.mean(oh * oh, axis=-1, keepdims=True)
        outs.append(oh * lax.rsqrt(ms + EPS) * nw * gate[:, sl])
    return jnp.concatenate(outs, axis=-1)


def _hgrn_kernel(q_ref, f_ref, i_ref, g_ref, lb_ref, nw_ref, tri_ref, hmask_ref, offmask_ref, o_ref, state_ref):
    c = pl.program_id(0)

    @pl.when(c == 0)
    def _():
        state_ref[...] = jnp.zeros_like(state_ref)

    for bi in range(q_ref.shape[0]):
        out = _hgrn_chunk(q_ref[bi], f_ref[bi], i_ref[bi], g_ref[bi], lb_ref[...], nw_ref[...], tri_ref[...],
                          hmask_ref[...], offmask_ref[...], state_ref.at[bi])
        o_ref[bi] = out.astype(o_ref.dtype)


def _hgrn_mixer(proj, lb, norm_w, consts, bsz, seq):
    n = bsz * seq
    nc = seq // CHUNK
    tri, hmask, offmask = consts
    width = HGRN_HEADS * HGRN_DIM
    const2 = lambda i: (0, 0)
    col = lambda j: (lambda i: (0, i, j))
    proj3 = proj.reshape(bsz, seq, proj.shape[1])
    qrows = (HGRN_NBLK - 1) * HGRN_HEADS * SUB
    out = pl.pallas_call(
        _hgrn_kernel,
        grid=(nc,),
        in_specs=[
            pl.BlockSpec((bsz, CHUNK, width), col(4)),
            pl.BlockSpec((bsz, CHUNK, width), col(5)),
            pl.BlockSpec((bsz, CHUNK, width), col(6)),
            pl.BlockSpec((bsz, CHUNK, width), col(7)),
            pl.BlockSpec((1, width), const2),
            pl.BlockSpec((1, HGRN_DIM), const2),
            pl.BlockSpec((CHUNK, CHUNK), const2),
            pl.BlockSpec((HGRN_HEADS * SUB, width), const2),
            pl.BlockSpec((qrows, LANES), const2),
        ],
        out_specs=pl.BlockSpec((bsz, CHUNK, width), lambda i: (0, i, 0)),
        out_shape=jax.ShapeDtypeStruct((bsz, seq, width), bf16),
        scratch_shapes=[pltpu.VMEM((bsz, HGRN_HEADS, HGRN_DIM, HGRN_DIM), f32)],
        compiler_params=_cparams(1),
        name="hgrn_mixer",
    )(proj3, proj3, proj3, proj3, lb, norm_w, tri, hmask, offmask)
    return out.reshape(n, width)


MEM_T = 512


def _memattn_kernel(q_ref, k_ref, v_ref, o_ref):
    q = q_ref[...].astype(bf16)
    outs = []
    for h in range(MEM_HEADS):
        sl = slice(h * MEM_HEAD_DIM, (h + 1) * MEM_HEAD_DIM)
        s = _dot_nt(q[:, sl], k_ref[:, sl]) * (MEM_HEAD_DIM ** -0.5)
        m = jnp.max(s, axis=-1, keepdims=True)
        e = jnp.exp(s - m)
        p = e / jnp.sum(e, axis=-1, keepdims=True)
        outs.append(_dot(p.astype(bf16), v_ref[:, sl]))
    o_ref[...] = jnp.concatenate(outs, axis=-1).astype(o_ref.dtype)


def _mem_attention(proj, kv, bsz, seq):
    n = bsz * seq
    nt = seq // MEM_T
    width = MEM_HEADS * MEM_HEAD_DIM
    return pl.pallas_call(
        _memattn_kernel,
        grid=(bsz, nt),
        in_specs=[
            pl.BlockSpec((MEM_T, width), lambda b, i: (b * nt + i, 8)),
            pl.BlockSpec((MEM_TOKENS, width), lambda b, i: (b, 0)),
            pl.BlockSpec((MEM_TOKENS, width), lambda b, i: (b, 1)),
        ],
        out_specs=pl.BlockSpec((MEM_T, width), lambda b, i: (b * nt + i, 0)),
        out_shape=jax.ShapeDtypeStruct((n, width), bf16),
        compiler_params=_cparams(2),
        name="mem_attention",
    )(proj, kv, kv)


MERGE_TM = 512
MERGE_TN = 512


def _merge_kernel(h_ref, b0_ref, b1_ref, b2_ref, b3_ref, g0_ref, g1_ref, g2_ref, g3_ref, wb_ref, o_ref):
    h = h_ref[...]
    acc = None
    for n, (br, wg) in enumerate(zip((b0_ref, b1_ref, b2_ref, b3_ref), (g0_ref, g1_ref, g2_ref, g3_ref))):
        gate = _sigmoid(_dot(h, wg[0]))
        term = gate * _dot(br[...], wb_ref[n])
        acc = term if acc is None else acc + term
    o_ref[...] = acc.astype(o_ref.dtype)


def _merge(h_bf, branches, w_packed, layer, gate_col0, w_branch):
    n = h_bf.shape[0]
    tm, tn = MERGE_TM, MERGE_TN
    bspec = pl.BlockSpec((tm, BRANCH_WIDTH), lambda j, i: (i, 0))
    gspec = lambda nbr: pl.BlockSpec((1, D_MODEL, tn),
                                     lambda j, i: (layer, 0, (gate_col0 + nbr * D_MODEL) // tn + j))
    return pl.pallas_call(
        _merge_kernel,
        grid=(D_MODEL // tn, n // tm),
        in_specs=[pl.BlockSpec((tm, D_MODEL), lambda j, i: (i, 0)), bspec, bspec, bspec, bspec,
                  gspec(0), gspec(1), gspec(2), gspec(3),
                  pl.BlockSpec((N_BRANCH, BRANCH_WIDTH, tn), lambda j, i: (0, 0, j))],
        out_specs=pl.BlockSpec((tm, tn), lambda j, i: (i, j)),
        out_shape=jax.ShapeDtypeStruct((n, D_MODEL), bf16),
        compiler_params=_cparams(2),
        name="gated_merge",
    )(h_bf, *branches, w_packed, w_packed, w_packed, w_packed, w_branch)


def _layer_norm(x, g, b):
    mu = jnp.mean(x, axis=-1, keepdims=True)
    xc = x - mu
    var = jnp.mean(xc * xc, axis=-1, keepdims=True)
    return xc * lax.rsqrt(var + EPS) * g + b


WO_TM = 256


HALF = D_MODEL // 2
U16_HI = 0xFFFF0000


def _pack_halves(x):
    lo = lax.bitcast_convert_type(x[:, :HALF].astype(bf16).astype(f32), jnp.uint32)
    hi = lax.bitcast_convert_type(x[:, HALF:].astype(bf16).astype(f32), jnp.uint32)
    return lax.shift_right_logical(lo, jnp.uint32(16)) | (hi & jnp.uint32(U16_HI))


def _unpack_halves(w):
    lo = lax.bitcast_convert_type(lax.shift_left(w, jnp.uint32(16)), f32)
    hi = lax.bitcast_convert_type(w & jnp.uint32(U16_HI), f32)
    return lo, hi


def _wo_ln_kernel(m_ref, w_ref, h_ref, g_ref, b_ref, hf_ref, hb_ref, hp_ref):
    y = ALPHA * h_ref[...] + _dot(m_ref[...], w_ref[...])
    hn = _layer_norm(y, g_ref[...], b_ref[...])
    hf_ref[...] = hn
    hb_ref[...] = hn.astype(bf16)
    hp_ref[...] = _pack_halves(hn)


def _wo_ln(merged, w_o, h, g, b):
    n = h.shape[0]
    tm = WO_TM
    rowspec = pl.BlockSpec((tm, D_MODEL), lambda i: (i, 0))
    const2 = lambda i: (0, 0)
    return pl.pallas_call(
        _wo_ln_kernel,
        grid=(n // tm,),
        in_specs=[rowspec, pl.BlockSpec((D_MODEL, D_MODEL), const2), rowspec,
                  pl.BlockSpec((1, D_MODEL), const2), pl.BlockSpec((1, D_MODEL), const2)],
        out_specs=[rowspec, rowspec, pl.BlockSpec((tm, HALF), lambda i: (i, 0))],
        out_shape=[jax.ShapeDtypeStruct((n, D_MODEL), f32),
                   jax.ShapeDtypeStruct((n, D_MODEL), bf16),
                   jax.ShapeDtypeStruct((n, HALF), jnp.uint32)],
        compiler_params=_cparams(1),
        name="wo_layernorm",
    )(merged, w_o, h, g, b)


ROUTER_TM = 256


def _router_kernel(h_ref, w_ref, bias_ref, ltri_ref, eidx_ref, wts_ref, rank_ref, counts_ref, cnt_ref):
    step = pl.program_id(0)

    @pl.when(step == 0)
    def _():
        cnt_ref[...] = jnp.zeros_like(cnt_ref)

    h = h_ref[...]
    w = w_ref[...]
    h_hi = h.astype(bf16)
    h_lo = (h - h_hi.astype(f32)).astype(bf16)
    w_hi = w.astype(bf16)
    w_lo = (w - w_hi.astype(f32)).astype(bf16)
    logits = _dot(h_hi, w_hi) + _dot(h_hi, w_lo) + _dot(h_lo, w_hi)
    scores = _sigmoid(logits)
    lane_i = lax.broadcasted_iota(jnp.int32, logits.shape, 1)
    lane = lane_i.astype(f32)
    real = lane_i < N_EXPERTS
    choice = jnp.where(real, scores + bias_ref[...], NEG_INF)
    gsize = N_EXPERTS // N_GROUPS
    grp = lax.shift_right_logical(lane_i, gsize.bit_length() - 1)
    big = float(LANES)

    def first_argmax(x):
        m = jnp.max(x, axis=-1, keepdims=True)
        idx = jnp.min(jnp.where(x == m, lane, big), axis=-1, keepdims=True)
        return m, idx

    gscore = []
    for g in range(N_GROUPS):
        cg = jnp.where(grp == g, choice, NEG_INF)
        m1, i1 = first_argmax(cg)
        m2 = jnp.max(jnp.where(lane == i1, NEG_INF, cg), axis=-1, keepdims=True)
        gscore.append(m1 + m2)
    keep = jnp.zeros(logits.shape, f32)
    for g in range(N_GROUPS):
        rank = jnp.zeros_like(gscore[g])
        for o in range(N_GROUPS):
            if o == g:
                continue
            ahead = (gscore[o] > gscore[g]) if o > g else (gscore[o] >= gscore[g])
            rank = rank + jnp.where(ahead, 1.0, 0.0)
        keep = jnp.where((grp == g) & (rank < TOPK_GROUPS), 1.0, keep)
    cur = jnp.where(real, jnp.where(keep > 0.0, choice, MASK_SCORE), NEG_INF)
    idxs, ws, hits = [], [], []
    sel = jnp.zeros(logits.shape, f32)
    for _ in range(TOP_K):
        _, ik = first_argmax(cur)
        hit = lane == ik
        ws.append(jnp.sum(jnp.where(hit, scores, 0.0), axis=-1, keepdims=True))
        idxs.append(ik)
        hits.append(hit)
        sel = jnp.where(hit, 1.0, sel)
        cur = jnp.where(hit, NEG_INF, cur)
    wsum = ws[0]
    for wk in ws[1:]:
        wsum = wsum + wk
    eidx_ref[...] = jnp.concatenate(idxs, axis=-1).astype(jnp.int32)
    wts_ref[...] = jnp.concatenate([wk / wsum * ROUTED_SCALE for wk in ws], axis=-1)
    before = _dot(ltri_ref[...], sel.astype(bf16)) + cnt_ref[...]
    ranks = [jnp.sum(jnp.where(hit, before, 0.0), axis=-1, keepdims=True) for hit in hits]
    rank_ref[...] = jnp.concatenate(ranks, axis=-1).astype(jnp.int32)
    cnt_ref[...] = cnt_ref[...] + jnp.sum(sel, axis=0, keepdims=True)
    counts_ref[...] = cnt_ref[...]


def _router(h, w_router_p, bias_p, ltri):
    n = h.shape[0]
    tm = ROUTER_TM
    const2 = lambda i: (0, 0)
    kspec = pl.BlockSpec((tm, TOP_K), lambda i: (i, 0))
    return pl.pallas_call(
        _router_kernel,
        grid=(n // tm,),
        in_specs=[pl.BlockSpec((tm, D_MODEL), lambda i: (i, 0)),
                  pl.BlockSpec((D_MODEL, LANES), const2),
                  pl.BlockSpec((1, LANES), const2),
                  pl.BlockSpec((tm, tm), const2)],
        out_specs=[kspec, kspec, kspec, pl.BlockSpec((1, LANES), const2)],
        out_shape=[jax.ShapeDtypeStruct((n, TOP_K), jnp.int32),
                   jax.ShapeDtypeStruct((n, TOP_K), f32),
                   jax.ShapeDtypeStruct((n, TOP_K), jnp.int32),
                   jax.ShapeDtypeStruct((1, LANES), f32)],
        scratch_shapes=[pltpu.VMEM((1, LANES), f32)],
        compiler_params=_cparams(1),
        name="router",
    )(h, w_router_p, bias_p, ltri)


GATHER_UNROLL = 8


def _start_gather(src_hbm, idx_ref, buf, sem, slot, count):
    def body(o, carry):
        for u in range(GATHER_UNROLL):
            pltpu.make_async_copy(src_hbm.at[pl.ds(idx_ref[0, 0, o * GATHER_UNROLL + u], 1)],
                                  buf.at[slot, o, pl.ds(u, 1)], sem.at[slot]).start()
        return carry
    lax.fori_loop(0, count // GATHER_UNROLL, body, 0)


SCATTER_TM = 128
SCATTER_SLOTS = 3
BLOCK_TILE_ROWS = EXPERT_ROWS * ROW_TILES


def _scatter_kernel(vend_ref, pend_ref, pos_ref, hrows_hbm, xs_hbm, stage, zbuf, lsem, ssem, zsem):
    i = pl.program_id(0)
    nsteps = pl.num_programs(0)
    cnt = SCATTER_TM * TOP_K

    @pl.when(i == 0)
    def _():
        zbuf[...] = jnp.zeros_like(zbuf)

        def piece(r0):
            return pltpu.make_async_copy(zbuf, xs_hbm.at[pl.ds(pl.multiple_of(r0, ZERO_ROWS), ZERO_ROWS)], zsem)

        def span(e):
            first = lax.shift_right_logical(vend_ref[e], ZERO_ROWS.bit_length() - 1)
            last = lax.shift_right_logical(pend_ref[e], ZERO_ROWS.bit_length() - 1)
            return first, last

        def start(e, carry):
            first, last = span(e)
            lax.fori_loop(first, last, lambda p, c: (piece(p * ZERO_ROWS).start(), c)[1], 0)
            return carry

        def wait(e, carry):
            first, last = span(e)
            lax.fori_loop(first, last, lambda p, c: (piece(p * ZERO_ROWS).wait(), c)[1], 0)
            return carry

        tail0 = lax.shift_right_logical(pend_ref[N_EXPERTS - 1], ZERO_ROWS.bit_length() - 1)
        tail1 = xs_hbm.shape[0] // ZERO_ROWS
        lax.fori_loop(0, N_EXPERTS, start, 0)
        lax.fori_loop(tail0, tail1, lambda p, c: (piece(p * ZERO_ROWS).start(), c)[1], 0)
        lax.fori_loop(0, N_EXPERTS, wait, 0)
        lax.fori_loop(tail0, tail1, lambda p, c: (piece(p * ZERO_ROWS).wait(), c)[1], 0)

    groups = SCATTER_TM // GATHER_UNROLL
    slot = lax.rem(i, SCATTER_SLOTS)
    nxt = lax.rem(i + 1, SCATTER_SLOTS)

    def load(j, s):
        src = pl.multiple_of(j * groups, groups)
        return pltpu.make_async_copy(hrows_hbm.at[pl.ds(src, groups)], stage.at[s], lsem.at[s])

    def scatter_wait(s):
        pltpu.make_async_copy(xs_hbm.at[pl.ds(0, cnt * ROW_TILES)], xs_hbm.at[pl.ds(0, cnt * ROW_TILES)],
                              ssem.at[s]).wait()

    @pl.when(i == 0)
    def _():
        load(0, 0).start()

    @pl.when(i >= SCATTER_SLOTS - 1)
    def _():
        scatter_wait(nxt)

    @pl.when(i + 1 < nsteps)
    def _():
        load(i + 1, nxt).start()

    load(i, slot).wait()

    def body(g, carry):
        for u in range(GATHER_UNROLL):
            for k in range(TOP_K):
                dst = pos_ref[0, 0, (g * GATHER_UNROLL + u) * TOP_K + k]
                pltpu.make_async_copy(stage.at[slot, g, pl.ds(u, 1)], xs_hbm.at[pl.ds(dst, 1)],
                                      ssem.at[slot]).start()
        return carry

    lax.fori_loop(0, groups, body, 0)

    @pl.when(i == nsteps - 1)
    def _():
        for back in range(SCATTER_SLOTS - 2, -1, -1):
            @pl.when(i >= back)
            def _():
                scatter_wait(lax.rem(i - back + SCATTER_SLOTS, SCATTER_SLOTS))


def _scatter_rows(vend, pend, pos_tok3, h, nb):
    n = h.shape[0]
    h_groups = h.reshape(n // GATHER_UNROLL, GATHER_UNROLL, HALF)
    grid_spec = pltpu.PrefetchScalarGridSpec(
        num_scalar_prefetch=2,
        grid=(n // SCATTER_TM,),
        in_specs=[
            pl.BlockSpec((1, 1, SCATTER_TM * TOP_K), lambda i, pa, pe: (i, 0, 0), memory_space=pltpu.SMEM),
            pl.BlockSpec(memory_space=pl.ANY),
        ],
        out_specs=pl.BlockSpec(memory_space=pl.ANY),
        scratch_shapes=[pltpu.VMEM((SCATTER_SLOTS, SCATTER_TM // GATHER_UNROLL, GATHER_UNROLL, HALF), jnp.uint32),
                        pltpu.VMEM((ZERO_ROWS, HALF), jnp.uint32),
                        pltpu.SemaphoreType.DMA((SCATTER_SLOTS,)),
                        pltpu.SemaphoreType.DMA((SCATTER_SLOTS,)),
                        pltpu.SemaphoreType.DMA(())],
    )
    return pl.pallas_call(
        _scatter_kernel,
        grid_spec=grid_spec,
        out_shape=jax.ShapeDtypeStruct((nb * BLOCK_TILE_ROWS, HALF), jnp.uint32),
        compiler_params=_cparams(1),
        name="dispatch_scatter",
    )(vend, pend, pos_tok3, h_groups)


def _expert_kernel(be_ref, nused_ref, x_ref, wg_ref, wu_ref, wd_ref, o_ref, wgb, wub, wdb):
    i = pl.program_id(0)
    n_used = nused_ref[0]

    @pl.when(i < n_used)
    def _():
        prev = be_ref[jnp.maximum(i - 1, 0)]

        @pl.when((i == 0) | (be_ref[i] != prev))
        def _():
            wgb[...] = wg_ref[0, 0].astype(bf16)
            wub[...] = wu_ref[0, 0].astype(bf16)
            wdb[...] = wd_ref[0, 0].astype(bf16)

        x_lo, x_hi = _unpack_halves(x_ref[...])
        xb = jnp.concatenate([x_lo.astype(bf16), x_hi.astype(bf16)], axis=-1)
        act = _silu(_dot(xb, wgb[...])) * _dot(xb, wub[...])
        o_ref[...] = _pack_halves(_dot(act.astype(bf16), wdb[...]))

    @pl.when(i >= n_used)
    def _():
        o_ref[...] = jnp.zeros_like(o_ref)


def _expert_mlp(block_e, n_used, xs, wg, wu, wd, layer, nb):
    last = lambda i, nu: jnp.minimum(i, nu[0] - 1)
    wspec = lambda shape: pl.BlockSpec(shape, lambda i, be, nu: (layer, be[last(i, nu)], 0, 0))
    grid_spec = pltpu.PrefetchScalarGridSpec(
        num_scalar_prefetch=2,
        grid=(nb,),
        in_specs=[
            pl.BlockSpec((BLOCK_TILE_ROWS, HALF), lambda i, be, nu: (last(i, nu), 0)),
            wspec((1, 1, D_MODEL, D_EXPERT)), wspec((1, 1, D_MODEL, D_EXPERT)), wspec((1, 1, D_EXPERT, D_MODEL)),
        ],
        out_specs=pl.BlockSpec((BLOCK_TILE_ROWS, HALF), lambda i, be, nu: (i, 0)),
        scratch_shapes=[pltpu.VMEM((D_MODEL, D_EXPERT), bf16), pltpu.VMEM((D_MODEL, D_EXPERT), bf16),
                        pltpu.VMEM((D_EXPERT, D_MODEL), bf16)],
    )
    return pl.pallas_call(
        _expert_kernel,
        grid_spec=grid_spec,
        out_shape=jax.ShapeDtypeStruct((nb * BLOCK_TILE_ROWS, HALF), jnp.uint32),
        compiler_params=_cparams(1),
        name="expert_mlp",
    )(block_e, n_used, xs, wg, wu, wd)


COMB_TM = 64


def _combine_kernel(idx_ref, idxn_ref, wts_ref, h_ref, hb_ref, sg_ref, su_ref, sd_ref, g_ref, b_ref, ys_hbm,
                    hf_ref, hb_out_ref, buf, sem):
    i = pl.program_id(0)
    nb = pl.num_programs(0)
    slot = i % 2
    cnt = COMB_TM * TOP_K

    @pl.when(i == 0)
    def _():
        _start_gather(ys_hbm, idx_ref, buf, sem, 0, cnt)

    @pl.when(i + 1 < nb)
    def _():
        _start_gather(ys_hbm, idxn_ref, buf, sem, 1 - slot, cnt)

    pltpu.make_async_copy(buf.at[slot], buf.at[slot], sem.at[slot]).wait()
    wts = wts_ref[...]
    groups = COMB_TM // GATHER_UNROLL
    routed = None
    for k in range(TOP_K):
        y_lo, y_hi = _unpack_halves(buf[slot, pl.ds(k * groups, groups)].reshape(COMB_TM, HALF))
        term = jnp.concatenate([y_lo, y_hi], axis=-1) * wts[:, k:k + 1]
        routed = term if routed is None else routed + term
    hb = hb_ref[...]
    act = _silu(_dot(hb, sg_ref[...])) * _dot(hb, su_ref[...])
    shared = _dot(act.astype(bf16), sd_ref[...])
    hn = _layer_norm(ALPHA * h_ref[...] + (routed + shared), g_ref[...], b_ref[...])
    hf_ref[...] = hn
    hb_out_ref[...] = hn.astype(bf16)


def _combine(pos3, wts, h, h_bf, ws_gate, ws_up, ws_down, g, b, ys2d):
    n = h.shape[0]
    tm = COMB_TM
    nb = n // tm
    cnt = tm * TOP_K
    rowspec = pl.BlockSpec((tm, D_MODEL), lambda i: (i, 0))
    const2 = lambda i: (0, 0)
    return pl.pallas_call(
        _combine_kernel,
        grid=(nb,),
        in_specs=[
            pl.BlockSpec((1, 1, cnt), lambda i: (i, 0, 0), memory_space=pltpu.SMEM),
            pl.BlockSpec((1, 1, cnt), lambda i: (jnp.minimum(i + 1, nb - 1), 0, 0), memory_space=pltpu.SMEM),
            pl.BlockSpec((tm, TOP_K), lambda i: (i, 0)),
            rowspec, rowspec,
            pl.BlockSpec((D_MODEL, D_SHARED), const2),
            pl.BlockSpec((D_MODEL, D_SHARED), const2),
            pl.BlockSpec((D_SHARED, D_MODEL), const2),
            pl.BlockSpec((1, D_MODEL), const2),
            pl.BlockSpec((1, D_MODEL), const2),
            pl.BlockSpec(memory_space=pl.ANY),
        ],
        out_specs=[rowspec, rowspec],
        out_shape=[jax.ShapeDtypeStruct((n, D_MODEL), f32),
                   jax.ShapeDtypeStruct((n, D_MODEL), bf16)],
        scratch_shapes=[pltpu.VMEM((2, cnt // GATHER_UNROLL, GATHER_UNROLL, HALF), jnp.uint32),
                        pltpu.SemaphoreType.DMA((2,))],
        compiler_params=_cparams(1),
        name="moe_combine",
    )(pos3, pos3, wts, h, h_bf, ws_gate, ws_up, ws_down, g, b, ys2d)


def _dispatch(eidx, rank, counts_f, nb):
    n = eidx.shape[0]
    counts = counts_f[0, :N_EXPERTS].astype(jnp.int32)
    padded = (counts + EXPERT_ROWS - 1) // EXPERT_ROWS * EXPERT_ROWS
    pend = jnp.cumsum(padded)
    pstart = pend - padded
    starts = jnp.arange(nb, dtype=jnp.int32) * EXPERT_ROWS
    block_e = jnp.minimum(jnp.sum((pend[None, :] <= starts[:, None]).astype(jnp.int32), axis=1), N_EXPERTS - 1)
    n_used = (pend[-1] // EXPERT_ROWS).astype(jnp.int32).reshape(1)
    onehot = eidx[:, :, None] == jnp.arange(N_EXPERTS, dtype=jnp.int32)[None, None, :]
    pos = jnp.sum(jnp.where(onehot, pstart[None, None, :], 0), axis=-1) + rank
    pos_tok3 = pos.reshape(n // SCATTER_TM, 1, SCATTER_TM * TOP_K)
    tm = COMB_TM
    pos_k3 = pos.reshape(n // tm, tm, TOP_K).transpose(0, 2, 1).reshape(n // tm, 1, tm * TOP_K)
    return block_e.astype(jnp.int32), n_used, pstart + counts, pend, pos_tok3, pos_k3


def _consts():
    r = jnp.arange(CHUNK)
    tri = (r[:, None] >= r[None, :]).astype(bf16)
    trit = (r[:, None] <= r[None, :]).astype(bf16)
    lane = jnp.arange(LANES)
    ch = jnp.arange(SSD_HEADS * SSD_HEADDIM)
    ssd_expand = (lane[:, None] == (ch[None, :] // SSD_HEADDIM)).astype(bf16)
    hc = jnp.arange(HGRN_HEADS * HGRN_DIM)
    hm_rows = jnp.arange(HGRN_HEADS * SUB)
    hmask = ((hm_rows[:, None] // SUB) == (hc[None, :] // HGRN_DIM)).astype(f32)
    qr = jnp.arange((HGRN_NBLK - 1) * HGRN_HEADS * SUB)
    qblk = qr // (HGRN_HEADS * SUB) + 1
    kstart = SUB * jnp.arange(HGRN_NBLK - 1) * jnp.arange(1, HGRN_NBLK) // 2
    kblk = jnp.sum(lane[None, :] >= kstart[:, None], axis=0)
    offmask = ((qblk[:, None] == kblk[None, :]) & (lane[None, :] < HGRN_OFF_KEYS)).astype(f32)
    rt = jnp.arange(ROUTER_TM)
    ltri = (rt[:, None] > rt[None, :]).astype(bf16)
    return tri, trit, ssd_expand, hmask, offmask, ltri


def _hgrn_lower_bounds(lb_param):
    sm = jax.nn.softmax(lb_param.astype(f32), axis=0)
    return jnp.cumsum(sm, axis=0) - sm[0:1]


def kernel(x, mem, w_in, conv_w, conv_b, dt_bias, a_log, d_skip, ssd_norm, pool_w, pool_scale, hgrn_lb, hgrn_norm, w_mem_kv, w_branch, w_o, ln1_g, ln1_b, w_router, router_bias, w_exp_gate, w_exp_up, w_exp_down, w_sh_gate, w_sh_up, w_sh_down, ln2_g, ln2_b):
    bsz, seq, d = x.shape
    n = bsz * seq
    nk = n * TOP_K
    nb = -(-nk // EXPERT_ROWS) + N_EXPERTS
    tri, trit, ssd_expand, hmask, offmask, ltri = _consts()
    lb_all = _hgrn_lower_bounds(hgrn_lb)

    c_dt0 = BRANCH_WIDTH * 2 + SSD_CONV_DIM
    c_dt1 = c_dt0 + SSD_HEADS
    c_gate = c_dt1 + 5 * BRANCH_WIDTH
    pad_lanes = LANES - SSD_HEADS

    h = x.reshape(n, d)
    h_bf = h.astype(bf16)
    mem_bf = mem.reshape(bsz * MEM_TOKENS, d).astype(bf16)

    w_packed = jnp.concatenate(
        [w_in[:, :, :c_dt0], w_in[:, :, c_dt1:], jnp.pad(w_in[:, :, c_dt0:c_dt1], ((0, 0), (0, 0), (0, pad_lanes)))],
        axis=2).astype(bf16)
    w_kv_bf = w_mem_kv.astype(bf16)
    n_main = c_gate - SSD_HEADS
    c_dtp = n_main + N_BRANCH * d

    for l in range(DEPTH):
        proj = _matmul(h_bf, w_packed, l, f32, 512, 1024, 0, n_main)
        dtp = _matmul(h_bf, w_packed, l, f32, 512, LANES, c_dtp, LANES)
        kv = _matmul(mem_bf, w_kv_bf, l, bf16, 512, 1024)

        br_pool = _pool_mixer(proj, pool_w[l].astype(bf16), pool_scale[l].reshape(1, -1), bsz, seq)
        br_ssd = _ssd_mixer(
            proj, dtp, conv_w[l], conv_b[l].reshape(1, -1),
            jnp.pad(dt_bias[l], (0, pad_lanes)).reshape(1, -1),
            jnp.pad(a_log[l], (0, pad_lanes)).reshape(1, -1),
            jnp.repeat(d_skip[l], SSD_HEADDIM).reshape(1, -1),
            ssd_norm[l].reshape(1, -1), (tri, trit, ssd_expand), bsz, seq)
        br_hgrn = _hgrn_mixer(proj, lb_all[l].reshape(1, -1), hgrn_norm[l].reshape(1, -1),
                              (tri, hmask, offmask), bsz, seq)
        br_mem = _mem_attention(proj, kv, bsz, seq)

        merged = _merge(h_bf, (br_pool, br_ssd, br_hgrn, br_mem), w_packed, l, n_main, w_branch[l].astype(bf16))
        h, h_bf, h_pk = _wo_ln(merged, w_o[l].astype(bf16), h, ln1_g[l].reshape(1, -1), ln1_b[l].reshape(1, -1))

        eidx, wts, rank, counts = _router(h, jnp.pad(w_router[l], ((0, 0), (0, LANES - N_EXPERTS))),
                                          jnp.pad(router_bias[l], (0, LANES - N_EXPERTS)).reshape(1, -1), ltri)
        block_e, n_used, vend, pend, pos_tok3, pos_k3 = _dispatch(eidx, rank, counts, nb)
        xs = _scatter_rows(vend, pend, pos_tok3, h_pk, nb)
        ys = _expert_mlp(block_e, n_used, xs, w_exp_gate, w_exp_up, w_exp_down, l, nb)
        h, h_bf = _combine(pos_k3, wts, h, h_bf, w_sh_gate[l].astype(bf16), w_sh_up[l].astype(bf16),
                                   w_sh_down[l].astype(bf16), ln2_g[l].reshape(1, -1), ln2_b[l].reshape(1, -1), ys)
    return h.reshape(bsz, seq, d)
```

```python
import functools

import jax
import jax.numpy as jnp
from jax import lax
from jax.experimental import pallas as pl
from jax.experimental.pallas import tpu as pltpu

f32 = jnp.float32
bf16 = jnp.bfloat16

D_MODEL = 2048
DEPTH = 4
CHUNK = 64
N_BRANCH = 4
BRANCH_WIDTH = 1024
POOL_WINDOWS = (2, 4, 8, 16)
POOL_GROUP = 256
SSD_HEADS = 16
SSD_HEADDIM = 64
SSD_STATE = 128
SSD_GROUPS = 4
SSD_CONV = 4
SSD_CONV_DIM = 2048
HGRN_HEADS = 8
HGRN_DIM = 128
MEM_TOKENS = 256
MEM_HEADS = 4
MEM_HEAD_DIM = 256
N_EXPERTS = 64
TOP_K = 8
N_GROUPS = 8
TOPK_GROUPS = 4
D_EXPERT = 256
D_SHARED = 256
ROUTED_SCALE = 2.5
MASK_SCORE = -1e4
ALPHA = (2 * DEPTH) ** 0.25
EPS = 1e-5

LANES = 128
ROW_TILES = 1
EXPERT_ROWS = 512
ZERO_ROWS = 128
SUB = 16
VMEM_LIMIT = 56 * 1024 * 1024

NEG_INF = float("-inf")


def _cparams(n_axes):
    return pltpu.CompilerParams(dimension_semantics=("arbitrary",) * n_axes,
                                vmem_limit_bytes=VMEM_LIMIT)


def _sigmoid(x):
    return jax.nn.sigmoid(x)


def _silu(x):
    return x * jax.nn.sigmoid(x)


def _split3(x):
    hi = x.astype(bf16)
    r1 = x - hi.astype(f32)
    mid = r1.astype(bf16)
    lo = (r1 - mid.astype(f32)).astype(bf16)
    return hi, mid, lo


def _dot(a, b):
    return jnp.dot(a, b, preferred_element_type=f32)


def _dot_nt(a, b):
    return lax.dot_general(a, b, (((1,), (1,)), ((), ())), preferred_element_type=f32)


def _dot_tn(a, b):
    return lax.dot_general(a, b, (((0,), (0,)), ((), ())), preferred_element_type=f32)


def _sel_left(sel01, x):
    return sum(_dot(sel01, p) for p in _split3(x))


def _sel_right(x, sel01):
    return sum(_dot(p, sel01) for p in _split3(x))


def _mm_kernel(x_ref, w_ref, o_ref):
    o_ref[...] = _dot(x_ref[...], w_ref[0]).astype(o_ref.dtype)


def _matmul(x, w, layer, out_dtype, tm, tn, col0=0, ncols=None):
    m, k = x.shape
    n = w.shape[2] if ncols is None else ncols
    cb = col0 // tn
    return pl.pallas_call(
        _mm_kernel,
        grid=(n // tn, m // tm),
        in_specs=[pl.BlockSpec((tm, k), lambda j, i: (i, 0)),
                  pl.BlockSpec((1, k, tn), lambda j, i: (layer, 0, cb + j))],
        out_specs=pl.BlockSpec((tm, tn), lambda j, i: (i, j)),
        out_shape=jax.ShapeDtypeStruct((m, n), out_dtype),
        compiler_params=_cparams(2),
        name="matmul",
    )(x, w)


POOL_T = 512
POOL_HALO = 16


def _pool_kernel(u_ref, halo_ref, w_ref, scale_ref, o_ref):
    i = pl.program_id(1)
    u = u_ref[...]
    halo = jnp.where(i > 0, halo_ref[...], 0.0)
    ext = jnp.concatenate([halo, u], axis=0)
    t = i * POOL_T + lax.broadcasted_iota(jnp.int32, (POOL_T, 1), 0)
    outs = []
    for gi, w in enumerate(POOL_WINDOWS):
        xg = ext[:, gi * POOL_GROUP:(gi + 1) * POOL_GROUP]
        s = xg
        span = 1
        while span < w:
            s = s[span:, :] + s[:-span, :]
            span *= 2
        s = s[s.shape[0] - POOL_T:, :]
        cnt = jnp.minimum(t + 1, w).astype(f32)
        d = s / cnt - u[:, gi * POOL_GROUP:(gi + 1) * POOL_GROUP]
        outs.append(_dot(d.astype(bf16), w_ref[gi]))
    y = jnp.concatenate(outs, axis=-1) * scale_ref[...]
    o_ref[...] = y.astype(o_ref.dtype)


def _pool_mixer(proj, pool_w, pool_scale, bsz, seq):
    n = bsz * seq
    nt = seq // POOL_T
    return pl.pallas_call(
        _pool_kernel,
        grid=(bsz, nt),
        in_specs=[
            pl.BlockSpec((POOL_T, BRANCH_WIDTH), lambda b, i: (b * nt + i, 0)),
            pl.BlockSpec((POOL_HALO, BRANCH_WIDTH),
                         lambda b, i: (jnp.maximum((b * nt + i) * (POOL_T // POOL_HALO) - 1, 0), 0)),
            pl.BlockSpec((len(POOL_WINDOWS), POOL_GROUP, POOL_GROUP), lambda b, i: (0, 0, 0)),
            pl.BlockSpec((1, BRANCH_WIDTH), lambda b, i: (0, 0)),
        ],
        out_specs=pl.BlockSpec((POOL_T, BRANCH_WIDTH), lambda b, i: (b * nt + i, 0)),
        out_shape=jax.ShapeDtypeStruct((n, BRANCH_WIDTH), bf16),
        compiler_params=_cparams(2),
        name="pool_mixer",
    )(proj, proj, pool_w, pool_scale)


CONV_TAIL = 8


def _softplus(x):
    return jnp.maximum(x, 0.0) + jnp.log1p(jnp.exp(-jnp.abs(x)))


def _ssd_kernel(z_ref, xbc_ref, dt_ref, cw_ref, cb_ref, dtb_ref, alog_ref, dskip_ref, nw_ref,
                tri_ref, trit_ref, exp_ref, o_ref, state_ref, tail_ref):
    i = pl.program_id(0)

    @pl.when(i == 0)
    def _():
        state_ref[...] = jnp.zeros_like(state_ref)
        tail_ref[...] = jnp.zeros_like(tail_ref)

    for bi in range(z_ref.shape[0]):
        _ssd_chunk(z_ref.at[bi], xbc_ref.at[bi], dt_ref.at[bi], cw_ref, cb_ref, dtb_ref, alog_ref, dskip_ref, nw_ref,
                   tri_ref, trit_ref, exp_ref, o_ref.at[bi], state_ref.at[bi], tail_ref.at[bi])


def _ssd_chunk(z_ref, xbc_ref, dt_ref, cw_ref, cb_ref, dtb_ref, alog_ref, dskip_ref, nw_ref,
               tri_ref, trit_ref, exp_ref, o_ref, state_ref, tail_ref):
    xbc = xbc_ref[...]
    ext = jnp.concatenate([tail_ref[...], xbc], axis=0)
    cw = cw_ref[...]
    conv = cb_ref[...]
    for k in range(SSD_CONV):
        off = CONV_TAIL - (SSD_CONV - 1) + k
        conv = conv + cw[k:k + 1, :] * ext[off:off + CHUNK, :]
    tail_ref[...] = xbc[CHUNK - CONV_TAIL:, :]
    act = _silu(conv)
    inner = SSD_HEADS * SSD_HEADDIM
    gw = SSD_GROUPS * SSD_STATE
    xs = act[:, :inner]
    bmat = act[:, inner:inner + gw]
    cmat = act[:, inner + gw:]

    dt = _softplus(dt_ref[...] + dtb_ref[...])
    a = -jnp.exp(alog_ref[...])
    ad = dt * a
    ad_parts = _split3(ad)
    tri = tri_ref[...]
    acs = sum(_dot(tri, p) for p in ad_parts)
    acs_t = sum(_dot_tn(p, trit_ref[...]) for p in ad_parts)
    a_last = acs[CHUNK - 1:CHUNK, :]
    expand = exp_ref[...]
    dt_e = _sel_right(dt, expand)
    eacs_e = _sel_right(jnp.exp(acs), expand)
    dec_e = _sel_right(jnp.exp(a_last - acs), expand)
    x = xs * dt_e
    xb = x.astype(bf16)
    xd = (x * dec_e).astype(bf16)
    elast_e = eacs_e[CHUNK - 1:CHUNK, :]

    row = lax.broadcasted_iota(jnp.int32, (CHUNK, CHUNK), 0)
    col = lax.broadcasted_iota(jnp.int32, (CHUNK, CHUNK), 1)
    causal = row >= col
    hpg = SSD_HEADS // SSD_GROUPS
    gch = hpg * SSD_HEADDIM
    y_diag = []
    y_off = []
    for g in range(SSD_GROUPS):
        bg = bmat[:, g * SSD_STATE:(g + 1) * SSD_STATE].astype(bf16)
        cg = cmat[:, g * SSD_STATE:(g + 1) * SSD_STATE].astype(bf16)
        cb = _dot_nt(cg, bg)
        for hh in range(hpg):
            h = g * hpg + hh
            diff = acs[:, h:h + 1] - acs_t[h:h + 1, :]
            decay = jnp.where(causal, jnp.exp(jnp.where(causal, diff, 0.0)), 0.0)
            sc = (cb * decay).astype(bf16)
            y_diag.append(_dot(sc, xb[:, h * SSD_HEADDIM:(h + 1) * SSD_HEADDIM]))
        st = state_ref[:, g * gch:(g + 1) * gch]
        y_off.append(_dot(cg, st.astype(bf16)))
        upd = _dot_tn(bg, xd[:, g * gch:(g + 1) * gch])
        state_ref[:, g * gch:(g + 1) * gch] = st * elast_e[:, g * gch:(g + 1) * gch] + upd
    y = (jnp.concatenate(y_diag, axis=-1) + jnp.concatenate(y_off, axis=-1) * eacs_e
         + xs * dskip_ref[...])
    y = y * _silu(z_ref[...])
    nw = nw_ref[...]
    ngw = inner // SSD_GROUPS
    outs = []
    for g in range(SSD_GROUPS):
        yg = y[:, g * ngw:(g + 1) * ngw]
        ms = jnp.mean(yg * yg, axis=-1, keepdims=True)
        outs.append(yg * lax.rsqrt(ms + EPS) * nw[:, g * ngw:(g + 1) * ngw])
    o_ref[...] = jnp.concatenate(outs, axis=-1).astype(o_ref.dtype)


def _ssd_mixer(proj, dtp, conv_w, conv_b, dt_bias, a_log, d_skip_e, norm_w, consts, bsz, seq):
    n = bsz * seq
    nc = seq // CHUNK
    tri, trit, expand = consts
    const2 = lambda i: (0, 0)
    inner = SSD_HEADS * SSD_HEADDIM
    proj3 = proj.reshape(bsz, seq, proj.shape[1])
    dtp3 = dtp.reshape(bsz, seq, LANES)
    out = pl.pallas_call(
        _ssd_kernel,
        grid=(nc,),
        in_specs=[
            pl.BlockSpec((bsz, CHUNK, inner), lambda i: (0, i, 1)),
            pl.BlockSpec((bsz, CHUNK, SSD_CONV_DIM), lambda i: (0, i, 1)),
            pl.BlockSpec((bsz, CHUNK, LANES), lambda i: (0, i, 0)),
            pl.BlockSpec((SSD_CONV, SSD_CONV_DIM), const2),
            pl.BlockSpec((1, SSD_CONV_DIM), const2),
            pl.BlockSpec((1, LANES), const2),
            pl.BlockSpec((1, LANES), const2),
            pl.BlockSpec((1, inner), const2),
            pl.BlockSpec((1, inner), const2),
            pl.BlockSpec((CHUNK, CHUNK), const2),
            pl.BlockSpec((CHUNK, CHUNK), const2),
            pl.BlockSpec((LANES, inner), const2),
        ],
        out_specs=pl.BlockSpec((bsz, CHUNK, inner), lambda i: (0, i, 0)),
        out_shape=jax.ShapeDtypeStruct((bsz, seq, inner), bf16),
        scratch_shapes=[pltpu.VMEM((bsz, SSD_STATE, inner), f32),
                        pltpu.VMEM((bsz, CONV_TAIL, SSD_CONV_DIM), f32)],
        compiler_params=_cparams(1),
        name="ssd_mixer",
    )(proj3, proj3, dtp3, conv_w, conv_b, dt_bias, a_log, d_skip_e, norm_w, tri, trit, expand)
    return out.reshape(n, inner)


LOG2E = 1.4426950408889634
HGRN_NBLK = CHUNK // SUB
HGRN_OFF_KEYS = SUB * (HGRN_NBLK - 1) * HGRN_NBLK // 2


def _hgrn_chunk(q, f, v, g, lb, nw, tri, hmask, offmask, state_ref):
    width = HGRN_HEADS * HGRN_DIM
    qf = _silu(q)
    kf = (1.0 - lb) * _sigmoid(-f)
    b2 = _sel_left(tri, jnp.log1p(-kf)) * LOG2E
    c2 = b2 - jnp.log2(kf)
    b_last = b2[CHUNK - 1:CHUNK, :]
    vb = v.astype(bf16)

    qd = (qf * jnp.exp2(b2)).astype(bf16)
    kd = jnp.exp2(b_last - c2).astype(bf16)
    eb_last = jnp.exp2(b_last)
    o_inter = []
    for h in range(HGRN_HEADS):
        sl = slice(h * HGRN_DIM, (h + 1) * HGRN_DIM)
        st = state_ref[h]
        o_inter.append(_dot_nt(qd[:, sl], st.astype(bf16)))
        state_ref[h] = st * eb_last[:, sl] + _dot_tn(vb[:, sl], kd[:, sl])
    o = jnp.concatenate(o_inter, axis=-1)

    qm, kk, vv = [], [], []
    for blk in range(1, HGRN_NBLK):
        r0 = blk * SUB
        anchor = b2[r0 - 1:r0, :]
        qq = qf[r0:r0 + SUB, :] * jnp.exp2(b2[r0:r0 + SUB, :] - anchor)
        qm.append((jnp.concatenate([qq] * HGRN_HEADS, axis=0) * hmask).astype(bf16))
        kk.append(jnp.exp2(anchor - c2[:r0, :]).astype(bf16))
        vv.append(vb[:r0, :])
    zpad = jnp.zeros((LANES - HGRN_OFF_KEYS, width), bf16)
    att = _dot_nt(jnp.concatenate(qm, axis=0), jnp.concatenate(kk + [zpad], axis=0))
    res = _dot((att * offmask).astype(bf16), jnp.concatenate(vv + [zpad], axis=0))
    rows = [jnp.zeros((SUB, width), f32)]
    for blk in range(1, HGRN_NBLK):
        base = (blk - 1) * HGRN_HEADS * SUB
        acc = None
        for h in range(HGRN_HEADS):
            part = res[base + h * SUB:base + (h + 1) * SUB, :] * hmask[h * SUB:(h + 1) * SUB, :]
            acc = part if acc is None else acc + part
        rows.append(acc)
    o = o + jnp.concatenate(rows, axis=0)

    tpos = lax.broadcasted_iota(jnp.int32, (CHUNK, 1), 0) & (SUB - 1)
    diag = [None] * HGRN_HEADS
    for d in range(SUB):
        cs = c2 if d == 0 else pltpu.roll(c2, d, 0)
        vs = v if d == 0 else pltpu.roll(v, d, 0)
        p = qf * jnp.exp2(b2 - cs)
        same_block = tpos >= d
        for h in range(HGRN_HEADS):
            sl = slice(h * HGRN_DIM, (h + 1) * HGRN_DIM)
            att = jnp.where(same_block, jnp.sum(p[:, sl], axis=-1, keepdims=True), 0.0)
            term = att * vs[:, sl]
            diag[h] = term if diag[h] is None else diag[h] + term
    o = o + jnp.concatenate(diag, axis=-1)

    gate = _silu(g)
    outs = []
    for h in range(HGRN_HEADS):
        sl = slice(h * HGRN_DIM, (h + 1) * HGRN_DIM)
        oh = o[:, sl]
        ms = jnp.mean(oh * oh, axis=-1, keepdims=True)
        outs.append(oh * lax.rsqrt(ms + EPS) * nw * gate[:, sl])
    return jnp.concatenate(outs, axis=-1)


def _hgrn_kernel(q_ref, f_ref, i_ref, g_ref, lb_ref, nw_ref, tri_ref, hmask_ref, offmask_ref, o_ref, state_ref):
    c = pl.program_id(0)

    @pl.when(c == 0)
    def _():
        state_ref[...] = jnp.zeros_like(state_ref)

    for bi in range(q_ref.shape[0]):
        out = _hgrn_chunk(q_ref[bi], f_ref[bi], i_ref[bi], g_ref[bi], lb_ref[...], nw_ref[...], tri_ref[...],
                          hmask_ref[...], offmask_ref[...], state_ref.at[bi])
        o_ref[bi] = out.astype(o_ref.dtype)


def _hgrn_mixer(proj, lb, norm_w, consts, bsz, seq):
    n = bsz * seq
    nc = seq // CHUNK
    tri, hmask, offmask = consts
    width = HGRN_HEADS * HGRN_DIM
    const2 = lambda i: (0, 0)
    col = lambda j: (lambda i: (0, i, j))
    proj3 = proj.reshape(bsz, seq, proj.shape[1])
    qrows = (HGRN_NBLK - 1) * HGRN_HEADS * SUB
    out = pl.pallas_call(
        _hgrn_kernel,
        grid=(nc,),
        in_specs=[
            pl.BlockSpec((bsz, CHUNK, width), col(0)),
            pl.BlockSpec((bsz, CHUNK, width), col(1)),
            pl.BlockSpec((bsz, CHUNK, width), col(2)),
            pl.BlockSpec((bsz, CHUNK, width), col(3)),
            pl.BlockSpec((1, width), const2),
            pl.BlockSpec((1, HGRN_DIM), const2),
            pl.BlockSpec((CHUNK, CHUNK), const2),
            pl.BlockSpec((HGRN_HEADS * SUB, width), const2),
            pl.BlockSpec((qrows, LANES), const2),
        ],
        out_specs=pl.BlockSpec((bsz, CHUNK, width), lambda i: (0, i, 0)),
        out_shape=jax.ShapeDtypeStruct((bsz, seq, width), bf16),
        scratch_shapes=[pltpu.VMEM((bsz, HGRN_HEADS, HGRN_DIM, HGRN_DIM), f32)],
        compiler_params=_cparams(1),
        name="hgrn_mixer",
    )(proj3, proj3, proj3, proj3, lb, norm_w, tri, hmask, offmask)
    return out.reshape(n, width)


MEM_T = 512


def _memattn_kernel(q_ref, k_ref, v_ref, o_ref):
    q = q_ref[...].astype(bf16)
    outs = []
    for h in range(MEM_HEADS):
        sl = slice(h * MEM_HEAD_DIM, (h + 1) * MEM_HEAD_DIM)
        s = _dot_nt(q[:, sl], k_ref[:, sl]) * (MEM_HEAD_DIM ** -0.5)
        m = jnp.max(s, axis=-1, keepdims=True)
        e = jnp.exp(s - m)
        p = e / jnp.sum(e, axis=-1, keepdims=True)
        outs.append(_dot(p.astype(bf16), v_ref[:, sl]))
    o_ref[...] = jnp.concatenate(outs, axis=-1).astype(o_ref.dtype)


def _mem_attention(proj, kv, bsz, seq):
    n = bsz * seq
    nt = seq // MEM_T
    width = MEM_HEADS * MEM_HEAD_DIM
    return pl.pallas_call(
        _memattn_kernel,
        grid=(bsz, nt),
        in_specs=[
            pl.BlockSpec((MEM_T, width), lambda b, i: (b * nt + i, 4)),
            pl.BlockSpec((MEM_TOKENS, width), lambda b, i: (b, 0)),
            pl.BlockSpec((MEM_TOKENS, width), lambda b, i: (b, 1)),
        ],
        out_specs=pl.BlockSpec((MEM_T, width), lambda b, i: (b * nt + i, 0)),
        out_shape=jax.ShapeDtypeStruct((n, width), bf16),
        compiler_params=_cparams(2),
        name="mem_attention",
    )(proj, kv, kv)


MERGE_TM = 512
MERGE_TN = 512


def _merge_kernel(h_ref, b0_ref, b1_ref, b2_ref, b3_ref, g0_ref, g1_ref, g2_ref, g3_ref, wb_ref, o_ref):
    h = h_ref[...]
    acc = None
    for n, (br, wg) in enumerate(zip((b0_ref, b1_ref, b2_ref, b3_ref), (g0_ref, g1_ref, g2_ref, g3_ref))):
        gate = _sigmoid(_dot(h, wg[0]))
        term = gate * _dot(br[...], wb_ref[n])
        acc = term if acc is None else acc + term
    o_ref[...] = acc.astype(o_ref.dtype)


def _merge(h_bf, branches, w_packed, layer, gate_col0, w_branch):
    n = h_bf.shape[0]
    tm, tn = MERGE_TM, MERGE_TN
    bspec = pl.BlockSpec((tm, BRANCH_WIDTH), lambda j, i: (i, 0))
    gspec = lambda nbr: pl.BlockSpec((1, D_MODEL, tn),
                                     lambda j, i: (layer, 0, (gate_col0 + nbr * D_MODEL) // tn + j))
    return pl.pallas_call(
        _merge_kernel,
        grid=(D_MODEL // tn, n // tm),
        in_specs=[pl.BlockSpec((tm, D_MODEL), lambda j, i: (i, 0)), bspec, bspec, bspec, bspec,
                  gspec(0), gspec(1), gspec(2), gspec(3),
                  pl.BlockSpec((N_BRANCH, BRANCH_WIDTH, tn), lambda j, i: (0, 0, j))],
        out_specs=pl.BlockSpec((tm, tn), lambda j, i: (i, j)),
        out_shape=jax.ShapeDtypeStruct((n, D_MODEL), bf16),
        compiler_params=_cparams(2),
        name="gated_merge",
    )(h_bf, *branches, w_packed, w_packed, w_packed, w_packed, w_branch)


def _layer_norm(x, g, b):
    mu = jnp.mean(x, axis=-1, keepdims=True)
    xc = x - mu
    var = jnp.mean(xc * xc, axis=-1, keepdims=True)
    return xc * lax.rsqrt(var + EPS) * g + b


WO_TM = 256


HALF = D_MODEL // 2
U16_HI = 0xFFFF0000


def _pack_halves(x):
    lo = lax.bitcast_convert_type(x[:, :HALF].astype(bf16).astype(f32), jnp.uint32)
    hi = lax.bitcast_convert_type(x[:, HALF:].astype(bf16).astype(f32), jnp.uint32)
    return lax.shift_right_logical(lo, jnp.uint32(16)) | (hi & jnp.uint32(U16_HI))


def _unpack_halves(w):
    lo = lax.bitcast_convert_type(lax.shift_left(w, jnp.uint32(16)), f32)
    hi = lax.bitcast_convert_type(w & jnp.uint32(U16_HI), f32)
    return lo, hi


def _wo_ln_kernel(m_ref, w_ref, h_ref, g_ref, b_ref, hf_ref, hb_ref, hp_ref):
    y = ALPHA * h_ref[...] + _dot(m_ref[...], w_ref[...])
    hn = _layer_norm(y, g_ref[...], b_ref[...])
    hf_ref[...] = hn
    hb_ref[...] = hn.astype(bf16)
    hp_ref[...] = _pack_halves(hn)


def _wo_ln(merged, w_o, h, g, b):
    n = h.shape[0]
    tm = WO_TM
    rowspec = pl.BlockSpec((tm, D_MODEL), lambda i: (i, 0))
    const2 = lambda i: (0, 0)
    return pl.pallas_call(
        _wo_ln_kernel,
        grid=(n // tm,),
        in_specs=[rowspec, pl.BlockSpec((D_MODEL, D_MODEL), const2), rowspec,
                  pl.BlockSpec((1, D_MODEL), const2), pl.BlockSpec((1, D_MODEL), const2)],
        out_specs=[rowspec, rowspec, pl.BlockSpec((tm, HALF), lambda i: (i, 0))],
        out_shape=[jax.ShapeDtypeStruct((n, D_MODEL), f32),
                   jax.ShapeDtypeStruct((n, D_MODEL), bf16),
                   jax.ShapeDtypeStruct((n, HALF), jnp.uint32)],
        compiler_params=_cparams(1),
        name="wo_layernorm",
    )(merged, w_o, h, g, b)


ROUTER_TM = 256


def _router_kernel(h_ref, w_ref, bias_ref, ltri_ref, eidx_ref, wts_ref, rank_ref, counts_ref, cnt_ref):
    step = pl.program_id(0)

    @pl.when(step == 0)
    def _():
        cnt_ref[...] = jnp.zeros_like(cnt_ref)

    h = h_ref[...]
    w = w_ref[...]
    h_hi = h.astype(bf16)
    h_lo = (h - h_hi.astype(f32)).astype(bf16)
    w_hi = w.astype(bf16)
    w_lo = (w - w_hi.astype(f32)).astype(bf16)
    logits = _dot(h_hi, w_hi) + _dot(h_hi, w_lo) + _dot(h_lo, w_hi)
    scores = _sigmoid(logits)
    lane_i = lax.broadcasted_iota(jnp.int32, logits.shape, 1)
    lane = lane_i.astype(f32)
    real = lane_i < N_EXPERTS
    choice = jnp.where(real, scores + bias_ref[...], NEG_INF)
    gsize = N_EXPERTS // N_GROUPS
    grp = lax.shift_right_logical(lane_i, gsize.bit_length() - 1)
    big = float(LANES)

    def first_argmax(x):
        m = jnp.max(x, axis=-1, keepdims=True)
        idx = jnp.min(jnp.where(x == m, lane, big), axis=-1, keepdims=True)
        return m, idx

    gscore = []
    for g in range(N_GROUPS):
        cg = jnp.where(grp == g, choice, NEG_INF)
        m1, i1 = first_argmax(cg)
        m2 = jnp.max(jnp.where(lane == i1, NEG_INF, cg), axis=-1, keepdims=True)
        gscore.append(m1 + m2)
    keep = jnp.zeros(logits.shape, f32)
    for g in range(N_GROUPS):
        rank = jnp.zeros_like(gscore[g])
        for o in range(N_GROUPS):
            if o == g:
                continue
            ahead = (gscore[o] > gscore[g]) if o > g else (gscore[o] >= gscore[g])
            rank = rank + jnp.where(ahead, 1.0, 0.0)
        keep = jnp.where((grp == g) & (rank < TOPK_GROUPS), 1.0, keep)
    cur = jnp.where(real, jnp.where(keep > 0.0, choice, MASK_SCORE), NEG_INF)
    idxs, ws, hits = [], [], []
    sel = jnp.zeros(logits.shape, f32)
    for _ in range(TOP_K):
        _, ik = first_argmax(cur)
        hit = lane == ik
        ws.append(jnp.sum(jnp.where(hit, scores, 0.0), axis=-1, keepdims=True))
        idxs.append(ik)
        hits.append(hit)
        sel = jnp.where(hit, 1.0, sel)
        cur = jnp.where(hit, NEG_INF, cur)
    wsum = ws[0]
    for wk in ws[1:]:
        wsum = wsum + wk
    eidx_ref[...] = jnp.concatenate(idxs, axis=-1).astype(jnp.int32)
    wts_ref[...] = jnp.concatenate([wk / wsum * ROUTED_SCALE for wk in ws], axis=-1)
    before = _dot(ltri_ref[...], sel.astype(bf16)) + cnt_ref[...]
    ranks = [jnp.sum(jnp.where(hit, before, 0.0), axis=-1, keepdims=True) for hit in hits]
    rank_ref[...] = jnp.concatenate(ranks, axis=-1).astype(jnp.int32)
    cnt_ref[...] = cnt_ref[...] + jnp.sum(sel, axis=0, keepdims=True)
    counts_ref[...] = cnt_ref[...]


def _router(h, w_router_p, bias_p, ltri):
    n = h.shape[0]
    tm = ROUTER_TM
    const2 = lambda i: (0, 0)
    kspec = pl.BlockSpec((tm, TOP_K), lambda i: (i, 0))
    return pl.pallas_call(
        _router_kernel,
        grid=(n // tm,),
        in_specs=[pl.BlockSpec((tm, D_MODEL), lambda i: (i, 0)),
                  pl.BlockSpec((D_MODEL, LANES), const2),
                  pl.BlockSpec((1, LANES), const2),
                  pl.BlockSpec((tm, tm), const2)],
        out_specs=[kspec, kspec, kspec, pl.BlockSpec((1, LANES), const2)],
        out_shape=[jax.ShapeDtypeStruct((n, TOP_K), jnp.int32),
                   jax.ShapeDtypeStruct((n, TOP_K), f32),
                   jax.ShapeDtypeStruct((n, TOP_K), jnp.int32),
                   jax.ShapeDtypeStruct((1, LANES), f32)],
        scratch_shapes=[pltpu.VMEM((1, LANES), f32)],
        compiler_params=_cparams(1),
        name="router",
    )(h, w_router_p, bias_p, ltri)


GATHER_UNROLL = 8


def _start_gather(src_hbm, idx_ref, buf, sem, slot, count):
    def body(o, carry):
        for u in range(GATHER_UNROLL):
            pltpu.make_async_copy(src_hbm.at[pl.ds(idx_ref[0, 0, o * GATHER_UNROLL + u], 1)],
                                  buf.at[slot, o, pl.ds(u, 1)], sem.at[slot]).start()
        return carry
    lax.fori_loop(0, count // GATHER_UNROLL, body, 0)


SCATTER_TM = 128
SCATTER_SLOTS = 3
BLOCK_TILE_ROWS = EXPERT_ROWS * ROW_TILES


def _scatter_kernel(vend_ref, pend_ref, pos_ref, hrows_hbm, xs_hbm, stage, zbuf, lsem, ssem, zsem):
    i = pl.program_id(0)
    nsteps = pl.num_programs(0)
    cnt = SCATTER_TM * TOP_K

    @pl.when(i == 0)
    def _():
        zbuf[...] = jnp.zeros_like(zbuf)

        def piece(r0):
            return pltpu.make_async_copy(zbuf, xs_hbm.at[pl.ds(pl.multiple_of(r0, ZERO_ROWS), ZERO_ROWS)], zsem)

        def span(e):
            first = lax.shift_right_logical(vend_ref[e], ZERO_ROWS.bit_length() - 1)
            last = lax.shift_right_logical(pend_ref[e], ZERO_ROWS.bit_length() - 1)
            return first, last

        def start(e, carry):
            first, last = span(e)
            lax.fori_loop(first, last, lambda p, c: (piece(p * ZERO_ROWS).start(), c)[1], 0)
            return carry

        def wait(e, carry):
            first, last = span(e)
            lax.fori_loop(first, last, lambda p, c: (piece(p * ZERO_ROWS).wait(), c)[1], 0)
            return carry

        tail0 = lax.shift_right_logical(pend_ref[N_EXPERTS - 1], ZERO_ROWS.bit_length() - 1)
        tail1 = xs_hbm.shape[0] // ZERO_ROWS
        lax.fori_loop(0, N_EXPERTS, start, 0)
        lax.fori_loop(tail0, tail1, lambda p, c: (piece(p * ZERO_ROWS).start(), c)[1], 0)
        lax.fori_loop(0, N_EXPERTS, wait, 0)
        lax.fori_loop(tail0, tail1, lambda p, c: (piece(p * ZERO_ROWS).wait(), c)[1], 0)

    groups = SCATTER_TM // GATHER_UNROLL
    slot = lax.rem(i, SCATTER_SLOTS)
    nxt = lax.rem(i + 1, SCATTER_SLOTS)

    def load(j, s):
        src = pl.multiple_of(j * groups, groups)
        return pltpu.make_async_copy(hrows_hbm.at[pl.ds(src, groups)], stage.at[s], lsem.at[s])

    def scatter_wait(s):
        pltpu.make_async_copy(xs_hbm.at[pl.ds(0, cnt * ROW_TILES)], xs_hbm.at[pl.ds(0, cnt * ROW_TILES)],
                              ssem.at[s]).wait()

    @pl.when(i == 0)
    def _():
        load(0, 0).start()

    @pl.when(i >= SCATTER_SLOTS - 1)
    def _():
        scatter_wait(nxt)

    @pl.when(i + 1 < nsteps)
    def _():
        load(i + 1, nxt).start()

    load(i, slot).wait()

    def body(g, carry):
        for u in range(GATHER_UNROLL):
            for k in range(TOP_K):
                dst = pos_ref[0, 0, (g * GATHER_UNROLL + u) * TOP_K + k]
                pltpu.make_async_copy(stage.at[slot, g, pl.ds(u, 1)], xs_hbm.at[pl.ds(dst, 1)],
                                      ssem.at[slot]).start()
        return carry

    lax.fori_loop(0, groups, body, 0)

    @pl.when(i == nsteps - 1)
    def _():
        for back in range(SCATTER_SLOTS - 2, -1, -1):
            @pl.when(i >= back)
            def _():
                scatter_wait(lax.rem(i - back + SCATTER_SLOTS, SCATTER_SLOTS))


def _scatter_rows(vend, pend, pos_tok3, h, nb):
    n = h.shape[0]
    h_groups = h.reshape(n // GATHER_UNROLL, GATHER_UNROLL, HALF)
    grid_spec = pltpu.PrefetchScalarGridSpec(
        num_scalar_prefetch=2,
        grid=(n // SCATTER_TM,),
        in_specs=[
            pl.BlockSpec((1, 1, SCATTER_TM * TOP_K), lambda i, pa, pe: (i, 0, 0), memory_space=pltpu.SMEM),
            pl.BlockSpec(memory_space=pl.ANY),
        ],
        out_specs=pl.BlockSpec(memory_space=pl.ANY),
        scratch_shapes=[pltpu.VMEM((SCATTER_SLOTS, SCATTER_TM // GATHER_UNROLL, GATHER_UNROLL, HALF), jnp.uint32),
                        pltpu.VMEM((ZERO_ROWS, HALF), jnp.uint32),
                        pltpu.SemaphoreType.DMA((SCATTER_SLOTS,)),
                        pltpu.SemaphoreType.DMA((SCATTER_SLOTS,)),
                        pltpu.SemaphoreType.DMA(())],
    )
    return pl.pallas_call(
        _scatter_kernel,
        grid_spec=grid_spec,
        out_shape=jax.ShapeDtypeStruct((nb * BLOCK_TILE_ROWS, HALF), jnp.uint32),
        compiler_params=_cparams(1),
        name="dispatch_scatter",
    )(vend, pend, pos_tok3, h_groups)


def _expert_kernel(be_ref, nused_ref, x_ref, wg_ref, wu_ref, wd_ref, o_ref, wgb, wub, wdb):
    i = pl.program_id(0)
    n_used = nused_ref[0]

    @pl.when(i < n_used)
    def _():
        prev = be_ref[jnp.maximum(i - 1, 0)]

        @pl.when((i == 0) | (be_ref[i] != prev))
        def _():
            wgb[...] = wg_ref[0, 0].astype(bf16)
            wub[...] = wu_ref[0, 0].astype(bf16)
            wdb[...] = wd_ref[0, 0].astype(bf16)

        x_lo, x_hi = _unpack_halves(x_ref[...])
        xb = jnp.concatenate([x_lo.astype(bf16), x_hi.astype(bf16)], axis=-1)
        act = _silu(_dot(xb, wgb[...])) * _dot(xb, wub[...])
        o_ref[...] = _pack_halves(_dot(act.astype(bf16), wdb[...]))

    @pl.when(i >= n_used)
    def _():
        o_ref[...] = jnp.zeros_like(o_ref)


def _expert_mlp(block_e, n_used, xs, wg, wu, wd, layer, nb):
    last = lambda i, nu: jnp.minimum(i, nu[0] - 1)
    wspec = lambda shape: pl.BlockSpec(shape, lambda i, be, nu: (layer, be[last(i, nu)], 0, 0))
    grid_spec = pltpu.PrefetchScalarGridSpec(
        num_scalar_prefetch=2,
        grid=(nb,),
        in_specs=[
            pl.BlockSpec((BLOCK_TILE_ROWS, HALF), lambda i, be, nu: (last(i, nu), 0)),
            wspec((1, 1, D_MODEL, D_EXPERT)), wspec((1, 1, D_MODEL, D_EXPERT)), wspec((1, 1, D_EXPERT, D_MODEL)),
        ],
        out_specs=pl.BlockSpec((BLOCK_TILE_ROWS, HALF), lambda i, be, nu: (i, 0)),
        scratch_shapes=[pltpu.VMEM((D_MODEL, D_EXPERT), bf16), pltpu.VMEM((D_MODEL, D_EXPERT), bf16),
                        pltpu.VMEM((D_EXPERT, D_MODEL), bf16)],
    )
    return pl.pallas_call(
        _expert_kernel,
        grid_spec=grid_spec,
        out_shape=jax.ShapeDtypeStruct((nb * BLOCK_TILE_ROWS, HALF), jnp.uint32),
        compiler_params=_cparams(1),
        name="expert_mlp",
    )(block_e, n_used, xs, wg, wu, wd)


COMB_TM = 64


def _combine_kernel(idx_ref, idxn_ref, wts_ref, h_ref, hb_ref, sg_ref, su_ref, sd_ref, g_ref, b_ref, ys_hbm,
                    hf_ref, hb_out_ref, buf, sem):
    i = pl.program_id(0)
    nb = pl.num_programs(0)
    slot = i % 2
    cnt = COMB_TM * TOP_K

    @pl.when(i == 0)
    def _():
        _start_gather(ys_hbm, idx_ref, buf, sem, 0, cnt)

    @pl.when(i + 1 < nb)
    def _():
        _start_gather(ys_hbm, idxn_ref, buf, sem, 1 - slot, cnt)

    pltpu.make_async_copy(buf.at[slot], buf.at[slot], sem.at[slot]).wait()
    wts = wts_ref[...]
    groups = COMB_TM // GATHER_UNROLL
    routed = None
    for k in range(TOP_K):
        y_lo, y_hi = _unpack_halves(buf[slot, pl.ds(k * groups, groups)].reshape(COMB_TM, HALF))
        term = jnp.concatenate([y_lo, y_hi], axis=-1) * wts[:, k:k + 1]
        routed = term if routed is None else routed + term
    hb = hb_ref[...]
    act = _silu(_dot(hb, sg_ref[...])) * _dot(hb, su_ref[...])
    shared = _dot(act.astype(bf16), sd_ref[...])
    hn = _layer_norm(ALPHA * h_ref[...] + (routed + shared), g_ref[...], b_ref[...])
    hf_ref[...] = hn
    hb_out_ref[...] = hn.astype(bf16)


def _combine(pos3, wts, h, h_bf, ws_gate, ws_up, ws_down, g, b, ys2d):
    n = h.shape[0]
    tm = COMB_TM
    nb = n // tm
    cnt = tm * TOP_K
    rowspec = pl.BlockSpec((tm, D_MODEL), lambda i: (i, 0))
    const2 = lambda i: (0, 0)
    return pl.pallas_call(
        _combine_kernel,
        grid=(nb,),
        in_specs=[
            pl.BlockSpec((1, 1, cnt), lambda i: (i, 0, 0), memory_space=pltpu.SMEM),
            pl.BlockSpec((1, 1, cnt), lambda i: (jnp.minimum(i + 1, nb - 1), 0, 0), memory_space=pltpu.SMEM),
            pl.BlockSpec((tm, TOP_K), lambda i: (i, 0)),
            rowspec, rowspec,
            pl.BlockSpec((D_MODEL, D_SHARED), const2),
            pl.BlockSpec((D_MODEL, D_SHARED), const2),
            pl.BlockSpec((D_SHARED, D_MODEL), const2),
            pl.BlockSpec((1, D_MODEL), const2),
            pl.BlockSpec((1, D_MODEL), const2),
            pl.BlockSpec(memory_space=pl.ANY),
        ],
        out_specs=[rowspec, rowspec],
        out_shape=[jax.ShapeDtypeStruct((n, D_MODEL), f32),
                   jax.ShapeDtypeStruct((n, D_MODEL), bf16)],
        scratch_shapes=[pltpu.VMEM((2, cnt // GATHER_UNROLL, GATHER_UNROLL, HALF), jnp.uint32),
                        pltpu.SemaphoreType.DMA((2,))],
        compiler_params=_cparams(1),
        name="moe_combine",
    )(pos3, pos3, wts, h, h_bf, ws_gate, ws_up, ws_down, g, b, ys2d)


def _dispatch(eidx, rank, counts_f, nb):
    n = eidx.shape[0]
    counts = counts_f[0, :N_EXPERTS].astype(jnp.int32)
    padded = (counts + EXPERT_ROWS - 1) // EXPERT_ROWS * EXPERT_ROWS
    pend = jnp.cumsum(padded)
    pstart = pend - padded
    starts = jnp.arange(nb, dtype=jnp.int32) * EXPERT_ROWS
    block_e = jnp.minimum(jnp.sum((pend[None, :] <= starts[:, None]).astype(jnp.int32), axis=1), N_EXPERTS - 1)
    n_used = (pend[-1] // EXPERT_ROWS).astype(jnp.int32).reshape(1)
    onehot = eidx[:, :, None] == jnp.arange(N_EXPERTS, dtype=jnp.int32)[None, None, :]
    pos = jnp.sum(jnp.where(onehot, pstart[None, None, :], 0), axis=-1) + rank
    pos_tok3 = pos.reshape(n // SCATTER_TM, 1, SCATTER_TM * TOP_K)
    tm = COMB_TM
    pos_k3 = pos.reshape(n // tm, tm, TOP_K).transpose(0, 2, 1).reshape(n // tm, 1, tm * TOP_K)
    return block_e.astype(jnp.int32), n_used, pstart + counts, pend, pos_tok3, pos_k3


def _consts():
    r = jnp.arange(CHUNK)
    tri = (r[:, None] >= r[None, :]).astype(bf16)
    trit = (r[:, None] <= r[None, :]).astype(bf16)
    lane = jnp.arange(LANES)
    ch = jnp.arange(SSD_HEADS * SSD_HEADDIM)
    ssd_expand = (lane[:, None] == (ch[None, :] // SSD_HEADDIM)).astype(bf16)
    hc = jnp.arange(HGRN_HEADS * HGRN_DIM)
    hm_rows = jnp.arange(HGRN_HEADS * SUB)
    hmask = ((hm_rows[:, None] // SUB) == (hc[None, :] // HGRN_DIM)).astype(f32)
    qr = jnp.arange((HGRN_NBLK - 1) * HGRN_HEADS * SUB)
    qblk = qr // (HGRN_HEADS * SUB) + 1
    kstart = SUB * jnp.arange(HGRN_NBLK - 1) * jnp.arange(1, HGRN_NBLK) // 2
    kblk = jnp.sum(lane[None, :] >= kstart[:, None], axis=0)
    offmask = ((qblk[:, None] == kblk[None, :]) & (lane[None, :] < HGRN_OFF_KEYS)).astype(f32)
    rt = jnp.arange(ROUTER_TM)
    ltri = (rt[:, None] > rt[None, :]).astype(bf16)
    return tri, trit, ssd_expand, hmask, offmask, ltri


def _hgrn_lower_bounds(lb_param):
    sm = jax.nn.softmax(lb_param.astype(f32), axis=0)
    return jnp.cumsum(sm, axis=0) - sm[0:1]


def kernel(x, mem, w_in, conv_w, conv_b, dt_bias, a_log, d_skip, ssd_norm, pool_w, pool_scale, hgrn_lb, hgrn_norm, w_mem_kv, w_branch, w_o, ln1_g, ln1_b, w_router, router_bias, w_exp_gate, w_exp_up, w_exp_down, w_sh_gate, w_sh_up, w_sh_down, ln2_g, ln2_b):
    bsz, seq, d = x.shape
    n = bsz * seq
    nk = n * TOP_K
    nb = -(-nk // EXPERT_ROWS) + N_EXPERTS
    tri, trit, ssd_expand, hmask, offmask, ltri = _consts()
    lb_all = _hgrn_lower_bounds(hgrn_lb)

    c_dt0 = BRANCH_WIDTH * 2 + SSD_CONV_DIM
    c_dt1 = c_dt0 + SSD_HEADS
    c_gate = c_dt1 + 5 * BRANCH_WIDTH
    pad_lanes = LANES - SSD_HEADS

    h = x.reshape(n, d)
    h_bf = h.astype(bf16)
    mem_bf = mem.reshape(bsz * MEM_TOKENS, d).astype(bf16)

    w_a = w_in[:, :, :c_dt0].astype(bf16)
    w_b = w_in[:, :, c_dt1:].astype(bf16)
    w_dt = jnp.pad(w_in[:, :, c_dt0:c_dt1], ((0, 0), (0, 0), (0, pad_lanes))).astype(bf16)
    w_kv_bf = w_mem_kv.astype(bf16)
    n_b = c_gate - c_dt1

    for l in range(DEPTH):
        proj = _matmul(h_bf, w_a, l, f32, 512, 1024)
        proj_b = _matmul(h_bf, w_b, l, f32, 512, 1024, 0, n_b)
        dtp = _matmul(h_bf, w_dt, l, f32, 512, LANES)
        kv = _matmul(mem_bf, w_kv_bf, l, bf16, 512, 1024)

        br_pool = _pool_mixer(proj, pool_w[l].astype(bf16), pool_scale[l].reshape(1, -1), bsz, seq)
        br_ssd = _ssd_mixer(
            proj, dtp, conv_w[l], conv_b[l].reshape(1, -1),
            jnp.pad(dt_bias[l], (0, pad_lanes)).reshape(1, -1),
            jnp.pad(a_log[l], (0, pad_lanes)).reshape(1, -1),
            jnp.repeat(d_skip[l], SSD_HEADDIM).reshape(1, -1),
            ssd_norm[l].reshape(1, -1), (tri, trit, ssd_expand), bsz, seq)
        br_hgrn = _hgrn_mixer(proj_b, lb_all[l].reshape(1, -1), hgrn_norm[l].reshape(1, -1),
                              (tri, hmask, offmask), bsz, seq)
        br_mem = _mem_attention(proj_b, kv, bsz, seq)

        merged = _merge(h_bf, (br_pool, br_ssd, br_hgrn, br_mem), w_b, l, n_b, w_branch[l].astype(bf16))
        h, h_bf, h_pk = _wo_ln(merged, w_o[l].astype(bf16), h, ln1_g[l].reshape(1, -1), ln1_b[l].reshape(1, -1))

        eidx, wts, rank, counts = _router(h, jnp.pad(w_router[l], ((0, 0), (0, LANES - N_EXPERTS))),
                                          jnp.pad(router_bias[l], (0, LANES - N_EXPERTS)).reshape(1, -1), ltri)
        block_e, n_used, vend, pend, pos_tok3, pos_k3 = _dispatch(eidx, rank, counts, nb)
        xs = _scatter_rows(vend, pend, pos_tok3, h_pk, nb)
        ys = _expert_mlp(block_e, n_used, xs, w_exp_gate, w_exp_up, w_exp_down, l, nb)
        h, h_bf = _combine(pos_k3, wts, h, h_bf, w_sh_gate[l].astype(bf16), w_sh_up[l].astype(bf16),
                                   w_sh_down[l].astype(bf16), ln2_g[l].reshape(1, -1), ln2_b[l].reshape(1, -1), ys)
    return h.reshape(bsz, seq, d)
```

```python
import functools

import jax
import jax.numpy as jnp
from jax import lax
from jax.experimental import pallas as pl
from jax.experimental.pallas import tpu as pltpu

f32 = jnp.float32
bf16 = jnp.bfloat16

D_MODEL = 2048
DEPTH = 4
CHUNK = 64
N_BRANCH = 4
BRANCH_WIDTH = 1024
POOL_WINDOWS = (2, 4, 8, 16)
POOL_GROUP = 256
SSD_HEADS = 16
SSD_HEADDIM = 64
SSD_STATE = 128
SSD_GROUPS = 4
SSD_CONV = 4
SSD_CONV_DIM = 2048
HGRN_HEADS = 8
HGRN_DIM = 128
MEM_TOKENS = 256
MEM_HEADS = 4
MEM_HEAD_DIM = 256
N_EXPERTS = 64
TOP_K = 8
N_GROUPS = 8
TOPK_GROUPS = 4
D_EXPERT = 256
D_SHARED = 256
ROUTED_SCALE = 2.5
MASK_SCORE = -1e4
ALPHA = (2 * DEPTH) ** 0.25
EPS = 1e-5

LANES = 128
ROW_TILES = 1
EXPERT_ROWS = 512
ZERO_ROWS = 128
SUB = 16
VMEM_LIMIT = 56 * 1024 * 1024

NEG_INF = float("-inf")


def _cparams(n_axes):
    return pltpu.CompilerParams(dimension_semantics=("arbitrary",) * n_axes,
                                vmem_limit_bytes=VMEM_LIMIT)


def _sigmoid(x):
    return jax.nn.sigmoid(x)


def _silu(x):
    return x * jax.nn.sigmoid(x)


def _split3(x):
    hi = x.astype(bf16)
    r1 = x - hi.astype(f32)
    mid = r1.astype(bf16)
    lo = (r1 - mid.astype(f32)).astype(bf16)
    return hi, mid, lo


def _dot(a, b):
    return jnp.dot(a, b, preferred_element_type=f32)


def _dot_nt(a, b):
    return lax.dot_general(a, b, (((1,), (1,)), ((), ())), preferred_element_type=f32)


def _dot_tn(a, b):
    return lax.dot_general(a, b, (((0,), (0,)), ((), ())), preferred_element_type=f32)


def _sel_left(sel01, x):
    return sum(_dot(sel01, p) for p in _split3(x))


def _sel_right(x, sel01):
    return sum(_dot(p, sel01) for p in _split3(x))


def _mm_kernel(x_ref, w_ref, o_ref):
    o_ref[...] = _dot(x_ref[...], w_ref[0]).astype(o_ref.dtype)


def _mm_nt_kernel(x_ref, wt_ref, o_ref):
    o_ref[...] = _dot_nt(x_ref[...], wt_ref[0]).astype(o_ref.dtype)


def _matmul(x, w, layer, out_dtype, tm, tn, col0=0, ncols=None, transposed=False):
    m, k = x.shape
    n = w.shape[1 if transposed else 2] if ncols is None else ncols
    cb = col0 // tn
    if transposed:
        wspec = pl.BlockSpec((1, tn, k), lambda j, i: (layer, cb + j, 0))
    else:
        wspec = pl.BlockSpec((1, k, tn), lambda j, i: (layer, 0, cb + j))
    return pl.pallas_call(
        _mm_nt_kernel if transposed else _mm_kernel,
        grid=(n // tn, m // tm),
        in_specs=[pl.BlockSpec((tm, k), lambda j, i: (i, 0)), wspec],
        out_specs=pl.BlockSpec((tm, tn), lambda j, i: (i, j)),
        out_shape=jax.ShapeDtypeStruct((m, n), out_dtype),
        compiler_params=_cparams(2),
        name="matmul",
    )(x, w)


POOL_T = 512
POOL_HALO = 16


def _pool_kernel(u_ref, halo_ref, w_ref, scale_ref, o_ref):
    i = pl.program_id(1)
    u = u_ref[...]
    halo = jnp.where(i > 0, halo_ref[...], 0.0)
    ext = jnp.concatenate([halo, u], axis=0)
    t = i * POOL_T + lax.broadcasted_iota(jnp.int32, (POOL_T, 1), 0)
    outs = []
    for gi, w in enumerate(POOL_WINDOWS):
        xg = ext[:, gi * POOL_GROUP:(gi + 1) * POOL_GROUP]
        s = xg
        span = 1
        while span < w:
            s = s[span:, :] + s[:-span, :]
            span *= 2
        s = s[s.shape[0] - POOL_T:, :]
        cnt = jnp.minimum(t + 1, w).astype(f32)
        d = s / cnt - u[:, gi * POOL_GROUP:(gi + 1) * POOL_GROUP]
        outs.append(_dot(d.astype(bf16), w_ref[gi]))
    y = jnp.concatenate(outs, axis=-1) * scale_ref[...]
    o_ref[...] = y.astype(o_ref.dtype)


def _pool_mixer(proj, pool_w, pool_scale, bsz, seq):
    n = bsz * seq
    nt = seq // POOL_T
    return pl.pallas_call(
        _pool_kernel,
        grid=(bsz, nt),
        in_specs=[
            pl.BlockSpec((POOL_T, BRANCH_WIDTH), lambda b, i: (b * nt + i, 0)),
            pl.BlockSpec((POOL_HALO, BRANCH_WIDTH),
                         lambda b, i: (jnp.maximum((b * nt + i) * (POOL_T // POOL_HALO) - 1, 0), 0)),
            pl.BlockSpec((len(POOL_WINDOWS), POOL_GROUP, POOL_GROUP), lambda b, i: (0, 0, 0)),
            pl.BlockSpec((1, BRANCH_WIDTH), lambda b, i: (0, 0)),
        ],
        out_specs=pl.BlockSpec((POOL_T, BRANCH_WIDTH), lambda b, i: (b * nt + i, 0)),
        out_shape=jax.ShapeDtypeStruct((n, BRANCH_WIDTH), bf16),
        compiler_params=_cparams(2),
        name="pool_mixer",
    )(proj, proj, pool_w, pool_scale)


CONV_TAIL = 8


def _softplus(x):
    return jnp.maximum(x, 0.0) + jnp.log1p(jnp.exp(-jnp.abs(x)))


def _ssd_kernel(z_ref, xbc_ref, dt_ref, cw_ref, cb_ref, dtb_ref, alog_ref, dskip_ref, nw_ref,
                tri_ref, trit_ref, exp_ref, o_ref, state_ref, tail_ref):
    i = pl.program_id(0)

    @pl.when(i == 0)
    def _():
        state_ref[...] = jnp.zeros_like(state_ref)
        tail_ref[...] = jnp.zeros_like(tail_ref)

    for bi in range(z_ref.shape[0]):
        _ssd_chunk(z_ref.at[bi], xbc_ref.at[bi], dt_ref.at[bi], cw_ref, cb_ref, dtb_ref, alog_ref, dskip_ref, nw_ref,
                   tri_ref, trit_ref, exp_ref, o_ref.at[bi], state_ref.at[bi], tail_ref.at[bi])


def _ssd_chunk(z_ref, xbc_ref, dt_ref, cw_ref, cb_ref, dtb_ref, alog_ref, dskip_ref, nw_ref,
               tri_ref, trit_ref, exp_ref, o_ref, state_ref, tail_ref):
    xbc = xbc_ref[...]
    ext = jnp.concatenate([tail_ref[...], xbc], axis=0)
    cw = cw_ref[...]
    conv = cb_ref[...]
    for k in range(SSD_CONV):
        off = CONV_TAIL - (SSD_CONV - 1) + k
        conv = conv + cw[k:k + 1, :] * ext[off:off + CHUNK, :]
    tail_ref[...] = xbc[CHUNK - CONV_TAIL:, :]
    act = _silu(conv)
    inner = SSD_HEADS * SSD_HEADDIM
    gw = SSD_GROUPS * SSD_STATE
    xs = act[:, :inner]
    bmat = act[:, inner:inner + gw]
    cmat = act[:, inner + gw:]

    dt = _softplus(dt_ref[...] + dtb_ref[...])
    a = -jnp.exp(alog_ref[...])
    ad = dt * a
    ad_parts = _split3(ad)
    tri = tri_ref[...]
    acs = sum(_dot(tri, p) for p in ad_parts)
    acs_t = sum(_dot_tn(p, trit_ref[...]) for p in ad_parts)
    a_last = acs[CHUNK - 1:CHUNK, :]
    expand = exp_ref[...]
    dt_e = _sel_right(dt, expand)
    eacs_e = _sel_right(jnp.exp(acs), expand)
    dec_e = _sel_right(jnp.exp(a_last - acs), expand)
    x = xs * dt_e
    xb = x.astype(bf16)
    xd = (x * dec_e).astype(bf16)
    elast_e = eacs_e[CHUNK - 1:CHUNK, :]

    row = lax.broadcasted_iota(jnp.int32, (CHUNK, CHUNK), 0)
    col = lax.broadcasted_iota(jnp.int32, (CHUNK, CHUNK), 1)
    causal = row >= col
    hpg = SSD_HEADS // SSD_GROUPS
    gch = hpg * SSD_HEADDIM
    y_diag = []
    y_off = []
    for g in range(SSD_GROUPS):
        bg = bmat[:, g * SSD_STATE:(g + 1) * SSD_STATE].astype(bf16)
        cg = cmat[:, g * SSD_STATE:(g + 1) * SSD_STATE].astype(bf16)
        cb = _dot_nt(cg, bg)
        for hh in range(hpg):
            h = g * hpg + hh
            diff = acs[:, h:h + 1] - acs_t[h:h + 1, :]
            decay = jnp.where(causal, jnp.exp(jnp.where(causal, diff, 0.0)), 0.0)
            sc = (cb * decay).astype(bf16)
            y_diag.append(_dot(sc, xb[:, h * SSD_HEADDIM:(h + 1) * SSD_HEADDIM]))
        st = state_ref[:, g * gch:(g + 1) * gch]
        y_off.append(_dot(cg, st.astype(bf16)))
        upd = _dot_tn(bg, xd[:, g * gch:(g + 1) * gch])
        state_ref[:, g * gch:(g + 1) * gch] = st * elast_e[:, g * gch:(g + 1) * gch] + upd
    y = (jnp.concatenate(y_diag, axis=-1) + jnp.concatenate(y_off, axis=-1) * eacs_e
         + xs * dskip_ref[...])
    y = y * _silu(z_ref[...])
    nw = nw_ref[...]
    ngw = inner // SSD_GROUPS
    outs = []
    for g in range(SSD_GROUPS):
        yg = y[:, g * ngw:(g + 1) * ngw]
        ms = jnp.mean(yg * yg, axis=-1, keepdims=True)
        outs.append(yg * lax.rsqrt(ms + EPS) * nw[:, g * ngw:(g + 1) * ngw])
    o_ref[...] = jnp.concatenate(outs, axis=-1).astype(o_ref.dtype)


def _ssd_mixer(proj, dtp, conv_w, conv_b, dt_bias, a_log, d_skip_e, norm_w, consts, bsz, seq):
    n = bsz * seq
    nc = seq // CHUNK
    tri, trit, expand = consts
    const2 = lambda i: (0, 0)
    inner = SSD_HEADS * SSD_HEADDIM
    proj3 = proj.reshape(bsz, seq, proj.shape[1])
    dtp3 = dtp.reshape(bsz, seq, LANES)
    out = pl.pallas_call(
        _ssd_kernel,
        grid=(nc,),
        in_specs=[
            pl.BlockSpec((bsz, CHUNK, inner), lambda i: (0, i, 1)),
            pl.BlockSpec((bsz, CHUNK, SSD_CONV_DIM), lambda i: (0, i, 1)),
            pl.BlockSpec((bsz, CHUNK, LANES), lambda i: (0, i, 0)),
            pl.BlockSpec((SSD_CONV, SSD_CONV_DIM), const2),
            pl.BlockSpec((1, SSD_CONV_DIM), const2),
            pl.BlockSpec((1, LANES), const2),
            pl.BlockSpec((1, LANES), const2),
            pl.BlockSpec((1, inner), const2),
            pl.BlockSpec((1, inner), const2),
            pl.BlockSpec((CHUNK, CHUNK), const2),
            pl.BlockSpec((CHUNK, CHUNK), const2),
            pl.BlockSpec((LANES, inner), const2),
        ],
        out_specs=pl.BlockSpec((bsz, CHUNK, inner), lambda i: (0, i, 0)),
        out_shape=jax.ShapeDtypeStruct((bsz, seq, inner), bf16),
        scratch_shapes=[pltpu.VMEM((bsz, SSD_STATE, inner), f32),
                        pltpu.VMEM((bsz, CONV_TAIL, SSD_CONV_DIM), f32)],
        compiler_params=_cparams(1),
        name="ssd_mixer",
    )(proj3, proj3, dtp3, conv_w, conv_b, dt_bias, a_log, d_skip_e, norm_w, tri, trit, expand)
    return out.reshape(n, inner)


LOG2E = 1.4426950408889634
HGRN_NBLK = CHUNK // SUB
HGRN_OFF_KEYS = SUB * (HGRN_NBLK - 1) * HGRN_NBLK // 2


def _hgrn_chunk(q, f, v, g, lb, nw, tri, hmask, offmask, state_ref):
    width = HGRN_HEADS * HGRN_DIM
    qf = _silu(q)
    kf = (1.0 - lb) * _sigmoid(-f)
    b2 = _sel_left(tri, jnp.log1p(-kf)) * LOG2E
    c2 = b2 - jnp.log2(kf)
    b_last = b2[CHUNK - 1:CHUNK, :]
    vb = v.astype(bf16)

    qd = (qf * jnp.exp2(b2)).astype(bf16)
    kd = jnp.exp2(b_last - c2).astype(bf16)
    eb_last = jnp.exp2(b_last)
    o_inter = []
    for h in range(HGRN_HEADS):
        sl = slice(h * HGRN_DIM, (h + 1) * HGRN_DIM)
        st = state_ref[h]
        o_inter.append(_dot_nt(qd[:, sl], st.astype(bf16)))
        state_ref[h] = st * eb_last[:, sl] + _dot_tn(vb[:, sl], kd[:, sl])
    o = jnp.concatenate(o_inter, axis=-1)

    qm, kk, vv = [], [], []
    for blk in range(1, HGRN_NBLK):
        r0 = blk * SUB
        anchor = b2[r0 - 1:r0, :]
        qq = qf[r0:r0 + SUB, :] * jnp.exp2(b2[r0:r0 + SUB, :] - anchor)
        qm.append((jnp.concatenate([qq] * HGRN_HEADS, axis=0) * hmask).astype(bf16))
        kk.append(jnp.exp2(anchor - c2[:r0, :]).astype(bf16))
        vv.append(vb[:r0, :])
    zpad = jnp.zeros((LANES - HGRN_OFF_KEYS, width), bf16)
    att = _dot_nt(jnp.concatenate(qm, axis=0), jnp.concatenate(kk + [zpad], axis=0))
    res = _dot((att * offmask).astype(bf16), jnp.concatenate(vv + [zpad], axis=0))
    rows = [jnp.zeros((SUB, width), f32)]
    for blk in range(1, HGRN_NBLK):
        base = (blk - 1) * HGRN_HEADS * SUB
        acc = None
        for h in range(HGRN_HEADS):
            part = res[base + h * SUB:base + (h + 1) * SUB, :] * hmask[h * SUB:(h + 1) * SUB, :]
            acc = part if acc is None else acc + part
        rows.append(acc)
    o = o + jnp.concatenate(rows, axis=0)

    tpos = lax.broadcasted_iota(jnp.int32, (CHUNK, 1), 0) & (SUB - 1)
    diag = [None] * HGRN_HEADS
    for d in range(SUB):
        cs = c2 if d == 0 else pltpu.roll(c2, d, 0)
        vs = v if d == 0 else pltpu.roll(v, d, 0)
        p = qf * jnp.exp2(b2 - cs)
        same_block = tpos >= d
        for h in range(HGRN_HEADS):
            sl = slice(h * HGRN_DIM, (h + 1) * HGRN_DIM)
            att = jnp.where(same_block, jnp.sum(p[:, sl], axis=-1, keepdims=True), 0.0)
            term = att * vs[:, sl]
            diag[h] = term if diag[h] is None else diag[h] + term
    o = o + jnp.concatenate(diag, axis=-1)

    gate = _silu(g)
    outs = []
    for h in range(HGRN_HEADS):
        sl = slice(h * HGRN_DIM, (h + 1) * HGRN_DIM)
        oh = o[:, sl]
        ms = jnp.mean(oh * oh, axis=-1, keepdims=True)
        outs.append(oh * lax.rsqrt(ms + EPS) * nw * gate[:, sl])
    return jnp.concatenate(outs, axis=-1)


def _hgrn_kernel(q_ref, f_ref, i_ref, g_ref, lb_ref, nw_ref, tri_ref, hmask_ref, offmask_ref, o_ref, state_ref):
    c = pl.program_id(0)

    @pl.when(c == 0)
    def _():
        state_ref[...] = jnp.zeros_like(state_ref)

    for bi in range(q_ref.shape[0]):
        out = _hgrn_chunk(q_ref[bi], f_ref[bi], i_ref[bi], g_ref[bi], lb_ref[...], nw_ref[...], tri_ref[...],
                          hmask_ref[...], offmask_ref[...], state_ref.at[bi])
        o_ref[bi] = out.astype(o_ref.dtype)


def _hgrn_mixer(proj, lb, norm_w, consts, bsz, seq):
    n = bsz * seq
    nc = seq // CHUNK
    tri, hmask, offmask = consts
    width = HGRN_HEADS * HGRN_DIM
    const2 = lambda i: (0, 0)
    col = lambda j: (lambda i: (0, i, j))
    proj3 = proj.reshape(bsz, seq, proj.shape[1])
    qrows = (HGRN_NBLK - 1) * HGRN_HEADS * SUB
    out = pl.pallas_call(
        _hgrn_kernel,
        grid=(nc,),
        in_specs=[
            pl.BlockSpec((bsz, CHUNK, width), col(0)),
            pl.BlockSpec((bsz, CHUNK, width), col(1)),
            pl.BlockSpec((bsz, CHUNK, width), col(2)),
            pl.BlockSpec((bsz, CHUNK, width), col(3)),
            pl.BlockSpec((1, width), const2),
            pl.BlockSpec((1, HGRN_DIM), const2),
            pl.BlockSpec((CHUNK, CHUNK), const2),
            pl.BlockSpec((HGRN_HEADS * SUB, width), const2),
            pl.BlockSpec((qrows, LANES), const2),
        ],
        out_specs=pl.BlockSpec((bsz, CHUNK, width), lambda i: (0, i, 0)),
        out_shape=jax.ShapeDtypeStruct((bsz, seq, width), bf16),
        scratch_shapes=[pltpu.VMEM((bsz, HGRN_HEADS, HGRN_DIM, HGRN_DIM), f32)],
        compiler_params=_cparams(1),
        name="hgrn_mixer",
    )(proj3, proj3, proj3, proj3, lb, norm_w, tri, hmask, offmask)
    return out.reshape(n, width)


MEM_T = 512


def _memattn_kernel(q_ref, k_ref, v_ref, o_ref):
    q = q_ref[...].astype(bf16)
    outs = []
    for h in range(MEM_HEADS):
        sl = slice(h * MEM_HEAD_DIM, (h + 1) * MEM_HEAD_DIM)
        s = _dot_nt(q[:, sl], k_ref[:, sl]) * (MEM_HEAD_DIM ** -0.5)
        m = jnp.max(s, axis=-1, keepdims=True)
        e = jnp.exp(s - m)
        p = e / jnp.sum(e, axis=-1, keepdims=True)
        outs.append(_dot(p.astype(bf16), v_ref[:, sl]))
    o_ref[...] = jnp.concatenate(outs, axis=-1).astype(o_ref.dtype)


def _mem_attention(proj, kv, bsz, seq):
    n = bsz * seq
    nt = seq // MEM_T
    width = MEM_HEADS * MEM_HEAD_DIM
    return pl.pallas_call(
        _memattn_kernel,
        grid=(bsz, nt),
        in_specs=[
            pl.BlockSpec((MEM_T, width), lambda b, i: (b * nt + i, 4)),
            pl.BlockSpec((MEM_TOKENS, width), lambda b, i: (b, 0)),
            pl.BlockSpec((MEM_TOKENS, width), lambda b, i: (b, 1)),
        ],
        out_specs=pl.BlockSpec((MEM_T, width), lambda b, i: (b * nt + i, 0)),
        out_shape=jax.ShapeDtypeStruct((n, width), bf16),
        compiler_params=_cparams(2),
        name="mem_attention",
    )(proj, kv, kv)


MERGE_TM = 512
MERGE_TN = 512


def _merge_kernel(h_ref, b0_ref, b1_ref, b2_ref, b3_ref, g0_ref, g1_ref, g2_ref, g3_ref, wb_ref, o_ref):
    h = h_ref[...]
    acc = None
    for n, (br, wg) in enumerate(zip((b0_ref, b1_ref, b2_ref, b3_ref), (g0_ref, g1_ref, g2_ref, g3_ref))):
        gate = _sigmoid(_dot_nt(h, wg[0]))
        term = gate * _dot(br[...], wb_ref[n])
        acc = term if acc is None else acc + term
    o_ref[...] = acc.astype(o_ref.dtype)


def _merge(h_bf, branches, w_packed, layer, gate_col0, w_branch):
    n = h_bf.shape[0]
    tm, tn = MERGE_TM, MERGE_TN
    bspec = pl.BlockSpec((tm, BRANCH_WIDTH), lambda j, i: (i, 0))
    gspec = lambda nbr: pl.BlockSpec((1, tn, D_MODEL),
                                     lambda j, i: (layer, (gate_col0 + nbr * D_MODEL) // tn + j, 0))
    return pl.pallas_call(
        _merge_kernel,
        grid=(D_MODEL // tn, n // tm),
        in_specs=[pl.BlockSpec((tm, D_MODEL), lambda j, i: (i, 0)), bspec, bspec, bspec, bspec,
                  gspec(0), gspec(1), gspec(2), gspec(3),
                  pl.BlockSpec((N_BRANCH, BRANCH_WIDTH, tn), lambda j, i: (0, 0, j))],
        out_specs=pl.BlockSpec((tm, tn), lambda j, i: (i, j)),
        out_shape=jax.ShapeDtypeStruct((n, D_MODEL), bf16),
        compiler_params=_cparams(2),
        name="gated_merge",
    )(h_bf, *branches, w_packed, w_packed, w_packed, w_packed, w_branch)


def _layer_norm(x, g, b):
    mu = jnp.mean(x, axis=-1, keepdims=True)
    xc = x - mu
    var = jnp.mean(xc * xc, axis=-1, keepdims=True)
    return xc * lax.rsqrt(var + EPS) * g + b


WO_TM = 256


HALF = D_MODEL // 2
U16_HI = 0xFFFF0000


def _pack_halves(x):
    lo = lax.bitcast_convert_type(x[:, :HALF].astype(bf16).astype(f32), jnp.uint32)
    hi = lax.bitcast_convert_type(x[:, HALF:].astype(bf16).astype(f32), jnp.uint32)
    return lax.shift_right_logical(lo, jnp.uint32(16)) | (hi & jnp.uint32(U16_HI))


def _unpack_halves(w):
    lo = lax.bitcast_convert_type(lax.shift_left(w, jnp.uint32(16)), f32)
    hi = lax.bitcast_convert_type(w & jnp.uint32(U16_HI), f32)
    return lo, hi


def _wo_ln_kernel(m_ref, w_ref, h_ref, g_ref, b_ref, hf_ref, hb_ref, hp_ref):
    y = ALPHA * h_ref[...] + _dot(m_ref[...], w_ref[...])
    hn = _layer_norm(y, g_ref[...], b_ref[...])
    hf_ref[...] = hn
    hb_ref[...] = hn.astype(bf16)
    hp_ref[...] = _pack_halves(hn)


def _wo_ln(merged, w_o, h, g, b):
    n = h.shape[0]
    tm = WO_TM
    rowspec = pl.BlockSpec((tm, D_MODEL), lambda i: (i, 0))
    const2 = lambda i: (0, 0)
    return pl.pallas_call(
        _wo_ln_kernel,
        grid=(n // tm,),
        in_specs=[rowspec, pl.BlockSpec((D_MODEL, D_MODEL), const2), rowspec,
                  pl.BlockSpec((1, D_MODEL), const2), pl.BlockSpec((1, D_MODEL), const2)],
        out_specs=[rowspec, rowspec, pl.BlockSpec((tm, HALF), lambda i: (i, 0))],
        out_shape=[jax.ShapeDtypeStruct((n, D_MODEL), f32),
                   jax.ShapeDtypeStruct((n, D_MODEL), bf16),
                   jax.ShapeDtypeStruct((n, HALF), jnp.uint32)],
        compiler_params=_cparams(1),
        name="wo_layernorm",
    )(merged, w_o, h, g, b)


ROUTER_TM = 256


def _router_kernel(h_ref, w_ref, bias_ref, ltri_ref, eidx_ref, wts_ref, rank_ref, counts_ref, cnt_ref):
    step = pl.program_id(0)

    @pl.when(step == 0)
    def _():
        cnt_ref[...] = jnp.zeros_like(cnt_ref)

    h = h_ref[...]
    w = w_ref[...]
    h_hi = h.astype(bf16)
    h_lo = (h - h_hi.astype(f32)).astype(bf16)
    w_hi = w.astype(bf16)
    w_lo = (w - w_hi.astype(f32)).astype(bf16)
    logits = _dot(h_hi, w_hi) + _dot(h_hi, w_lo) + _dot(h_lo, w_hi)
    scores = _sigmoid(logits)
    lane_i = lax.broadcasted_iota(jnp.int32, logits.shape, 1)
    lane = lane_i.astype(f32)
    real = lane_i < N_EXPERTS
    choice = jnp.where(real, scores + bias_ref[...], NEG_INF)
    gsize = N_EXPERTS // N_GROUPS
    grp = lax.shift_right_logical(lane_i, gsize.bit_length() - 1)
    big = float(LANES)

    def first_argmax(x):
        m = jnp.max(x, axis=-1, keepdims=True)
        idx = jnp.min(jnp.where(x == m, lane, big), axis=-1, keepdims=True)
        return m, idx

    gscore = []
    for g in range(N_GROUPS):
        cg = jnp.where(grp == g, choice, NEG_INF)
        m1, i1 = first_argmax(cg)
        m2 = jnp.max(jnp.where(lane == i1, NEG_INF, cg), axis=-1, keepdims=True)
        gscore.append(m1 + m2)
    keep = jnp.zeros(logits.shape, f32)
    for g in range(N_GROUPS):
        rank = jnp.zeros_like(gscore[g])
        for o in range(N_GROUPS):
            if o == g:
                continue
            ahead = (gscore[o] > gscore[g]) if o > g else (gscore[o] >= gscore[g])
            rank = rank + jnp.where(ahead, 1.0, 0.0)
        keep = jnp.where((grp == g) & (rank < TOPK_GROUPS), 1.0, keep)
    cur = jnp.where(real, jnp.where(keep > 0.0, choice, MASK_SCORE), NEG_INF)
    idxs, ws, hits = [], [], []
    sel = jnp.zeros(logits.shape, f32)
    for _ in range(TOP_K):
        _, ik = first_argmax(cur)
        hit = lane == ik
        ws.append(jnp.sum(jnp.where(hit, scores, 0.0), axis=-1, keepdims=True))
        idxs.append(ik)
        hits.append(hit)
        sel = jnp.where(hit, 1.0, sel)
        cur = jnp.where(hit, NEG_INF, cur)
    wsum = ws[0]
    for wk in ws[1:]:
        wsum = wsum + wk
    eidx_ref[...] = jnp.concatenate(idxs, axis=-1).astype(jnp.int32)
    wts_ref[...] = jnp.concatenate([wk / wsum * ROUTED_SCALE for wk in ws], axis=-1)
    before = _dot(ltri_ref[...], sel.astype(bf16)) + cnt_ref[...]
    ranks = [jnp.sum(jnp.where(hit, before, 0.0), axis=-1, keepdims=True) for hit in hits]
    rank_ref[...] = jnp.concatenate(ranks, axis=-1).astype(jnp.int32)
    cnt_ref[...] = cnt_ref[...] + jnp.sum(sel, axis=0, keepdims=True)
    counts_ref[...] = cnt_ref[...]


def _router(h, w_router_p, bias_p, ltri):
    n = h.shape[0]
    tm = ROUTER_TM
    const2 = lambda i: (0, 0)
    kspec = pl.BlockSpec((tm, TOP_K), lambda i: (i, 0))
    return pl.pallas_call(
        _router_kernel,
        grid=(n // tm,),
        in_specs=[pl.BlockSpec((tm, D_MODEL), lambda i: (i, 0)),
                  pl.BlockSpec((D_MODEL, LANES), const2),
                  pl.BlockSpec((1, LANES), const2),
                  pl.BlockSpec((tm, tm), const2)],
        out_specs=[kspec, kspec, kspec, pl.BlockSpec((1, LANES), const2)],
        out_shape=[jax.ShapeDtypeStruct((n, TOP_K), jnp.int32),
                   jax.ShapeDtypeStruct((n, TOP_K), f32),
                   jax.ShapeDtypeStruct((n, TOP_K), jnp.int32),
                   jax.ShapeDtypeStruct((1, LANES), f32)],
        scratch_shapes=[pltpu.VMEM((1, LANES), f32)],
        compiler_params=_cparams(1),
        name="router",
    )(h, w_router_p, bias_p, ltri)


GATHER_UNROLL = 8


def _start_gather(src_hbm, idx_ref, buf, sem, slot, count):
    def body(o, carry):
        for u in range(GATHER_UNROLL):
            pltpu.make_async_copy(src_hbm.at[pl.ds(idx_ref[0, 0, o * GATHER_UNROLL + u], 1)],
                                  buf.at[slot, o, pl.ds(u, 1)], sem.at[slot]).start()
        return carry
    lax.fori_loop(0, count // GATHER_UNROLL, body, 0)


SCATTER_TM = 128
SCATTER_SLOTS = 3
BLOCK_TILE_ROWS = EXPERT_ROWS * ROW_TILES


def _scatter_kernel(vend_ref, pend_ref, pos_ref, hrows_hbm, xs_hbm, stage, zbuf, lsem, ssem, zsem):
    i = pl.program_id(0)
    nsteps = pl.num_programs(0)
    cnt = SCATTER_TM * TOP_K

    @pl.when(i == 0)
    def _():
        zbuf[...] = jnp.zeros_like(zbuf)

        def piece(r0):
            return pltpu.make_async_copy(zbuf, xs_hbm.at[pl.ds(pl.multiple_of(r0, ZERO_ROWS), ZERO_ROWS)], zsem)

        def span(e):
            first = lax.shift_right_logical(vend_ref[e], ZERO_ROWS.bit_length() - 1)
            last = lax.shift_right_logical(pend_ref[e], ZERO_ROWS.bit_length() - 1)
            return first, last

        def start(e, carry):
            first, last = span(e)
            lax.fori_loop(first, last, lambda p, c: (piece(p * ZERO_ROWS).start(), c)[1], 0)
            return carry

        def wait(e, carry):
            first, last = span(e)
            lax.fori_loop(first, last, lambda p, c: (piece(p * ZERO_ROWS).wait(), c)[1], 0)
            return carry

        tail0 = lax.shift_right_logical(pend_ref[N_EXPERTS - 1], ZERO_ROWS.bit_length() - 1)
        tail1 = xs_hbm.shape[0] // ZERO_ROWS
        lax.fori_loop(0, N_EXPERTS, start, 0)
        lax.fori_loop(tail0, tail1, lambda p, c: (piece(p * ZERO_ROWS).start(), c)[1], 0)
        lax.fori_loop(0, N_EXPERTS, wait, 0)
        lax.fori_loop(tail0, tail1, lambda p, c: (piece(p * ZERO_ROWS).wait(), c)[1], 0)

    groups = SCATTER_TM // GATHER_UNROLL
    slot = lax.rem(i, SCATTER_SLOTS)
    nxt = lax.rem(i + 1, SCATTER_SLOTS)

    def load(j, s):
        src = pl.multiple_of(j * groups, groups)
        return pltpu.make_async_copy(hrows_hbm.at[pl.ds(src, groups)], stage.at[s], lsem.at[s])

    def scatter_wait(s):
        pltpu.make_async_copy(xs_hbm.at[pl.ds(0, cnt * ROW_TILES)], xs_hbm.at[pl.ds(0, cnt * ROW_TILES)],
                              ssem.at[s]).wait()

    @pl.when(i == 0)
    def _():
        load(0, 0).start()

    @pl.when(i >= SCATTER_SLOTS - 1)
    def _():
        scatter_wait(nxt)

    @pl.when(i + 1 < nsteps)
    def _():
        load(i + 1, nxt).start()

    load(i, slot).wait()

    def body(g, carry):
        for u in range(GATHER_UNROLL):
            for k in range(TOP_K):
                dst = pos_ref[0, 0, (g * GATHER_UNROLL + u) * TOP_K + k]
                pltpu.make_async_copy(stage.at[slot, g, pl.ds(u, 1)], xs_hbm.at[pl.ds(dst, 1)],
                                      ssem.at[slot]).start()
        return carry

    lax.fori_loop(0, groups, body, 0)

    @pl.when(i == nsteps - 1)
    def _():
        for back in range(SCATTER_SLOTS - 2, -1, -1):
            @pl.when(i >= back)
            def _():
                scatter_wait(lax.rem(i - back + SCATTER_SLOTS, SCATTER_SLOTS))


def _scatter_rows(vend, pend, pos_tok3, h, nb):
    n = h.shape[0]
    h_groups = h.reshape(n // GATHER_UNROLL, GATHER_UNROLL, HALF)
    grid_spec = pltpu.PrefetchScalarGridSpec(
        num_scalar_prefetch=2,
        grid=(n // SCATTER_TM,),
        in_specs=[
            pl.BlockSpec((1, 1, SCATTER_TM * TOP_K), lambda i, pa, pe: (i, 0, 0), memory_space=pltpu.SMEM),
            pl.BlockSpec(memory_space=pl.ANY),
        ],
        out_specs=pl.BlockSpec(memory_space=pl.ANY),
        scratch_shapes=[pltpu.VMEM((SCATTER_SLOTS, SCATTER_TM // GATHER_UNROLL, GATHER_UNROLL, HALF), jnp.uint32),
                        pltpu.VMEM((ZERO_ROWS, HALF), jnp.uint32),
                        pltpu.SemaphoreType.DMA((SCATTER_SLOTS,)),
                        pltpu.SemaphoreType.DMA((SCATTER_SLOTS,)),
                        pltpu.SemaphoreType.DMA(())],
    )
    return pl.pallas_call(
        _scatter_kernel,
        grid_spec=grid_spec,
        out_shape=jax.ShapeDtypeStruct((nb * BLOCK_TILE_ROWS, HALF), jnp.uint32),
        compiler_params=_cparams(1),
        name="dispatch_scatter",
    )(vend, pend, pos_tok3, h_groups)


def _expert_kernel(be_ref, nused_ref, x_ref, wg_ref, wu_ref, wd_ref, o_ref, wgb, wub, wdb):
    i = pl.program_id(0)
    n_used = nused_ref[0]

    @pl.when(i < n_used)
    def _():
        prev = be_ref[jnp.maximum(i - 1, 0)]

        @pl.when((i == 0) | (be_ref[i] != prev))
        def _():
            wgb[...] = wg_ref[0, 0].astype(bf16)
            wub[...] = wu_ref[0, 0].astype(bf16)
            wdb[...] = wd_ref[0, 0].astype(bf16)

        x_lo, x_hi = _unpack_halves(x_ref[...])
        xb = jnp.concatenate([x_lo.astype(bf16), x_hi.astype(bf16)], axis=-1)
        act = _silu(_dot(xb, wgb[...])) * _dot(xb, wub[...])
        o_ref[...] = _pack_halves(_dot(act.astype(bf16), wdb[...]))

    @pl.when(i >= n_used)
    def _():
        o_ref[...] = jnp.zeros_like(o_ref)


def _expert_mlp(block_e, n_used, xs, wg, wu, wd, layer, nb):
    last = lambda i, nu: jnp.minimum(i, nu[0] - 1)
    wspec = lambda shape: pl.BlockSpec(shape, lambda i, be, nu: (layer, be[last(i, nu)], 0, 0))
    grid_spec = pltpu.PrefetchScalarGridSpec(
        num_scalar_prefetch=2,
        grid=(nb,),
        in_specs=[
            pl.BlockSpec((BLOCK_TILE_ROWS, HALF), lambda i, be, nu: (last(i, nu), 0)),
            wspec((1, 1, D_MODEL, D_EXPERT)), wspec((1, 1, D_MODEL, D_EXPERT)), wspec((1, 1, D_EXPERT, D_MODEL)),
        ],
        out_specs=pl.BlockSpec((BLOCK_TILE_ROWS, HALF), lambda i, be, nu: (i, 0)),
        scratch_shapes=[pltpu.VMEM((D_MODEL, D_EXPERT), bf16), pltpu.VMEM((D_MODEL, D_EXPERT), bf16),
                        pltpu.VMEM((D_EXPERT, D_MODEL), bf16)],
    )
    return pl.pallas_call(
        _expert_kernel,
        grid_spec=grid_spec,
        out_shape=jax.ShapeDtypeStruct((nb * BLOCK_TILE_ROWS, HALF), jnp.uint32),
        compiler_params=_cparams(1),
        name="expert_mlp",
    )(block_e, n_used, xs, wg, wu, wd)


COMB_TM = 64


def _combine_kernel(idx_ref, idxn_ref, wts_ref, h_ref, hb_ref, sg_ref, su_ref, sd_ref, g_ref, b_ref, ys_hbm,
                    hf_ref, hb_out_ref, buf, sem):
    i = pl.program_id(0)
    nb = pl.num_programs(0)
    slot = i % 2
    cnt = COMB_TM * TOP_K

    @pl.when(i == 0)
    def _():
        _start_gather(ys_hbm, idx_ref, buf, sem, 0, cnt)

    @pl.when(i + 1 < nb)
    def _():
        _start_gather(ys_hbm, idxn_ref, buf, sem, 1 - slot, cnt)

    pltpu.make_async_copy(buf.at[slot], buf.at[slot], sem.at[slot]).wait()
    wts = wts_ref[...]
    groups = COMB_TM // GATHER_UNROLL
    routed = None
    for k in range(TOP_K):
        y_lo, y_hi = _unpack_halves(buf[slot, pl.ds(k * groups, groups)].reshape(COMB_TM, HALF))
        term = jnp.concatenate([y_lo, y_hi], axis=-1) * wts[:, k:k + 1]
        routed = term if routed is None else routed + term
    hb = hb_ref[...]
    act = _silu(_dot(hb, sg_ref[...])) * _dot(hb, su_ref[...])
    shared = _dot(act.astype(bf16), sd_ref[...])
    hn = _layer_norm(ALPHA * h_ref[...] + (routed + shared), g_ref[...], b_ref[...])
    hf_ref[...] = hn
    hb_out_ref[...] = hn.astype(bf16)


def _combine(pos3, wts, h, h_bf, ws_gate, ws_up, ws_down, g, b, ys2d):
    n = h.shape[0]
    tm = COMB_TM
    nb = n // tm
    cnt = tm * TOP_K
    rowspec = pl.BlockSpec((tm, D_MODEL), lambda i: (i, 0))
    const2 = lambda i: (0, 0)
    return pl.pallas_call(
        _combine_kernel,
        grid=(nb,),
        in_specs=[
            pl.BlockSpec((1, 1, cnt), lambda i: (i, 0, 0), memory_space=pltpu.SMEM),
            pl.BlockSpec((1, 1, cnt), lambda i: (jnp.minimum(i + 1, nb - 1), 0, 0), memory_space=pltpu.SMEM),
            pl.BlockSpec((tm, TOP_K), lambda i: (i, 0)),
            rowspec, rowspec,
            pl.BlockSpec((D_MODEL, D_SHARED), const2),
            pl.BlockSpec((D_MODEL, D_SHARED), const2),
            pl.BlockSpec((D_SHARED, D_MODEL), const2),
            pl.BlockSpec((1, D_MODEL), const2),
            pl.BlockSpec((1, D_MODEL), const2),
            pl.BlockSpec(memory_space=pl.ANY),
        ],
        out_specs=[rowspec, rowspec],
        out_shape=[jax.ShapeDtypeStruct((n, D_MODEL), f32),
                   jax.ShapeDtypeStruct((n, D_MODEL), bf16)],
        scratch_shapes=[pltpu.VMEM((2, cnt // GATHER_UNROLL, GATHER_UNROLL, HALF), jnp.uint32),
                        pltpu.SemaphoreType.DMA((2,))],
        compiler_params=_cparams(1),
        name="moe_combine",
    )(pos3, pos3, wts, h, h_bf, ws_gate, ws_up, ws_down, g, b, ys2d)


def _dispatch(eidx, rank, counts_f, nb):
    n = eidx.shape[0]
    counts = counts_f[0, :N_EXPERTS].astype(jnp.int32)
    padded = (counts + EXPERT_ROWS - 1) // EXPERT_ROWS * EXPERT_ROWS
    pend = jnp.cumsum(padded)
    pstart = pend - padded
    starts = jnp.arange(nb, dtype=jnp.int32) * EXPERT_ROWS
    block_e = jnp.minimum(jnp.sum((pend[None, :] <= starts[:, None]).astype(jnp.int32), axis=1), N_EXPERTS - 1)
    n_used = (pend[-1] // EXPERT_ROWS).astype(jnp.int32).reshape(1)
    onehot = eidx[:, :, None] == jnp.arange(N_EXPERTS, dtype=jnp.int32)[None, None, :]
    pos = jnp.sum(jnp.where(onehot, pstart[None, None, :], 0), axis=-1) + rank
    pos_tok3 = pos.reshape(n // SCATTER_TM, 1, SCATTER_TM * TOP_K)
    tm = COMB_TM
    pos_k3 = pos.reshape(n // tm, tm, TOP_K).transpose(0, 2, 1).reshape(n // tm, 1, tm * TOP_K)
    return block_e.astype(jnp.int32), n_used, pstart + counts, pend, pos_tok3, pos_k3


def _consts():
    r = jnp.arange(CHUNK)
    tri = (r[:, None] >= r[None, :]).astype(bf16)
    trit = (r[:, None] <= r[None, :]).astype(bf16)
    lane = jnp.arange(LANES)
    ch = jnp.arange(SSD_HEADS * SSD_HEADDIM)
    ssd_expand = (lane[:, None] == (ch[None, :] // SSD_HEADDIM)).astype(bf16)
    hc = jnp.arange(HGRN_HEADS * HGRN_DIM)
    hm_rows = jnp.arange(HGRN_HEADS * SUB)
    hmask = ((hm_rows[:, None] // SUB) == (hc[None, :] // HGRN_DIM)).astype(f32)
    qr = jnp.arange((HGRN_NBLK - 1) * HGRN_HEADS * SUB)
    qblk = qr // (HGRN_HEADS * SUB) + 1
    kstart = SUB * jnp.arange(HGRN_NBLK - 1) * jnp.arange(1, HGRN_NBLK) // 2
    kblk = jnp.sum(lane[None, :] >= kstart[:, None], axis=0)
    offmask = ((qblk[:, None] == kblk[None, :]) & (lane[None, :] < HGRN_OFF_KEYS)).astype(f32)
    rt = jnp.arange(ROUTER_TM)
    ltri = (rt[:, None] > rt[None, :]).astype(bf16)
    return tri, trit, ssd_expand, hmask, offmask, ltri


def _hgrn_lower_bounds(lb_param):
    sm = jax.nn.softmax(lb_param.astype(f32), axis=0)
    return jnp.cumsum(sm, axis=0) - sm[0:1]


def kernel(x, mem, w_in, conv_w, conv_b, dt_bias, a_log, d_skip, ssd_norm, pool_w, pool_scale, hgrn_lb, hgrn_norm, w_mem_kv, w_branch, w_o, ln1_g, ln1_b, w_router, router_bias, w_exp_gate, w_exp_up, w_exp_down, w_sh_gate, w_sh_up, w_sh_down, ln2_g, ln2_b):
    bsz, seq, d = x.shape
    n = bsz * seq
    nk = n * TOP_K
    nb = -(-nk // EXPERT_ROWS) + N_EXPERTS
    tri, trit, ssd_expand, hmask, offmask, ltri = _consts()
    lb_all = _hgrn_lower_bounds(hgrn_lb)

    c_dt0 = BRANCH_WIDTH * 2 + SSD_CONV_DIM
    c_dt1 = c_dt0 + SSD_HEADS
    c_gate = c_dt1 + 5 * BRANCH_WIDTH
    pad_lanes = LANES - SSD_HEADS

    h = x.reshape(n, d)
    h_bf = h.astype(bf16)
    mem_bf = mem.reshape(bsz * MEM_TOKENS, d).astype(bf16)

    w_t = jnp.swapaxes(w_in, 1, 2)
    w_a = w_t[:, :c_dt0, :].astype(bf16)
    w_b = w_t[:, c_dt1:, :].astype(bf16)
    w_dt = jnp.pad(w_t[:, c_dt0:c_dt1, :], ((0, 0), (0, pad_lanes), (0, 0))).astype(bf16)
    w_kv_bf = w_mem_kv.astype(bf16)
    n_b = c_gate - c_dt1

    for l in range(DEPTH):
        proj = _matmul(h_bf, w_a, l, f32, 512, 1024, transposed=True)
        proj_b = _matmul(h_bf, w_b, l, f32, 512, 1024, 0, n_b, transposed=True)
        dtp = _matmul(h_bf, w_dt, l, f32, 512, LANES, transposed=True)
        kv = _matmul(mem_bf, w_kv_bf, l, bf16, 512, 1024)

        br_pool = _pool_mixer(proj, pool_w[l].astype(bf16), pool_scale[l].reshape(1, -1), bsz, seq)
        br_ssd = _ssd_mixer(
            proj, dtp, conv_w[l], conv_b[l].reshape(1, -1),
            jnp.pad(dt_bias[l], (0, pad_lanes)).reshape(1, -1),
            jnp.pad(a_log[l], (0, pad_lanes)).reshape(1, -1),
            jnp.repeat(d_skip[l], SSD_HEADDIM).reshape(1, -1),
            ssd_norm[l].reshape(1, -1), (tri, trit, ssd_expand), bsz, seq)
        br_hgrn = _hgrn_mixer(proj_b, lb_all[l].reshape(1, -1), hgrn_norm[l].reshape(1, -1),
                              (tri, hmask, offmask), bsz, seq)
        br_mem = _mem_attention(proj_b, kv, bsz, seq)

        merged = _merge(h_bf, (br_pool, br_ssd, br_hgrn, br_mem), w_b, l, n_b, w_branch[l].astype(bf16))
        h, h_bf, h_pk = _wo_ln(merged, w_o[l].astype(bf16), h, ln1_g[l].reshape(1, -1), ln1_b[l].reshape(1, -1))

        eidx, wts, rank, counts = _router(h, jnp.pad(w_router[l], ((0, 0), (0, LANES - N_EXPERTS))),
                                          jnp.pad(router_bias[l], (0, LANES - N_EXPERTS)).reshape(1, -1), ltri)
        block_e, n_used, vend, pend, pos_tok3, pos_k3 = _dispatch(eidx, rank, counts, nb)
        xs = _scatter_rows(vend, pend, pos_tok3, h_pk, nb)
        ys = _expert_mlp(block_e, n_used, xs, w_exp_gate, w_exp_up, w_exp_down, l, nb)
        h, h_bf = _combine(pos_k3, wts, h, h_bf, w_sh_gate[l].astype(bf16), w_sh_up[l].astype(bf16),
                                   w_sh_down[l].astype(bf16), ln2_g[l].reshape(1, -1), ln2_b[l].reshape(1, -1), ys)
    return h.reshape(bsz, seq, d)
```

```python
import functools

import jax
import jax.numpy as jnp
from jax import lax
from jax.experimental import pallas as pl
from jax.experimental.pallas import tpu as pltpu

f32 = jnp.float32
bf16 = jnp.bfloat16

D_MODEL = 2048
DEPTH = 4
CHUNK = 64
N_BRANCH = 4
BRANCH_WIDTH = 1024
POOL_WINDOWS = (2, 4, 8, 16)
POOL_GROUP = 256
SSD_HEADS = 16
SSD_HEADDIM = 64
SSD_STATE = 128
SSD_GROUPS = 4
SSD_CONV = 4
SSD_CONV_DIM = 2048
HGRN_HEADS = 8
HGRN_DIM = 128
MEM_TOKENS = 256
MEM_HEADS = 4
MEM_HEAD_DIM = 256
N_EXPERTS = 64
TOP_K = 8
N_GROUPS = 8
TOPK_GROUPS = 4
D_EXPERT = 256
D_SHARED = 256
ROUTED_SCALE = 2.5
MASK_SCORE = -1e4
ALPHA = (2 * DEPTH) ** 0.25
EPS = 1e-5

LANES = 128
ROW_TILES = 1
EXPERT_ROWS = 512
ZERO_ROWS = 128
SUB = 16
VMEM_LIMIT = 56 * 1024 * 1024

NEG_INF = float("-inf")


def _cparams(n_axes):
    return pltpu.CompilerParams(dimension_semantics=("arbitrary",) * n_axes,
                                vmem_limit_bytes=VMEM_LIMIT)


def _sigmoid(x):
    return jax.nn.sigmoid(x)


def _silu(x):
    return x * jax.nn.sigmoid(x)


def _split3(x):
    hi = x.astype(bf16)
    r1 = x - hi.astype(f32)
    mid = r1.astype(bf16)
    lo = (r1 - mid.astype(f32)).astype(bf16)
    return hi, mid, lo


def _dot(a, b):
    return jnp.dot(a, b, preferred_element_type=f32)


def _dot_nt(a, b):
    return lax.dot_general(a, b, (((1,), (1,)), ((), ())), preferred_element_type=f32)


def _dot_tn(a, b):
    return lax.dot_general(a, b, (((0,), (0,)), ((), ())), preferred_element_type=f32)


def _sel_left(sel01, x):
    return sum(_dot(sel01, p) for p in _split3(x))


def _sel_right(x, sel01):
    return sum(_dot(p, sel01) for p in _split3(x))


def _mm_kernel(x_ref, w_ref, o_ref):
    o_ref[...] = _dot(x_ref[...], w_ref[0]).astype(o_ref.dtype)


def _mm_nt_kernel(x_ref, wt_ref, o_ref):
    o_ref[...] = _dot_nt(x_ref[...], wt_ref[0]).astype(o_ref.dtype)


def _matmul(x, w, layer, out_dtype, tm, tn, col0=0, ncols=None, transposed=False):
    m, k = x.shape
    n = w.shape[1 if transposed else 2] if ncols is None else ncols
    cb = col0 // tn
    if transposed:
        wspec = pl.BlockSpec((1, tn, k), lambda j, i: (layer, cb + j, 0))
    else:
        wspec = pl.BlockSpec((1, k, tn), lambda j, i: (layer, 0, cb + j))
    return pl.pallas_call(
        _mm_nt_kernel if transposed else _mm_kernel,
        grid=(n // tn, m // tm),
        in_specs=[pl.BlockSpec((tm, k), lambda j, i: (i, 0)), wspec],
        out_specs=pl.BlockSpec((tm, tn), lambda j, i: (i, j)),
        out_shape=jax.ShapeDtypeStruct((m, n), out_dtype),
        compiler_params=_cparams(2),
        name="matmul",
    )(x, w)


POOL_T = 512
POOL_HALO = 16


def _pool_kernel(u_ref, halo_ref, w_ref, scale_ref, o_ref):
    i = pl.program_id(1)
    u = u_ref[...]
    halo = jnp.where(i > 0, halo_ref[...], 0.0)
    ext = jnp.concatenate([halo, u], axis=0)
    t = i * POOL_T + lax.broadcasted_iota(jnp.int32, (POOL_T, 1), 0)
    outs = []
    for gi, w in enumerate(POOL_WINDOWS):
        xg = ext[:, gi * POOL_GROUP:(gi + 1) * POOL_GROUP]
        s = xg
        span = 1
        while span < w:
            s = s[span:, :] + s[:-span, :]
            span *= 2
        s = s[s.shape[0] - POOL_T:, :]
        cnt = jnp.minimum(t + 1, w).astype(f32)
        d = s / cnt - u[:, gi * POOL_GROUP:(gi + 1) * POOL_GROUP]
        outs.append(_dot(d.astype(bf16), w_ref[gi]))
    y = jnp.concatenate(outs, axis=-1) * scale_ref[...]
    o_ref[...] = y.astype(o_ref.dtype)


def _pool_mixer(proj, pool_w, pool_scale, bsz, seq):
    n = bsz * seq
    nt = seq // POOL_T
    return pl.pallas_call(
        _pool_kernel,
        grid=(bsz, nt),
        in_specs=[
            pl.BlockSpec((POOL_T, BRANCH_WIDTH), lambda b, i: (b * nt + i, 0)),
            pl.BlockSpec((POOL_HALO, BRANCH_WIDTH),
                         lambda b, i: (jnp.maximum((b * nt + i) * (POOL_T // POOL_HALO) - 1, 0), 0)),
            pl.BlockSpec((len(POOL_WINDOWS), POOL_GROUP, POOL_GROUP), lambda b, i: (0, 0, 0)),
            pl.BlockSpec((1, BRANCH_WIDTH), lambda b, i: (0, 0)),
        ],
        out_specs=pl.BlockSpec((POOL_T, BRANCH_WIDTH), lambda b, i: (b * nt + i, 0)),
        out_shape=jax.ShapeDtypeStruct((n, BRANCH_WIDTH), bf16),
        compiler_params=_cparams(2),
        name="pool_mixer",
    )(proj, proj, pool_w, pool_scale)


CONV_TAIL = 8


def _softplus(x):
    return jnp.maximum(x, 0.0) + jnp.log1p(jnp.exp(-jnp.abs(x)))


def _ssd_kernel(z_ref, xbc_ref, dt_ref, cw_ref, cb_ref, dtb_ref, alog_ref, dskip_ref, nw_ref,
                tri_ref, trit_ref, exp_ref, o_ref, state_ref, tail_ref):
    i = pl.program_id(0)

    @pl.when(i == 0)
    def _():
        state_ref[...] = jnp.zeros_like(state_ref)
        tail_ref[...] = jnp.zeros_like(tail_ref)

    for bi in range(z_ref.shape[0]):
        _ssd_chunk(z_ref.at[bi], xbc_ref.at[bi], dt_ref.at[bi], cw_ref, cb_ref, dtb_ref, alog_ref, dskip_ref, nw_ref,
                   tri_ref, trit_ref, exp_ref, o_ref.at[bi], state_ref.at[bi], tail_ref.at[bi])


def _ssd_chunk(z_ref, xbc_ref, dt_ref, cw_ref, cb_ref, dtb_ref, alog_ref, dskip_ref, nw_ref,
               tri_ref, trit_ref, exp_ref, o_ref, state_ref, tail_ref):
    xbc = xbc_ref[...]
    ext = jnp.concatenate([tail_ref[...], xbc], axis=0)
    cw = cw_ref[...]
    conv = cb_ref[...]
    for k in range(SSD_CONV):
        off = CONV_TAIL - (SSD_CONV - 1) + k
        conv = conv + cw[k:k + 1, :] * ext[off:off + CHUNK, :]
    tail_ref[...] = xbc[CHUNK - CONV_TAIL:, :]
    act = _silu(conv)
    inner = SSD_HEADS * SSD_HEADDIM
    gw = SSD_GROUPS * SSD_STATE
    xs = act[:, :inner]
    bmat = act[:, inner:inner + gw]
    cmat = act[:, inner + gw:]

    dt = _softplus(dt_ref[...] + dtb_ref[...])
    a = -jnp.exp(alog_ref[...])
    ad = dt * a
    ad_parts = _split3(ad)
    tri = tri_ref[...]
    acs = sum(_dot(tri, p) for p in ad_parts)
    acs_t = sum(_dot_tn(p, trit_ref[...]) for p in ad_parts)
    a_last = acs[CHUNK - 1:CHUNK, :]
    expand = exp_ref[...]
    dt_e = _sel_right(dt, expand)
    eacs_e = _sel_right(jnp.exp(acs), expand)
    dec_e = _sel_right(jnp.exp(a_last - acs), expand)
    x = xs * dt_e
    xb = x.astype(bf16)
    xd = (x * dec_e).astype(bf16)
    elast_e = eacs_e[CHUNK - 1:CHUNK, :]

    row = lax.broadcasted_iota(jnp.int32, (CHUNK, CHUNK), 0)
    col = lax.broadcasted_iota(jnp.int32, (CHUNK, CHUNK), 1)
    causal = row >= col
    hpg = SSD_HEADS // SSD_GROUPS
    gch = hpg * SSD_HEADDIM
    y_diag = []
    y_off = []
    for g in range(SSD_GROUPS):
        bg = bmat[:, g * SSD_STATE:(g + 1) * SSD_STATE].astype(bf16)
        cg = cmat[:, g * SSD_STATE:(g + 1) * SSD_STATE].astype(bf16)
        cb = _dot_nt(cg, bg)
        for hh in range(hpg):
            h = g * hpg + hh
            diff = acs[:, h:h + 1] - acs_t[h:h + 1, :]
            decay = jnp.where(causal, jnp.exp(jnp.where(causal, diff, 0.0)), 0.0)
            sc = (cb * decay).astype(bf16)
            y_diag.append(_dot(sc, xb[:, h * SSD_HEADDIM:(h + 1) * SSD_HEADDIM]))
        st = state_ref[:, g * gch:(g + 1) * gch]
        y_off.append(_dot(cg, st.astype(bf16)))
        upd = _dot_tn(bg, xd[:, g * gch:(g + 1) * gch])
        state_ref[:, g * gch:(g + 1) * gch] = st * elast_e[:, g * gch:(g + 1) * gch] + upd
    y = (jnp.concatenate(y_diag, axis=-1) + jnp.concatenate(y_off, axis=-1) * eacs_e
         + xs * dskip_ref[...])
    y = y * _silu(z_ref[...])
    nw = nw_ref[...]
    ngw = inner // SSD_GROUPS
    outs = []
    for g in range(SSD_GROUPS):
        yg = y[:, g * ngw:(g + 1) * ngw]
        ms = jnp.mean(yg * yg, axis=-1, keepdims=True)
        outs.append(yg * lax.rsqrt(ms + EPS) * nw[:, g * ngw:(g + 1) * ngw])
    o_ref[...] = jnp.concatenate(outs, axis=-1).astype(o_ref.dtype)


def _ssd_mixer(proj, dtp, conv_w, conv_b, dt_bias, a_log, d_skip_e, norm_w, consts, bsz, seq):
    n = bsz * seq
    nc = seq // CHUNK
    tri, trit, expand = consts
    const2 = lambda i: (0, 0)
    inner = SSD_HEADS * SSD_HEADDIM
    proj3 = proj.reshape(bsz, seq, proj.shape[1])
    dtp3 = dtp.reshape(bsz, seq, LANES)
    out = pl.pallas_call(
        _ssd_kernel,
        grid=(nc,),
        in_specs=[
            pl.BlockSpec((bsz, CHUNK, inner), lambda i: (0, i, 1)),
            pl.BlockSpec((bsz, CHUNK, SSD_CONV_DIM), lambda i: (0, i, 1)),
            pl.BlockSpec((bsz, CHUNK, LANES), lambda i: (0, i, 0)),
            pl.BlockSpec((SSD_CONV, SSD_CONV_DIM), const2),
            pl.BlockSpec((1, SSD_CONV_DIM), const2),
            pl.BlockSpec((1, LANES), const2),
            pl.BlockSpec((1, LANES), const2),
            pl.BlockSpec((1, inner), const2),
            pl.BlockSpec((1, inner), const2),
            pl.BlockSpec((CHUNK, CHUNK), const2),
            pl.BlockSpec((CHUNK, CHUNK), const2),
            pl.BlockSpec((LANES, inner), const2),
        ],
        out_specs=pl.BlockSpec((bsz, CHUNK, inner), lambda i: (0, i, 0)),
        out_shape=jax.ShapeDtypeStruct((bsz, seq, inner), bf16),
        scratch_shapes=[pltpu.VMEM((bsz, SSD_STATE, inner), f32),
                        pltpu.VMEM((bsz, CONV_TAIL, SSD_CONV_DIM), f32)],
        compiler_params=_cparams(1),
        name="ssd_mixer",
    )(proj3, proj3, dtp3, conv_w, conv_b, dt_bias, a_log, d_skip_e, norm_w, tri, trit, expand)
    return out.reshape(n, inner)


LOG2E = 1.4426950408889634
HGRN_NBLK = CHUNK // SUB
HGRN_OFF_KEYS = SUB * (HGRN_NBLK - 1) * HGRN_NBLK // 2


def _hgrn_chunk(q, f, v, g, lb, nw, tri, hmask, offmask, state_ref):
    width = HGRN_HEADS * HGRN_DIM
    qf = _silu(q)
    kf = (1.0 - lb) * _sigmoid(-f)
    b2 = _sel_left(tri, jnp.log1p(-kf)) * LOG2E
    c2 = b2 - jnp.log2(kf)
    b_last = b2[CHUNK - 1:CHUNK, :]
    vb = v.astype(bf16)

    qd = (qf * jnp.exp2(b2)).astype(bf16)
    kd = jnp.exp2(b_last - c2).astype(bf16)
    eb_last = jnp.exp2(b_last)
    o_inter = []
    for h in range(HGRN_HEADS):
        sl = slice(h * HGRN_DIM, (h + 1) * HGRN_DIM)
        st = state_ref[h]
        o_inter.append(_dot_nt(qd[:, sl], st.astype(bf16)))
        state_ref[h] = st * eb_last[:, sl] + _dot_tn(vb[:, sl], kd[:, sl])
    o = jnp.concatenate(o_inter, axis=-1)

    qm, kk, vv = [], [], []
    for blk in range(1, HGRN_NBLK):
        r0 = blk * SUB
        anchor = b2[r0 - 1:r0, :]
        qq = qf[r0:r0 + SUB, :] * jnp.exp2(b2[r0:r0 + SUB, :] - anchor)
        qm.append((jnp.concatenate([qq] * HGRN_HEADS, axis=0) * hmask).astype(bf16))
        kk.append(jnp.exp2(anchor - c2[:r0, :]).astype(bf16))
        vv.append(vb[:r0, :])
    zpad = jnp.zeros((LANES - HGRN_OFF_KEYS, width), bf16)
    att = _dot_nt(jnp.concatenate(qm, axis=0), jnp.concatenate(kk + [zpad], axis=0))
    res = _dot((att * offmask).astype(bf16), jnp.concatenate(vv + [zpad], axis=0))
    rows = [jnp.zeros((SUB, width), f32)]
    for blk in range(1, HGRN_NBLK):
        base = (blk - 1) * HGRN_HEADS * SUB
        acc = None
        for h in range(HGRN_HEADS):
            part = res[base + h * SUB:base + (h + 1) * SUB, :] * hmask[h * SUB:(h + 1) * SUB, :]
            acc = part if acc is None else acc + part
        rows.append(acc)
    o = o + jnp.concatenate(rows, axis=0)

    tpos = lax.broadcasted_iota(jnp.int32, (CHUNK, 1), 0) & (SUB - 1)
    diag = [None] * HGRN_HEADS
    for d in range(SUB):
        cs = c2 if d == 0 else pltpu.roll(c2, d, 0)
        vs = v if d == 0 else pltpu.roll(v, d, 0)
        p = qf * jnp.exp2(b2 - cs)
        same_block = tpos >= d
        for h in range(HGRN_HEADS):
            sl = slice(h * HGRN_DIM, (h + 1) * HGRN_DIM)
            att = jnp.where(same_block, jnp.sum(p[:, sl], axis=-1, keepdims=True), 0.0)
            term = att * vs[:, sl]
            diag[h] = term if diag[h] is None else diag[h] + term
    o = o + jnp.concatenate(diag, axis=-1)

    gate = _silu(g)
    outs = []
    for h in range(HGRN_HEADS):
        sl = slice(h * HGRN_DIM, (h + 1) * HGRN_DIM)
        oh = o[:, sl]
        ms = jnp.mean(oh * oh, axis=-1, keepdims=True)
        outs.append(oh * lax.rsqrt(ms + EPS) * nw * gate[:, sl])
    return jnp.concatenate(outs, axis=-1)


def _hgrn_kernel(q_ref, f_ref, i_ref, g_ref, lb_ref, nw_ref, tri_ref, hmask_ref, offmask_ref, o_ref, state_ref):
    c = pl.program_id(0)

    @pl.when(c == 0)
    def _():
        state_ref[...] = jnp.zeros_like(state_ref)

    for bi in range(q_ref.shape[0]):
        out = _hgrn_chunk(q_ref[bi], f_ref[bi], i_ref[bi], g_ref[bi], lb_ref[...], nw_ref[...], tri_ref[...],
                          hmask_ref[...], offmask_ref[...], state_ref.at[bi])
        o_ref[bi] = out.astype(o_ref.dtype)


def _hgrn_mixer(proj, lb, norm_w, consts, bsz, seq):
    n = bsz * seq
    nc = seq // CHUNK
    tri, hmask, offmask = consts
    width = HGRN_HEADS * HGRN_DIM
    const2 = lambda i: (0, 0)
    col = lambda j: (lambda i: (0, i, j))
    proj3 = proj.reshape(bsz, seq, proj.shape[1])
    qrows = (HGRN_NBLK - 1) * HGRN_HEADS * SUB
    out = pl.pallas_call(
        _hgrn_kernel,
        grid=(nc,),
        in_specs=[
            pl.BlockSpec((bsz, CHUNK, width), col(0)),
            pl.BlockSpec((bsz, CHUNK, width), col(1)),
            pl.BlockSpec((bsz, CHUNK, width), col(2)),
            pl.BlockSpec((bsz, CHUNK, width), col(3)),
            pl.BlockSpec((1, width), const2),
            pl.BlockSpec((1, HGRN_DIM), const2),
            pl.BlockSpec((CHUNK, CHUNK), const2),
            pl.BlockSpec((HGRN_HEADS * SUB, width), const2),
            pl.BlockSpec((qrows, LANES), const2),
        ],
        out_specs=pl.BlockSpec((bsz, CHUNK, width), lambda i: (0, i, 0)),
        out_shape=jax.ShapeDtypeStruct((bsz, seq, width), bf16),
        scratch_shapes=[pltpu.VMEM((bsz, HGRN_HEADS, HGRN_DIM, HGRN_DIM), f32)],
        compiler_params=_cparams(1),
        name="hgrn_mixer",
    )(proj3, proj3, proj3, proj3, lb, norm_w, tri, hmask, offmask)
    return out.reshape(n, width)


MEM_T = 512


def _memattn_kernel(q_ref, k_ref, v_ref, o_ref):
    q = q_ref[...].astype(bf16)
    outs = []
    for h in range(MEM_HEADS):
        sl = slice(h * MEM_HEAD_DIM, (h + 1) * MEM_HEAD_DIM)
        s = _dot_nt(q[:, sl], k_ref[:, sl]) * (MEM_HEAD_DIM ** -0.5)
        m = jnp.max(s, axis=-1, keepdims=True)
        e = jnp.exp(s - m)
        p = e / jnp.sum(e, axis=-1, keepdims=True)
        outs.append(_dot(p.astype(bf16), v_ref[:, sl]))
    o_ref[...] = jnp.concatenate(outs, axis=-1).astype(o_ref.dtype)


def _mem_attention(proj, kv, bsz, seq):
    n = bsz * seq
    nt = seq // MEM_T
    width = MEM_HEADS * MEM_HEAD_DIM
    return pl.pallas_call(
        _memattn_kernel,
        grid=(bsz, nt),
        in_specs=[
            pl.BlockSpec((MEM_T, width), lambda b, i: (b * nt + i, 4)),
            pl.BlockSpec((MEM_TOKENS, width), lambda b, i: (b, 0)),
            pl.BlockSpec((MEM_TOKENS, width), lambda b, i: (b, 1)),
        ],
        out_specs=pl.BlockSpec((MEM_T, width), lambda b, i: (b * nt + i, 0)),
        out_shape=jax.ShapeDtypeStruct((n, width), bf16),
        compiler_params=_cparams(2),
        name="mem_attention",
    )(proj, kv, kv)


MERGE_TM = 512
MERGE_TN = 512


def _merge_kernel(h_ref, b0_ref, b1_ref, b2_ref, b3_ref, g0_ref, g1_ref, g2_ref, g3_ref, wb_ref, o_ref):
    h = h_ref[...]
    acc = None
    for n, (br, wg) in enumerate(zip((b0_ref, b1_ref, b2_ref, b3_ref), (g0_ref, g1_ref, g2_ref, g3_ref))):
        gate = _sigmoid(_dot_nt(h, wg[0]))
        term = gate * _dot(br[...], wb_ref[n])
        acc = term if acc is None else acc + term
    o_ref[...] = acc.astype(o_ref.dtype)


def _merge(h_bf, branches, w_packed, layer, gate_col0, w_branch):
    n = h_bf.shape[0]
    tm, tn = MERGE_TM, MERGE_TN
    bspec = pl.BlockSpec((tm, BRANCH_WIDTH), lambda j, i: (i, 0))
    gspec = lambda nbr: pl.BlockSpec((1, tn, D_MODEL),
                                     lambda j, i: (layer, (gate_col0 + nbr * D_MODEL) // tn + j, 0))
    return pl.pallas_call(
        _merge_kernel,
        grid=(D_MODEL // tn, n // tm),
        in_specs=[pl.BlockSpec((tm, D_MODEL), lambda j, i: (i, 0)), bspec, bspec, bspec, bspec,
                  gspec(0), gspec(1), gspec(2), gspec(3),
                  pl.BlockSpec((N_BRANCH, BRANCH_WIDTH, tn), lambda j, i: (0, 0, j))],
        out_specs=pl.BlockSpec((tm, tn), lambda j, i: (i, j)),
        out_shape=jax.ShapeDtypeStruct((n, D_MODEL), bf16),
        compiler_params=_cparams(2),
        name="gated_merge",
    )(h_bf, *branches, w_packed, w_packed, w_packed, w_packed, w_branch)


def _layer_norm(x, g, b):
    mu = jnp.mean(x, axis=-1, keepdims=True)
    xc = x - mu
    var = jnp.mean(xc * xc, axis=-1, keepdims=True)
    return xc * lax.rsqrt(var + EPS) * g + b


WO_TM = 256


HALF = D_MODEL // 2
U16_HI = 0xFFFF0000


def _pack_halves(x):
    lo = lax.bitcast_convert_type(x[:, :HALF].astype(bf16).astype(f32), jnp.uint32)
    hi = lax.bitcast_convert_type(x[:, HALF:].astype(bf16).astype(f32), jnp.uint32)
    return lax.shift_right_logical(lo, jnp.uint32(16)) | (hi & jnp.uint32(U16_HI))


def _unpack_halves(w):
    lo = lax.bitcast_convert_type(lax.shift_left(w, jnp.uint32(16)), f32)
    hi = lax.bitcast_convert_type(w & jnp.uint32(U16_HI), f32)
    return lo, hi


def _wo_ln_kernel(m_ref, w_ref, h_ref, g_ref, b_ref, hf_ref, hb_ref, hp_ref):
    y = ALPHA * h_ref[...] + _dot(m_ref[...], w_ref[...])
    hn = _layer_norm(y, g_ref[...], b_ref[...])
    hf_ref[...] = hn
    hb_ref[...] = hn.astype(bf16)
    hp_ref[...] = _pack_halves(hn)


def _wo_ln(merged, w_o, h, g, b):
    n = h.shape[0]
    tm = WO_TM
    rowspec = pl.BlockSpec((tm, D_MODEL), lambda i: (i, 0))
    const2 = lambda i: (0, 0)
    return pl.pallas_call(
        _wo_ln_kernel,
        grid=(n // tm,),
        in_specs=[rowspec, pl.BlockSpec((D_MODEL, D_MODEL), const2), rowspec,
                  pl.BlockSpec((1, D_MODEL), const2), pl.BlockSpec((1, D_MODEL), const2)],
        out_specs=[rowspec, rowspec, pl.BlockSpec((tm, HALF), lambda i: (i, 0))],
        out_shape=[jax.ShapeDtypeStruct((n, D_MODEL), f32),
                   jax.ShapeDtypeStruct((n, D_MODEL), bf16),
                   jax.ShapeDtypeStruct((n, HALF), jnp.uint32)],
        compiler_params=_cparams(1),
        name="wo_layernorm",
    )(merged, w_o, h, g, b)


ROUTER_TM = 256


def _router_kernel(h_ref, w_ref, bias_ref, ltri_ref, eidx_ref, wts_ref, rank_ref, counts_ref, cnt_ref):
    step = pl.program_id(0)

    @pl.when(step == 0)
    def _():
        cnt_ref[...] = jnp.zeros_like(cnt_ref)

    h = h_ref[...]
    w = w_ref[...]
    h_hi = h.astype(bf16)
    h_lo = (h - h_hi.astype(f32)).astype(bf16)
    w_hi = w.astype(bf16)
    w_lo = (w - w_hi.astype(f32)).astype(bf16)
    logits = _dot(h_hi, w_hi) + _dot(h_hi, w_lo) + _dot(h_lo, w_hi)
    scores = _sigmoid(logits)
    lane_i = lax.broadcasted_iota(jnp.int32, logits.shape, 1)
    lane = lane_i.astype(f32)
    real = lane_i < N_EXPERTS
    choice = jnp.where(real, scores + bias_ref[...], NEG_INF)
    gsize = N_EXPERTS // N_GROUPS
    grp = lax.shift_right_logical(lane_i, gsize.bit_length() - 1)
    big = float(LANES)

    def first_argmax(x):
        m = jnp.max(x, axis=-1, keepdims=True)
        idx = jnp.min(jnp.where(x == m, lane, big), axis=-1, keepdims=True)
        return m, idx

    gscore = []
    for g in range(N_GROUPS):
        cg = jnp.where(grp == g, choice, NEG_INF)
        m1, i1 = first_argmax(cg)
        m2 = jnp.max(jnp.where(lane == i1, NEG_INF, cg), axis=-1, keepdims=True)
        gscore.append(m1 + m2)
    keep = jnp.zeros(logits.shape, f32)
    for g in range(N_GROUPS):
        rank = jnp.zeros_like(gscore[g])
        for o in range(N_GROUPS):
            if o == g:
                continue
            ahead = (gscore[o] > gscore[g]) if o > g else (gscore[o] >= gscore[g])
            rank = rank + jnp.where(ahead, 1.0, 0.0)
        keep = jnp.where((grp == g) & (rank < TOPK_GROUPS), 1.0, keep)
    cur = jnp.where(real, jnp.where(keep > 0.0, choice, MASK_SCORE), NEG_INF)
    idxs, ws, hits = [], [], []
    sel = jnp.zeros(logits.shape, f32)
    for _ in range(TOP_K):
        _, ik = first_argmax(cur)
        hit = lane == ik
        ws.append(jnp.sum(jnp.where(hit, scores, 0.0), axis=-1, keepdims=True))
        idxs.append(ik)
        hits.append(hit)
        sel = jnp.where(hit, 1.0, sel)
        cur = jnp.where(hit, NEG_INF, cur)
    wsum = ws[0]
    for wk in ws[1:]:
        wsum = wsum + wk
    eidx_ref[...] = jnp.concatenate(idxs, axis=-1).astype(jnp.int32)
    wts_ref[...] = jnp.concatenate([wk / wsum * ROUTED_SCALE for wk in ws], axis=-1)
    before = _dot(ltri_ref[...], sel.astype(bf16)) + cnt_ref[...]
    ranks = [jnp.sum(jnp.where(hit, before, 0.0), axis=-1, keepdims=True) for hit in hits]
    rank_ref[...] = jnp.concatenate(ranks, axis=-1).astype(jnp.int32)
    cnt_ref[...] = cnt_ref[...] + jnp.sum(sel, axis=0, keepdims=True)
    counts_ref[...] = cnt_ref[...]


def _router(h, w_router_p, bias_p, ltri):
    n = h.shape[0]
    tm = ROUTER_TM
    const2 = lambda i: (0, 0)
    kspec = pl.BlockSpec((tm, TOP_K), lambda i: (i, 0))
    return pl.pallas_call(
        _router_kernel,
        grid=(n // tm,),
        in_specs=[pl.BlockSpec((tm, D_MODEL), lambda i: (i, 0)),
                  pl.BlockSpec((D_MODEL, LANES), const2),
                  pl.BlockSpec((1, LANES), const2),
                  pl.BlockSpec((tm, tm), const2)],
        out_specs=[kspec, kspec, kspec, pl.BlockSpec((1, LANES), const2)],
        out_shape=[jax.ShapeDtypeStruct((n, TOP_K), jnp.int32),
                   jax.ShapeDtypeStruct((n, TOP_K), f32),
                   jax.ShapeDtypeStruct((n, TOP_K), jnp.int32),
                   jax.ShapeDtypeStruct((1, LANES), f32)],
        scratch_shapes=[pltpu.VMEM((1, LANES), f32)],
        compiler_params=_cparams(1),
        name="router",
    )(h, w_router_p, bias_p, ltri)


GATHER_UNROLL = 8


def _start_gather(src_hbm, idx_ref, buf, sem, slot, count):
    def body(o, carry):
        for u in range(GATHER_UNROLL):
            pltpu.make_async_copy(src_hbm.at[pl.ds(idx_ref[0, 0, o * GATHER_UNROLL + u], 1)],
                                  buf.at[slot, o, pl.ds(u, 1)], sem.at[slot]).start()
        return carry
    lax.fori_loop(0, count // GATHER_UNROLL, body, 0)


SCATTER_TM = 128
SCATTER_SLOTS = 3
BLOCK_TILE_ROWS = EXPERT_ROWS * ROW_TILES


def _scatter_kernel(vend_ref, pend_ref, pos_ref, hrows_hbm, xs_hbm, stage, zbuf, lsem, ssem, zsem):
    i = pl.program_id(0)
    nsteps = pl.num_programs(0)
    cnt = SCATTER_TM * TOP_K

    @pl.when(i == 0)
    def _():
        zbuf[...] = jnp.zeros_like(zbuf)

        def piece(r0):
            return pltpu.make_async_copy(zbuf, xs_hbm.at[pl.ds(pl.multiple_of(r0, ZERO_ROWS), ZERO_ROWS)], zsem)

        def span(e):
            first = lax.shift_right_logical(vend_ref[e], ZERO_ROWS.bit_length() - 1)
            last = lax.shift_right_logical(pend_ref[e], ZERO_ROWS.bit_length() - 1)
            return first, last

        def start(e, carry):
            first, last = span(e)
            lax.fori_loop(first, last, lambda p, c: (piece(p * ZERO_ROWS).start(), c)[1], 0)
            return carry

        def wait(e, carry):
            first, last = span(e)
            lax.fori_loop(first, last, lambda p, c: (piece(p * ZERO_ROWS).wait(), c)[1], 0)
            return carry

        tail0 = lax.shift_right_logical(pend_ref[N_EXPERTS - 1], ZERO_ROWS.bit_length() - 1)
        tail1 = xs_hbm.shape[0] // ZERO_ROWS
        lax.fori_loop(0, N_EXPERTS, start, 0)
        lax.fori_loop(tail0, tail1, lambda p, c: (piece(p * ZERO_ROWS).start(), c)[1], 0)
        lax.fori_loop(0, N_EXPERTS, wait, 0)
        lax.fori_loop(tail0, tail1, lambda p, c: (piece(p * ZERO_ROWS).wait(), c)[1], 0)

    groups = SCATTER_TM // GATHER_UNROLL
    slot = lax.rem(i, SCATTER_SLOTS)
    nxt = lax.rem(i + 1, SCATTER_SLOTS)

    def load(j, s):
        src = pl.multiple_of(j * groups, groups)
        return pltpu.make_async_copy(hrows_hbm.at[pl.ds(src, groups)], stage.at[s], lsem.at[s])

    def scatter_wait(s):
        pltpu.make_async_copy(xs_hbm.at[pl.ds(0, cnt * ROW_TILES)], xs_hbm.at[pl.ds(0, cnt * ROW_TILES)],
                              ssem.at[s]).wait()

    @pl.when(i == 0)
    def _():
        load(0, 0).start()

    @pl.when(i >= SCATTER_SLOTS - 1)
    def _():
        scatter_wait(nxt)

    @pl.when(i + 1 < nsteps)
    def _():
        load(i + 1, nxt).start()

    load(i, slot).wait()

    def body(g, carry):
        for u in range(GATHER_UNROLL):
            for k in range(TOP_K):
                dst = pos_ref[0, 0, (g * GATHER_UNROLL + u) * TOP_K + k]
                pltpu.make_async_copy(stage.at[slot, g, pl.ds(u, 1)], xs_hbm.at[pl.ds(dst, 1)],
                                      ssem.at[slot]).start()
        return carry

    lax.fori_loop(0, groups, body, 0)

    @pl.when(i == nsteps - 1)
    def _():
        for back in range(SCATTER_SLOTS - 2, -1, -1):
            @pl.when(i >= back)
            def _():
                scatter_wait(lax.rem(i - back + SCATTER_SLOTS, SCATTER_SLOTS))


def _scatter_rows(vend, pend, pos_tok3, h, nb):
    n = h.shape[0]
    h_groups = h.reshape(n // GATHER_UNROLL, GATHER_UNROLL, HALF)
    grid_spec = pltpu.PrefetchScalarGridSpec(
        num_scalar_prefetch=2,
        grid=(n // SCATTER_TM,),
        in_specs=[
            pl.BlockSpec((1, 1, SCATTER_TM * TOP_K), lambda i, pa, pe: (i, 0, 0), memory_space=pltpu.SMEM),
            pl.BlockSpec(memory_space=pl.ANY),
        ],
        out_specs=pl.BlockSpec(memory_space=pl.ANY),
        scratch_shapes=[pltpu.VMEM((SCATTER_SLOTS, SCATTER_TM // GATHER_UNROLL, GATHER_UNROLL, HALF), jnp.uint32),
                        pltpu.VMEM((ZERO_ROWS, HALF), jnp.uint32),
                        pltpu.SemaphoreType.DMA((SCATTER_SLOTS,)),
                        pltpu.SemaphoreType.DMA((SCATTER_SLOTS,)),
                        pltpu.SemaphoreType.DMA(())],
    )
    return pl.pallas_call(
        _scatter_kernel,
        grid_spec=grid_spec,
        out_shape=jax.ShapeDtypeStruct((nb * BLOCK_TILE_ROWS, HALF), jnp.uint32),
        compiler_params=_cparams(1),
        name="dispatch_scatter",
    )(vend, pend, pos_tok3, h_groups)


def _expert_kernel(be_ref, nused_ref, x_ref, wg_ref, wu_ref, wd_ref, o_ref, wgb, wub, wdb):
    i = pl.program_id(0)
    n_used = nused_ref[0]

    @pl.when(i < n_used)
    def _():
        prev = be_ref[jnp.maximum(i - 1, 0)]

        @pl.when((i == 0) | (be_ref[i] != prev))
        def _():
            wgb[...] = wg_ref[0, 0].astype(bf16)
            wub[...] = wu_ref[0, 0].astype(bf16)
            wdb[...] = wd_ref[0, 0].astype(bf16)

        x_lo, x_hi = _unpack_halves(x_ref[...])
        xb = jnp.concatenate([x_lo.astype(bf16), x_hi.astype(bf16)], axis=-1)
        act = _silu(_dot(xb, wgb[...])) * _dot(xb, wub[...])
        o_ref[...] = _pack_halves(_dot(act.astype(bf16), wdb[...]))

    @pl.when(i >= n_used)
    def _():
        o_ref[...] = jnp.zeros_like(o_ref)


def _expert_mlp(block_e, n_used, xs, wg, wu, wd, layer, nb):
    last = lambda i, nu: jnp.minimum(i, nu[0] - 1)
    wspec = lambda shape: pl.BlockSpec(shape, lambda i, be, nu: (layer, be[last(i, nu)], 0, 0))
    grid_spec = pltpu.PrefetchScalarGridSpec(
        num_scalar_prefetch=2,
        grid=(nb,),
        in_specs=[
            pl.BlockSpec((BLOCK_TILE_ROWS, HALF), lambda i, be, nu: (last(i, nu), 0)),
            wspec((1, 1, D_MODEL, D_EXPERT)), wspec((1, 1, D_MODEL, D_EXPERT)), wspec((1, 1, D_EXPERT, D_MODEL)),
        ],
        out_specs=pl.BlockSpec((BLOCK_TILE_ROWS, HALF), lambda i, be, nu: (i, 0)),
        scratch_shapes=[pltpu.VMEM((D_MODEL, D_EXPERT), bf16), pltpu.VMEM((D_MODEL, D_EXPERT), bf16),
                        pltpu.VMEM((D_EXPERT, D_MODEL), bf16)],
    )
    return pl.pallas_call(
        _expert_kernel,
        grid_spec=grid_spec,
        out_shape=jax.ShapeDtypeStruct((nb * BLOCK_TILE_ROWS, HALF), jnp.uint32),
        compiler_params=_cparams(1),
        name="expert_mlp",
    )(block_e, n_used, xs, wg, wu, wd)


COMB_TM = 128


def _combine_kernel(idx_ref, idxn_ref, wts_ref, h_ref, hb_ref, sg_ref, su_ref, sd_ref, g_ref, b_ref, ys_hbm,
                    hf_ref, hb_out_ref, buf, sem):
    i = pl.program_id(0)
    nb = pl.num_programs(0)
    slot = i % 2
    cnt = COMB_TM * TOP_K

    @pl.when(i == 0)
    def _():
        _start_gather(ys_hbm, idx_ref, buf, sem, 0, cnt)

    @pl.when(i + 1 < nb)
    def _():
        _start_gather(ys_hbm, idxn_ref, buf, sem, 1 - slot, cnt)

    pltpu.make_async_copy(buf.at[slot], buf.at[slot], sem.at[slot]).wait()
    wts = wts_ref[...]
    groups = COMB_TM // GATHER_UNROLL
    routed = None
    for k in range(TOP_K):
        y_lo, y_hi = _unpack_halves(buf[slot, pl.ds(k * groups, groups)].reshape(COMB_TM, HALF))
        term = jnp.concatenate([y_lo, y_hi], axis=-1) * wts[:, k:k + 1]
        routed = term if routed is None else routed + term
    hb = hb_ref[...]
    act = _silu(_dot(hb, sg_ref[...])) * _dot(hb, su_ref[...])
    shared = _dot(act.astype(bf16), sd_ref[...])
    hn = _layer_norm(ALPHA * h_ref[...] + (routed + shared), g_ref[...], b_ref[...])
    hf_ref[...] = hn
    hb_out_ref[...] = hn.astype(bf16)


def _combine(pos3, wts, h, h_bf, ws_gate, ws_up, ws_down, g, b, ys2d):
    n = h.shape[0]
    tm = COMB_TM
    nb = n // tm
    cnt = tm * TOP_K
    rowspec = pl.BlockSpec((tm, D_MODEL), lambda i: (i, 0))
    const2 = lambda i: (0, 0)
    return pl.pallas_call(
        _combine_kernel,
        grid=(nb,),
        in_specs=[
            pl.BlockSpec((1, 1, cnt), lambda i: (i, 0, 0), memory_space=pltpu.SMEM),
            pl.BlockSpec((1, 1, cnt), lambda i: (jnp.minimum(i + 1, nb - 1), 0, 0), memory_space=pltpu.SMEM),
            pl.BlockSpec((tm, TOP_K), lambda i: (i, 0)),
            rowspec, rowspec,
            pl.BlockSpec((D_MODEL, D_SHARED), const2),
            pl.BlockSpec((D_MODEL, D_SHARED), const2),
            pl.BlockSpec((D_SHARED, D_MODEL), const2),
            pl.BlockSpec((1, D_MODEL), const2),
            pl.BlockSpec((1, D_MODEL), const2),
            pl.BlockSpec(memory_space=pl.ANY),
        ],
        out_specs=[rowspec, rowspec],
        out_shape=[jax.ShapeDtypeStruct((n, D_MODEL), f32),
                   jax.ShapeDtypeStruct((n, D_MODEL), bf16)],
        scratch_shapes=[pltpu.VMEM((2, cnt // GATHER_UNROLL, GATHER_UNROLL, HALF), jnp.uint32),
                        pltpu.SemaphoreType.DMA((2,))],
        compiler_params=_cparams(1),
        name="moe_combine",
    )(pos3, pos3, wts, h, h_bf, ws_gate, ws_up, ws_down, g, b, ys2d)


def _dispatch(eidx, rank, counts_f, nb):
    n = eidx.shape[0]
    counts = counts_f[0, :N_EXPERTS].astype(jnp.int32)
    padded = (counts + EXPERT_ROWS - 1) // EXPERT_ROWS * EXPERT_ROWS
    pend = jnp.cumsum(padded)
    pstart = pend - padded
    starts = jnp.arange(nb, dtype=jnp.int32) * EXPERT_ROWS
    block_e = jnp.minimum(jnp.sum((pend[None, :] <= starts[:, None]).astype(jnp.int32), axis=1), N_EXPERTS - 1)
    n_used = (pend[-1] // EXPERT_ROWS).astype(jnp.int32).reshape(1)
    onehot = eidx[:, :, None] == jnp.arange(N_EXPERTS, dtype=jnp.int32)[None, None, :]
    pos = jnp.sum(jnp.where(onehot, pstart[None, None, :], 0), axis=-1) + rank
    pos_tok3 = pos.reshape(n // SCATTER_TM, 1, SCATTER_TM * TOP_K)
    tm = COMB_TM
    pos_k3 = pos.reshape(n // tm, tm, TOP_K).transpose(0, 2, 1).reshape(n // tm, 1, tm * TOP_K)
    return block_e.astype(jnp.int32), n_used, pstart + counts, pend, pos_tok3, pos_k3


def _consts():
    r = jnp.arange(CHUNK)
    tri = (r[:, None] >= r[None, :]).astype(bf16)
    trit = (r[:, None] <= r[None, :]).astype(bf16)
    lane = jnp.arange(LANES)
    ch = jnp.arange(SSD_HEADS * SSD_HEADDIM)
    ssd_expand = (lane[:, None] == (ch[None, :] // SSD_HEADDIM)).astype(bf16)
    hc = jnp.arange(HGRN_HEADS * HGRN_DIM)
    hm_rows = jnp.arange(HGRN_HEADS * SUB)
    hmask = ((hm_rows[:, None] // SUB) == (hc[None, :] // HGRN_DIM)).astype(f32)
    qr = jnp.arange((HGRN_NBLK - 1) * HGRN_HEADS * SUB)
    qblk = qr // (HGRN_HEADS * SUB) + 1
    kstart = SUB * jnp.arange(HGRN_NBLK - 1) * jnp.arange(1, HGRN_NBLK) // 2
    kblk = jnp.sum(lane[None, :] >= kstart[:, None], axis=0)
    offmask = ((qblk[:, None] == kblk[None, :]) & (lane[None, :] < HGRN_OFF_KEYS)).astype(f32)
    rt = jnp.arange(ROUTER_TM)
    ltri = (rt[:, None] > rt[None, :]).astype(bf16)
    return tri, trit, ssd_expand, hmask, offmask, ltri


def _hgrn_lower_bounds(lb_param):
    sm = jax.nn.softmax(lb_param.astype(f32), axis=0)
    return jnp.cumsum(sm, axis=0) - sm[0:1]


def kernel(x, mem, w_in, conv_w, conv_b, dt_bias, a_log, d_skip, ssd_norm, pool_w, pool_scale, hgrn_lb, hgrn_norm, w_mem_kv, w_branch, w_o, ln1_g, ln1_b, w_router, router_bias, w_exp_gate, w_exp_up, w_exp_down, w_sh_gate, w_sh_up, w_sh_down, ln2_g, ln2_b):
    bsz, seq, d = x.shape
    n = bsz * seq
    nk = n * TOP_K
    nb = -(-nk // EXPERT_ROWS) + N_EXPERTS
    tri, trit, ssd_expand, hmask, offmask, ltri = _consts()
    lb_all = _hgrn_lower_bounds(hgrn_lb)

    c_dt0 = BRANCH_WIDTH * 2 + SSD_CONV_DIM
    c_dt1 = c_dt0 + SSD_HEADS
    c_gate = c_dt1 + 5 * BRANCH_WIDTH
    pad_lanes = LANES - SSD_HEADS

    h = x.reshape(n, d)
    h_bf = h.astype(bf16)
    mem_bf = mem.reshape(bsz * MEM_TOKENS, d).astype(bf16)

    w_t = jnp.swapaxes(w_in, 1, 2)
    w_a = w_t[:, :c_dt0, :].astype(bf16)
    w_b = w_t[:, c_dt1:, :].astype(bf16)
    w_dt = jnp.pad(w_t[:, c_dt0:c_dt1, :], ((0, 0), (0, pad_lanes), (0, 0))).astype(bf16)
    w_kv_bf = w_mem_kv.astype(bf16)
    n_b = c_gate - c_dt1

    for l in range(DEPTH):
        proj = _matmul(h_bf, w_a, l, f32, 512, 1024, transposed=True)
        proj_b = _matmul(h_bf, w_b, l, f32, 512, 1024, 0, n_b, transposed=True)
        dtp = _matmul(h_bf, w_dt, l, f32, 512, LANES, transposed=True)
        kv = _matmul(mem_bf, w_kv_bf, l, bf16, 512, 1024)

        br_pool = _pool_mixer(proj, pool_w[l].astype(bf16), pool_scale[l].reshape(1, -1), bsz, seq)
        br_ssd = _ssd_mixer(
            proj, dtp, conv_w[l], conv_b[l].reshape(1, -1),
            jnp.pad(dt_bias[l], (0, pad_lanes)).reshape(1, -1),
            jnp.pad(a_log[l], (0, pad_lanes)).reshape(1, -1),
            jnp.repeat(d_skip[l], SSD_HEADDIM).reshape(1, -1),
            ssd_norm[l].reshape(1, -1), (tri, trit, ssd_expand), bsz, seq)
        br_hgrn = _hgrn_mixer(proj_b, lb_all[l].reshape(1, -1), hgrn_norm[l].reshape(1, -1),
                              (tri, hmask, offmask), bsz, seq)
        br_mem = _mem_attention(proj_b, kv, bsz, seq)

        merged = _merge(h_bf, (br_pool, br_ssd, br_hgrn, br_mem), w_b, l, n_b, w_branch[l].astype(bf16))
        h, h_bf, h_pk = _wo_ln(merged, w_o[l].astype(bf16), h, ln1_g[l].reshape(1, -1), ln1_b[l].reshape(1, -1))

        eidx, wts, rank, counts = _router(h, jnp.pad(w_router[l], ((0, 0), (0, LANES - N_EXPERTS))),
                                          jnp.pad(router_bias[l], (0, LANES - N_EXPERTS)).reshape(1, -1), ltri)
        block_e, n_used, vend, pend, pos_tok3, pos_k3 = _dispatch(eidx, rank, counts, nb)
        xs = _scatter_rows(vend, pend, pos_tok3, h_pk, nb)
        ys = _expert_mlp(block_e, n_used, xs, w_exp_gate, w_exp_up, w_exp_down, l, nb)
        h, h_bf = _combine(pos_k3, wts, h, h_bf, w_sh_gate[l].astype(bf16), w_sh_up[l].astype(bf16),
                                   w_sh_down[l].astype(bf16), ln2_g[l].reshape(1, -1), ln2_b[l].reshape(1, -1), ys)
    return h.reshape(bsz, seq, d)
```
